```python
import jax
import jax.numpy as jnp
from jax import lax
import numpy as np

D_MODEL = 1024
BATCH = 16
SEQ = 2048
DEPTH = 4

GRID_W = 64
CTX_LEN = 256
N_MIXERS = 3
N_LRU_LAYERS = (DEPTH + 2) // 3
N_NA_LAYERS = (DEPTH + 1) // 3
N_RET_LAYERS = DEPTH // 3
N_MOD = 6
ADA_INIT = 0.5
RMS_EPS = 1e-6

D_RNN = D_MODEL
LRU_BLOCKS = 8
LRU_BLOCK_W = D_RNN // LRU_BLOCKS
CONV_W = 4
CONV_LEFT = CONV_W // 2
LRU_C = 8.0

NA_HEADS = 16
NA_HEAD_DIM = D_MODEL // NA_HEADS
NA_ROWS = 8
NA_COLS = 16
NA_QCOLS = 16
NA_KCOLS = 32
NEG_INF = -1e30

RET_HEADS = 4
RET_QK_DIM = D_MODEL // RET_HEADS
RET_V_DIM = 2 * D_MODEL // RET_HEADS
RET_CHUNK = 128
ROPE_BASE = 10000.0

N_EXPERTS = 16
N_GROUPS = 4
EXPERTS_PER_GROUP = N_EXPERTS // N_GROUPS
TOP_K = 2
D_EXPERT = 1024
MOE_BLOCK = 512

kernel_name = "hybrid_lru_na_retention_grouped_moe_dit"


def rmsnorm(x, g):
    x32 = x.astype(jnp.float32)
    y = x32 * lax.rsqrt(jnp.mean(x32 * x32, axis=-1, keepdims=True) + RMS_EPS)
    return (y * g.astype(jnp.float32)).astype(x.dtype)


def modulate(h, shift, scale):
    return h * (1 + scale) + shift


def axial_rope(t):
    L, hd = t.shape[1], t.shape[-1]
    quarter = hd // 4
    pos = jnp.arange(L)
    inv = ROPE_BASE ** (-jnp.arange(quarter, dtype=jnp.float32) / quarter)

    def rot(u, p):
        ang = p.astype(jnp.float32)[:, None] * inv
        cos = jnp.cos(ang)[None, :, None, :]
        sin = jnp.sin(ang)[None, :, None, :]
        u1, u2 = jnp.split(u, 2, axis=-1)
        return jnp.concatenate([u1 * cos - u2 * sin, u1 * sin + u2 * cos], axis=-1)

    half = hd // 2
    return jnp.concatenate([rot(t[..., :half], pos // GRID_W), rot(t[..., half:], pos % GRID_W)], axis=-1)


def depthwise_conv_centred(u, w, b):
    C = u.shape[-1]
    out = lax.conv_general_dilated(
        u, w.astype(u.dtype)[:, None, :], window_strides=(1,),
        padding=[(CONV_LEFT, CONV_W - 1 - CONV_LEFT)],
        dimension_numbers=('NWC', 'WIO', 'NWC'), feature_group_count=C)
    return out + b


def block_diag(u, w):
    ub = u.reshape(u.shape[:-1] + (LRU_BLOCKS, LRU_BLOCK_W))
    return jnp.einsum('...nc,ncd->...nd', ub, w.astype(jnp.float32)).reshape(u.shape)


def rglru_coeffs(u, w_r, b_r, w_i, b_i, lam):
    u32 = u.astype(jnp.float32)
    r = jax.nn.sigmoid(block_diag(u32, w_r) + b_r.astype(jnp.float32))
    i = jax.nn.sigmoid(block_diag(u32, w_i) + b_i.astype(jnp.float32))
    log_a = -LRU_C * r * jax.nn.softplus(-lam.astype(jnp.float32))
    a = jnp.exp(log_a)
    b = jnp.sqrt(-jnp.expm1(2.0 * log_a)) * (i * u32)
    return a, b


def linear_scan(a, b, h0):
    b = b.at[:, 0].add(a[:, 0] * h0)

    def combine(left, right):
        a_l, b_l = left
        a_r, b_r = right
        return a_l * a_r, a_r * b_l + b_r

    return lax.associative_scan(combine, (a, b), axis=1)[1]


def rglru_mixer(hz, hx, w_in, conv_w, conv_b, gate_w, gate_b, lam, w_out, ctx_out):
    B = hx.shape[0]

    def rnn_input(h):
        return depthwise_conv_centred(h @ w_in[:, D_RNN:], conv_w, conv_b)

    def gate_branch(h):
        return jax.nn.gelu(h @ w_in[:, :D_RNN])

    uz, ux = rnn_input(hz), rnn_input(hx)
    hx_sum = jnp.zeros(ux.shape, jnp.float32)
    hz_sum = jnp.zeros(uz.shape, jnp.float32)
    for d in range(2):
        az, bz = rglru_coeffs(uz, gate_w[d, 0], gate_b[d, 0], gate_w[d, 1], gate_b[d, 1], lam[d])
        ax, bx = rglru_coeffs(ux, gate_w[d, 0], gate_b[d, 0], gate_w[d, 1], gate_b[d, 1], lam[d])
        if d == 1:
            az, bz, ax, bx = jnp.flip(az, 1), jnp.flip(bz, 1), jnp.flip(ax, 1), jnp.flip(bx, 1)
        hz_d = linear_scan(az, bz, jnp.zeros((B, D_RNN), jnp.float32))
        hx_d = linear_scan(ax, bx, hz_d[:, -1])
        if d == 1:
            hz_d, hx_d = jnp.flip(hz_d, 1), jnp.flip(hx_d, 1)
        hx_sum = hx_sum + hx_d
        if ctx_out:
            hz_sum = hz_sum + hz_d
    yx = (hx_sum.astype(hx.dtype) * gate_branch(hx)) @ w_out
    yz = (hz_sum.astype(hz.dtype) * gate_branch(hz)) @ w_out if ctx_out else None
    return yz, yx


def neighbourhood_attention_mixer(hz, hx, w_qkv, rpb, w_o, ctx_out):
    B, S, D = hx.shape
    rows = S // GRID_W
    kr = min(NA_ROWS, rows)
    nj = GRID_W // NA_QCOLS

    def heads(h):
        q, k, v = jnp.split(h @ w_qkv, 3, axis=-1)
        shp = h.shape[:2] + (NA_HEADS, NA_HEAD_DIM)
        return q.reshape(shp) * (NA_HEAD_DIM ** -0.5), k.reshape(shp), v.reshape(shp)

    qz, kz, vz = heads(hz)
    qx, kx, vx = heads(hx)
    grid = (B, rows, GRID_W, NA_HEADS, NA_HEAD_DIM)
    qg, kg, vg = qx.reshape(grid), kx.reshape(grid), vx.reshape(grid)

    col = jnp.arange(GRID_W)
    q_cols = col.reshape(nj, NA_QCOLS)
    win_start = jnp.clip(col - NA_COLS // 2, 0, GRID_W - NA_COLS).reshape(nj, NA_QCOLS)
    k_cols = (jnp.clip(jnp.arange(nj) * NA_QCOLS - NA_COLS // 2, 0, GRID_W - NA_KCOLS)[:, None]
              + jnp.arange(NA_KCOLS))
    in_win = ((k_cols[:, None, :] >= win_start[:, :, None])
              & (k_cols[:, None, :] < win_start[:, :, None] + NA_COLS))
    rel_c_idx = jnp.clip(k_cols[:, None, :] - q_cols[:, :, None] + NA_COLS - 1, 0, 2 * NA_COLS - 2)
    rpb32 = rpb.astype(jnp.float32)
    n_loc = kr * NA_KCOLS

    def row_block(r):
        r0 = jnp.clip(r - kr // 2, 0, rows - kr)
        q_r = lax.dynamic_index_in_dim(qg, r, axis=1, keepdims=False)
        q_r = q_r.reshape(B, nj, NA_QCOLS, NA_HEADS, NA_HEAD_DIM)
        k_b = lax.dynamic_slice_in_dim(kg, r0, kr, axis=1)[:, :, k_cols]
        v_b = lax.dynamic_slice_in_dim(vg, r0, kr, axis=1)[:, :, k_cols]
        rel_r_idx = r0 + jnp.arange(kr) - r + NA_ROWS - 1
        bias = rpb32[:, rel_r_idx][:, :, rel_c_idx].transpose(0, 2, 3, 1, 4)
        s_loc = jnp.einsum('bjqhd,brjkhd->bhjqrk', q_r, k_b).astype(jnp.float32) + bias
        s_loc = jnp.where(in_win[:, :, None, :], s_loc, NEG_INF)
        s_ctx = jnp.einsum('bjqhd,bchd->bhjqc', q_r, kz).astype(jnp.float32)
        s = jnp.concatenate([s_loc.reshape(B, NA_HEADS, nj, NA_QCOLS, n_loc), s_ctx], axis=-1)
        p = jax.nn.softmax(s, axis=-1).astype(vx.dtype)
        p_loc = p[..., :n_loc].reshape(B, NA_HEADS, nj, NA_QCOLS, kr, NA_KCOLS)
        o = (jnp.einsum('bhjqrk,brjkhd->bjqhd', p_loc, v_b)
             + jnp.einsum('bhjqc,bchd->bjqhd', p[..., n_loc:], vz))
        return o.reshape(B, GRID_W, D)

    o = lax.map(row_block, jnp.arange(rows))
    yx = o.transpose(1, 0, 2, 3).reshape(B, S, D) @ w_o
    yz = None
    if ctx_out:
        s_zz = jnp.einsum('bqhd,bkhd->bhqk', qz, kz).astype(jnp.float32)
        p_zz = jax.nn.softmax(s_zz, axis=-1).astype(vz.dtype)
        yz = jnp.einsum('bhqk,bkhd->bqhd', p_zz, vz).reshape(hz.shape) @ w_o
    return yz, yx


def retention_scan(q, k, v, log_gamma, s0, include_diag):
    B, L, H, _ = q.shape
    dv = v.shape[-1]
    nc = L // RET_CHUNK

    def chunks(t):
        return t.reshape(B, nc, RET_CHUNK, H, t.shape[-1]).transpose(1, 0, 3, 2, 4)

    pos = jnp.arange(RET_CHUNK, dtype=jnp.float32)
    diff = pos[:, None] - pos[None, :]
    mask = diff >= 0 if include_diag else diff > 0
    decay = jnp.where(mask, jnp.exp(jnp.maximum(diff, 0.0) * log_gamma[:, None, None]), 0.0)
    q_decay = jnp.exp((pos + 1.0) * log_gamma[:, None])[..., None]
    k_decay = jnp.exp((RET_CHUNK - 1.0 - pos) * log_gamma[:, None])[..., None]
    chunk_decay = jnp.exp(RET_CHUNK * log_gamma)[:, None, None]

    def step(state, qkv):
        qc, kc, vc = qkv
        inner = jnp.einsum('bhqd,bhkd->bhqk', qc, kc) * decay
        o = (jnp.einsum('bhqk,bhkv->bhqv', inner, vc)
             + jnp.einsum('bhqd,bhdv->bhqv', qc * q_decay, state))
        state = state * chunk_decay + jnp.einsum('bhkd,bhkv->bhdv', kc * k_decay, vc)
        return state, o

    state, o = lax.scan(step, s0, (chunks(q), chunks(k), chunks(v)))
    return o.transpose(1, 0, 3, 2, 4).reshape(B, L, H, dv), state


def retention_mixer(hz, hx, w_qkvg, w_o, ctx_out):
    B = hx.shape[0]
    D = D_MODEL
    log_gamma = jnp.log1p(-(2.0 ** (-5.0 - jnp.arange(RET_HEADS, dtype=jnp.float32))))

    def project(h, rotate):
        L = h.shape[1]
        q, k, v, g = jnp.split(h @ w_qkvg, [D, 2 * D, 4 * D], axis=-1)
        q = q.reshape(B, L, RET_HEADS, RET_QK_DIM).astype(jnp.float32)
        k = k.reshape(B, L, RET_HEADS, RET_QK_DIM).astype(jnp.float32) * (RET_QK_DIM ** -0.5)
        v = v.reshape(B, L, RET_HEADS, RET_V_DIM).astype(jnp.float32)
        if rotate:
            q, k = axial_rope(q), axial_rope(k)
        return q, k, v, g

    def finish(o, g):
        o = o * lax.rsqrt(jnp.mean(o * o, axis=-1, keepdims=True) + RMS_EPS)
        o = o.reshape(g.shape).astype(g.dtype) * jax.nn.silu(g)
        return o @ w_o

    flip = lambda t: jnp.flip(t, 1)
    qz, kz, vz, gz = project(hz, False)
    qx, kx, vx, gx = project(hx, True)
    s0 = jnp.zeros((B, RET_HEADS, RET_QK_DIM, RET_V_DIM), jnp.float32)
    oz_f, sz_f = retention_scan(qz, kz, vz, log_gamma, s0, True)
    oz_b, sz_b = retention_scan(flip(qz), flip(kz), flip(vz), log_gamma, s0, False)
    ox_f, _ = retention_scan(qx, kx, vx, log_gamma, sz_f, True)
    ox_b, _ = retention_scan(flip(qx), flip(kx), flip(vx), log_gamma, sz_b, False)
    yx = finish(ox_f + flip(ox_b), gx)
    yz = finish(oz_f + flip(oz_b), gz) if ctx_out else None
    return yz, yx


def moe_ffn(h, router_w, router_b, w_gate, w_up, w_down):
    N, D = h.shape
    probs = jax.nn.softmax(h.astype(jnp.float32) @ router_w.astype(jnp.float32), axis=-1)
    sel = (probs + router_b.astype(jnp.float32)).reshape(N, N_GROUPS, EXPERTS_PER_GROUP)
    group = jnp.argmax(lax.top_k(sel, TOP_K)[0].sum(-1), axis=-1)
    in_group = jnp.take_along_axis(sel, group[:, None, None], axis=1)[:, 0]
    _, local = lax.top_k(in_group, TOP_K)
    expert = group[:, None] * EXPERTS_PER_GROUP + local
    weight = jnp.take_along_axis(probs, expert, axis=-1)
    weight = weight / jnp.sum(weight, axis=-1, keepdims=True)

    n_assign = N * TOP_K
    flat_e = expert.reshape(-1)
    flat_t = jnp.repeat(jnp.arange(N, dtype=jnp.int32), TOP_K)
    order = jnp.argsort(flat_e)
    e_sorted = flat_e[order]
    counts = jnp.bincount(flat_e, length=N_EXPERTS)
    padded = (counts + MOE_BLOCK - 1) // MOE_BLOCK * MOE_BLOCK
    pad_end = jnp.cumsum(padded)
    pad_start = pad_end - padded
    start = jnp.cumsum(counts) - counts
    dest = pad_start[e_sorted] + jnp.arange(n_assign) - start[e_sorted]
    n_blocks = n_assign // MOE_BLOCK + N_EXPERTS
    n_rows = n_blocks * MOE_BLOCK
    src = jnp.full((n_rows,), N, jnp.int32).at[dest].set(flat_t[order])
    gate = jnp.zeros((n_rows,), jnp.float32).at[dest].set(weight.reshape(-1)[order])
    block_expert = jnp.minimum(
        jnp.searchsorted(pad_end, jnp.arange(n_blocks) * MOE_BLOCK, side='right'), N_EXPERTS - 1)
    h_pad = jnp.concatenate([h, jnp.zeros((1, D), h.dtype)], axis=0)
    xs = h_pad[src].reshape(n_blocks, MOE_BLOCK, D)

    def expert_block(args):
        xb, e = args
        a = jax.nn.silu(xb @ w_gate[e]) * (xb @ w_up[e])
        return a @ w_down[e]

    ys = lax.map(expert_block, (xs, block_expert)).reshape(n_rows, D).astype(jnp.float32)
    out = jnp.zeros((N + 1, D), jnp.float32).at[src].add(ys * gate[:, None])[:N]
    return out.astype(h.dtype)


def setup_inputs(seed: int = 0) -> dict:
    key = jax.random.key(seed)
    keys = iter(jax.random.split(key, 32))
    D = D_MODEL

    def normal(shape, scale):
        return jax.random.normal(next(keys), shape, jnp.float32) * scale

    lam_p = jax.random.uniform(next(keys), (N_LRU_LAYERS, 2, D_RNN), jnp.float32, 0.9, 0.999) ** (1.0 / LRU_C)
    return {
        "x": normal((BATCH, SEQ, D), 1.0),
        "c": normal((BATCH, D), 1.0),
        "ctx": normal((BATCH, CTX_LEN, D), 1.0),
        "c_ctx": normal((D,), 1.0),
        "ada_w": normal((DEPTH, D, N_MOD * D), ADA_INIT * D ** -0.5),
        "ada_b": normal((DEPTH, N_MOD * D), 0.02),
        "norm_mix_g": 1.0 + normal((DEPTH, D), 0.02),
        "norm_ffn_g": 1.0 + normal((DEPTH, D), 0.02),
        "final_norm_g": 1.0 + normal((D,), 0.02),
        "lru_w_in": normal((N_LRU_LAYERS, D, 2 * D_RNN), D ** -0.5),
        "lru_conv_w": normal((N_LRU_LAYERS, CONV_W, D_RNN), CONV_W ** -0.5),
        "lru_conv_b": normal((N_LRU_LAYERS, D_RNN), 0.02),
        "lru_gate_w": normal((N_LRU_LAYERS, 2, 2, LRU_BLOCKS, LRU_BLOCK_W, LRU_BLOCK_W), LRU_BLOCK_W ** -0.5),
        "lru_gate_b": normal((N_LRU_LAYERS, 2, 2, D_RNN), 0.02),
        "lru_lambda": jnp.log(lam_p) - jnp.log1p(-lam_p),
        "lru_w_out": normal((N_LRU_LAYERS, D_RNN, D), D_RNN ** -0.5),
        "na_w_qkv": normal((N_NA_LAYERS, D, 3 * D), D ** -0.5),
        "na_rpb": normal((N_NA_LAYERS, NA_HEADS, 2 * NA_ROWS - 1, 2 * NA_COLS - 1), 0.02),
        "na_w_o": normal((N_NA_LAYERS, D, D), D ** -0.5),
        "ret_w_qkvg": normal((N_RET_LAYERS, D, 6 * D), D ** -0.5),
        "ret_w_o": normal((N_RET_LAYERS, 2 * D, D), (2 * D) ** -0.5),
        "router_w": normal((D, N_EXPERTS), D ** -0.5),
        "router_b": normal((N_EXPERTS,), 0.01),
        "moe_w_gate": normal((DEPTH, N_EXPERTS, D, D_EXPERT), D ** -0.5),
        "moe_w_up": normal((DEPTH, N_EXPERTS, D, D_EXPERT), D ** -0.5),
        "moe_w_down": normal((DEPTH, N_EXPERTS, D_EXPERT, D), D_EXPERT ** -0.5),
    }


def reference(x, c, ctx, c_ctx, ada_w, ada_b, norm_mix_g, norm_ffn_g, final_norm_g,
              lru_w_in, lru_conv_w, lru_conv_b, lru_gate_w, lru_gate_b, lru_lambda, lru_w_out,
              na_w_qkv, na_rpb, na_w_o, ret_w_qkvg, ret_w_o,
              router_w, router_b, moe_w_gate, moe_w_up, moe_w_down):
    B, S, D = x.shape
    C = ctx.shape[1]
    z = ctx
    silu_c = jax.nn.silu(c)
    silu_cc = jax.nn.silu(c_ctx)
    for i in range(DEPTH):
        kind, j = i % N_MIXERS, i // N_MIXERS
        need_ctx = i < DEPTH - 1
        mx = [m[:, None, :] for m in jnp.split(silu_c @ ada_w[i] + ada_b[i], N_MOD, axis=-1)]
        mz = jnp.split(silu_cc @ ada_w[i] + ada_b[i], N_MOD, axis=-1)

        hx = modulate(rmsnorm(x, norm_mix_g[i]), mx[0], mx[1])
        hz = modulate(rmsnorm(z, norm_mix_g[i]), mz[0], mz[1])
        if kind == 0:
            yz, yx = rglru_mixer(hz, hx, lru_w_in[j], lru_conv_w[j], lru_conv_b[j], lru_gate_w[j],
                                 lru_gate_b[j], lru_lambda[j], lru_w_out[j], need_ctx)
        elif kind == 1:
            yz, yx = neighbourhood_attention_mixer(hz, hx, na_w_qkv[j], na_rpb[j], na_w_o[j], need_ctx)
        else:
            yz, yx = retention_mixer(hz, hx, ret_w_qkvg[j], ret_w_o[j], need_ctx)
        x = x + mx[2] * yx

        hx2 = modulate(rmsnorm(x, norm_ffn_g[i]), mx[3], mx[4])
        if need_ctx:
            z = z + mz[2] * yz
            hz2 = modulate(rmsnorm(z, norm_ffn_g[i]), mz[3], mz[4])
            tokens = jnp.concatenate([hz2.reshape(B * C, D), hx2.reshape(B * S, D)], axis=0)
            f = moe_ffn(tokens, router_w, router_b, moe_w_gate[i], moe_w_up[i], moe_w_down[i])
            z = z + mz[5] * f[:B * C].reshape(B, C, D)
            x = x + mx[5] * f[B * C:].reshape(B, S, D)
        else:
            f = moe_ffn(hx2.reshape(B * S, D), router_w, router_b, moe_w_gate[i], moe_w_up[i], moe_w_down[i])
            x = x + mx[5] * f.reshape(B, S, D)
    return rmsnorm(x, final_norm_g)
```

```python
import functools

import jax
import jax.numpy as jnp
from jax import lax
from jax.experimental import pallas as pl
from jax.experimental.pallas import tpu as pltpu

F32 = jnp.float32
BF16 = jnp.bfloat16

GRID_W = 64
N_MOD = 6
RMS_EPS = 1e-6
LRU_BLOCKS = 8
CONV_W = 4
CONV_LEFT = CONV_W // 2
LRU_C = 8.0
NA_HEADS = 16
NA_ROWS = 8
NA_COLS = 16
NA_QROWS = 4
NEG_INF = -1e30
RET_HEADS = 4
ROPE_BASE = 10000.0
N_EXPERTS = 16
N_GROUPS = 4
EXPERTS_PER_GROUP = N_EXPERTS // N_GROUPS
TOP_K = 2
MOE_BLOCK = 512
LANES = 128
ROUTER_LANES = 128
SCAN_TILE = 256
VMEM_LIMIT = 56 * 1024 * 1024


def _pick(n, candidates):
    for c in candidates:
        if n % c == 0:
            return c
    raise ValueError(f"no tile in {candidates} divides {n}")


def _params(sem):
    return pltpu.CompilerParams(dimension_semantics=sem, vmem_limit_bytes=VMEM_LIMIT)


def _dot(a, b):
    return jnp.dot(a, b, preferred_element_type=F32)


def _dot_nt(a, b):
    return lax.dot_general(a, b, (((1,), (1,)), ((), ())), preferred_element_type=F32)


def _dot_tn(a, b):
    return lax.dot_general(a, b, (((0,), (0,)), ((), ())), preferred_element_type=F32)


def _is_ctx(pos0, tm, n_ctx):
    return (pos0 + lax.broadcasted_iota(jnp.int32, (tm, 1), 0)) < n_ctx


def _mod_row(mod_ref, is_ctx, k):
    return jnp.where(is_ctx, mod_ref[0, k:k + 1, :], mod_ref[0, N_MOD + k:N_MOD + k + 1, :])


def _norm_mod(x, g, mod_ref, is_ctx, k):
    y = x * lax.rsqrt(jnp.mean(x * x, axis=-1, keepdims=True) + RMS_EPS) * g
    return y * (1.0 + _mod_row(mod_ref, is_ctx, k + 1)) + _mod_row(mod_ref, is_ctx, k)


def _ada_kernel(cc_ref, w_ref, b_ref, o_ref):
    cc = cc_ref[...]
    s = (cc * jax.nn.sigmoid(cc)).astype(BF16)
    o_ref[0] = _dot(s, w_ref[0].astype(BF16)) + b_ref[0]


def _ada(cc, ada_w, ada_b):
    depth, d, n = ada_w.shape
    rows = cc.shape[0]
    tn = _pick(n, (1536, 1024, 512, 256, 128))
    return pl.pallas_call(
        _ada_kernel,
        grid=(depth, n // tn),
        in_specs=[pl.BlockSpec((rows, d), lambda i, j: (0, 0)),
                  pl.BlockSpec((1, d, tn), lambda i, j: (i, 0, j)),
                  pl.BlockSpec((1, 1, tn), lambda i, j: (i, 0, j))],
        out_specs=pl.BlockSpec((1, rows, tn), lambda i, j: (i, 0, j)),
        out_shape=jax.ShapeDtypeStruct((depth, rows, n), F32),
        compiler_params=_params(("parallel", "parallel")),
        name="ada_mod",
    )(cc, ada_w, ada_b.reshape(depth, 1, n))


def _proj_kernel(x_ref, g_ref, mod_ref, w_ref, o_ref, h_ref, *, tm, n_ctx, act_tiles):
    i, j = pl.program_id(1), pl.program_id(2)

    @pl.when(j == 0)
    def _():
        is_ctx = _is_ctx(i * tm, tm, n_ctx)
        h_ref[...] = _norm_mod(x_ref[0], g_ref[...], mod_ref, is_ctx, 0).astype(BF16)

    y = _dot(h_ref[...], w_ref[...])
    if act_tiles:
        @pl.when(j < act_tiles)
        def _():
            o_ref[0] = jax.nn.gelu(y).astype(o_ref.dtype)

        @pl.when(j >= act_tiles)
        def _():
            o_ref[0] = y.astype(o_ref.dtype)
    else:
        o_ref[0] = y.astype(o_ref.dtype)


def _norm_proj(t, g, mod, w, n_ctx, act_cols=0):
    b, p, d = t.shape
    n = w.shape[1]
    tm = _pick(p, (768, 512, 256))
    tn = _pick(n, (1024, 512))
    assert act_cols % tn == 0
    kern = functools.partial(_proj_kernel, tm=tm, n_ctx=n_ctx, act_tiles=act_cols // tn)
    return pl.pallas_call(
        kern,
        grid=(b, p // tm, n // tn),
        in_specs=[pl.BlockSpec((1, tm, d), lambda bi, i, j: (bi, i, 0)),
                  pl.BlockSpec((1, d), lambda bi, i, j: (0, 0)),
                  pl.BlockSpec((1, 2 * N_MOD, d), lambda bi, i, j: (bi, 0, 0)),
                  pl.BlockSpec((d, tn), lambda bi, i, j: (0, j))],
        out_specs=pl.BlockSpec((1, tm, tn), lambda bi, i, j: (bi, i, j)),
        out_shape=jax.ShapeDtypeStruct((b, p, n), BF16),
        scratch_shapes=[pltpu.VMEM((tm, d), BF16)],
        compiler_params=_params(("parallel", "parallel", "arbitrary")),
        name="norm_proj",
    )(t, g.reshape(1, d), mod, w)


def _out_kernel(a_ref, w_ref, x_ref, mod_ref, g_ref, rw_ref, xo_ref, h_ref, p_ref, *, tm, n_ctx):
    is_ctx = _is_ctx(pl.program_id(1) * tm, tm, n_ctx)
    y = _dot(a_ref[0], w_ref[...])
    xn = x_ref[0] + _mod_row(mod_ref, is_ctx, 2) * y
    xo_ref[0] = xn
    h = _norm_mod(xn, g_ref[...], mod_ref, is_ctx, 3)
    hi = h.astype(BF16)
    h_ref[0] = hi
    lo = (h - hi.astype(F32)).astype(BF16)
    p_ref[0] = _dot(hi, rw_ref[...]) + _dot(lo, rw_ref[...])


def _out_proj(a, w, t, mod, g, rw, n_ctx):
    b, p, d = t.shape
    k = a.shape[-1]
    tm = _pick(p, (768, 512, 256))
    kern = functools.partial(_out_kernel, tm=tm, n_ctx=n_ctx)
    return pl.pallas_call(
        kern,
        grid=(b, p // tm),
        in_specs=[pl.BlockSpec((1, tm, k), lambda bi, i: (bi, i, 0)),
                  pl.BlockSpec((k, d), lambda bi, i: (0, 0)),
                  pl.BlockSpec((1, tm, d), lambda bi, i: (bi, i, 0)),
                  pl.BlockSpec((1, 2 * N_MOD, d), lambda bi, i: (bi, 0, 0)),
                  pl.BlockSpec((1, d), lambda bi, i: (0, 0)),
                  pl.BlockSpec((d, ROUTER_LANES), lambda bi, i: (0, 0))],
        out_specs=[pl.BlockSpec((1, tm, d), lambda bi, i: (bi, i, 0)),
                   pl.BlockSpec((1, tm, d), lambda bi, i: (bi, i, 0)),
                   pl.BlockSpec((1, tm, ROUTER_LANES), lambda bi, i: (bi, i, 0))],
        out_shape=[jax.ShapeDtypeStruct((b, p, d), F32),
                   jax.ShapeDtypeStruct((b, p, d), BF16),
                   jax.ShapeDtypeStruct((b, p, ROUTER_LANES), F32)],
        compiler_params=_params(("parallel", "parallel")),
        name="out_proj",
    )(a, w, t, mod, g.reshape(1, d), rw)


def _lru_kernel(gt_ref, up_ref, cw_ref, cb_ref, wg_ref, gb_ref, lam_ref, o_ref,
                upad, hf_s, hb_s, af, bf, ab, bb, *, n_ctx, tt, nt):
    p = nt * tt
    tc = o_ref.shape[-1]
    pad = 8
    upad[0:pad, :] = jnp.zeros((pad, tc), F32)
    upad[pad + p:pad + p + pad, :] = jnp.zeros((pad, tc), F32)
    for t in range(nt):
        upad[pad + t * tt:pad + (t + 1) * tt, :] = up_ref[0, t * tt:(t + 1) * tt, :].astype(F32)

    neg_lam = -lam_ref[...]
    sp = jnp.maximum(neg_lam, 0.0) + jnp.log1p(jnp.exp(-jnp.abs(neg_lam)))
    rowi = lax.broadcasted_iota(jnp.int32, (tt, 1), 0)

    def conv_tile(t):
        r0 = t * tt
        first = r0 in (0, n_ctx)
        last = r0 + tt in (n_ctx, p)
        acc = jnp.broadcast_to(cb_ref[...], (tt, tc))
        for kk in range(CONV_W):
            d = kk - CONV_LEFT
            xs = upad[pad + r0 + d:pad + r0 + d + tt, :]
            if first and d < 0:
                xs = jnp.where(rowi >= -d, xs, 0.0)
            if last and d > 0:
                xs = jnp.where(rowi < tt - d, xs, 0.0)
            acc = acc + cw_ref[kk:kk + 1, :] * xs
        return acc

    def gates(u, d, a_ref, b_ref):
        ub = u.astype(BF16)
        for n in range(tc // LANES):
            sl = slice(n * LANES, (n + 1) * LANES)
            ri = _dot(ub[:, sl], wg_ref[d, n])
            r = jax.nn.sigmoid(ri[:, :LANES] + gb_ref[d, 0:1, sl])
            ig = jax.nn.sigmoid(ri[:, LANES:] + gb_ref[d, 1:2, sl])
            log_a = (-LRU_C) * r * sp[d:d + 1, sl]
            a = jnp.exp(log_a)
            one_minus_a2 = -jnp.tanh(log_a) * (a * a + 1.0)
            a_ref[:, sl] = a
            b_ref[:, sl] = jnp.sqrt(one_minus_a2) * (ig * u[:, sl])

    nz = n_ctx // tt
    fwd_order = list(range(nt))
    bwd_order = list(range(nz - 1, -1, -1)) + list(range(nt - 1, nz - 1, -1))
    carry = (jnp.zeros((1, tc), F32), jnp.zeros((1, tc), F32))
    for tf, tb in zip(fwd_order, bwd_order):
        gates(conv_tile(tf), 0, af, bf)
        gates(conv_tile(tb), 1, ab, bb)

        def step(j, c, tf=tf, tb=tb):
            hf, hb = c
            hf = af[pl.ds(j, 1), :] * hf + bf[pl.ds(j, 1), :]
            hf_s[pl.ds(tf * tt + j, 1), :] = hf
            jb = tt - 1 - j
            hb = ab[pl.ds(jb, 1), :] * hb + bb[pl.ds(jb, 1), :]
            hb_s[pl.ds(tb * tt + jb, 1), :] = hb
            return hf, hb

        carry = lax.fori_loop(0, tt, step, carry, unroll=8)

    for t in range(nt):
        rs = slice(t * tt, (t + 1) * tt)
        o_ref[0, rs, :] = ((hf_s[rs, :] + hb_s[rs, :]) * gt_ref[0, rs, :].astype(F32)).astype(o_ref.dtype)


def _lru(proj, conv_w, conv_b, gate_w, gate_b, lam, n_ctx):
    b, p, d2 = proj.shape
    d = d2 // 2
    tc = 512
    tt = SCAN_TILE
    assert p % tt == 0 and n_ctx % tt == 0 and d % tc == 0 and d // LRU_BLOCKS == LANES
    nt = p // tt
    nct = d // tc
    wg = jnp.concatenate([gate_w[:, 0], gate_w[:, 1]], axis=-1).astype(BF16)
    kern = functools.partial(_lru_kernel, n_ctx=n_ctx, tt=tt, nt=nt)
    return pl.pallas_call(
        kern,
        grid=(b, nct),
        in_specs=[pl.BlockSpec((1, p, tc), lambda bi, ci: (bi, 0, ci)),
                  pl.BlockSpec((1, p, tc), lambda bi, ci: (bi, 0, nct + ci)),
                  pl.BlockSpec((CONV_W, tc), lambda bi, ci: (0, ci)),
                  pl.BlockSpec((1, tc), lambda bi, ci: (0, ci)),
                  pl.BlockSpec((2, tc // LANES, LANES, 2 * LANES), lambda bi, ci: (0, ci, 0, 0)),
                  pl.BlockSpec((2, 2, tc), lambda bi, ci: (0, 0, ci)),
                  pl.BlockSpec((2, tc), lambda bi, ci: (0, ci))],
        out_specs=pl.BlockSpec((1, p, tc), lambda bi, ci: (bi, 0, ci)),
        out_shape=jax.ShapeDtypeStruct((b, p, d), BF16),
        scratch_shapes=[pltpu.VMEM((p + 16, tc), F32),
                        pltpu.VMEM((p, tc), F32), pltpu.VMEM((p, tc), F32),
                        pltpu.VMEM((tt, tc), F32), pltpu.VMEM((tt, tc), F32),
                        pltpu.VMEM((tt, tc), F32), pltpu.VMEM((tt, tc), F32)],
        compiler_params=_params(("parallel", "parallel")),
        name="rglru",
    )(proj, proj, conv_w, conv_b.reshape(1, d), wg, gate_b, lam)


def _na_window(rb, rows):
    win = NA_QROWS + NA_ROWS - 1
    return min(max(NA_QROWS * rb - NA_ROWS // 2, 0), rows - win)


def _na_bias(rpb, rows):
    nb = rows // NA_QROWS
    win = NA_QROWS + NA_ROWS - 1
    kr = min(NA_ROWS, rows)
    out = []
    for rb in (0, 1, nb - 1):
        w0 = _na_window(rb, rows)
        r_abs = NA_QROWS * rb + jnp.arange(NA_QROWS)
        r0 = jnp.clip(r_abs - kr // 2, 0, rows - kr)
        k_abs = w0 + jnp.arange(win)
        valid_r = (k_abs[None, :] >= r0[:, None]) & (k_abs[None, :] < r0[:, None] + kr)
        rel_r = jnp.clip(k_abs[None, :] - r_abs[:, None] + NA_ROWS - 1, 0, 2 * NA_ROWS - 2)
        col = jnp.arange(GRID_W)
        c0 = jnp.clip(col - NA_COLS // 2, 0, GRID_W - NA_COLS)
        valid_c = (col[None, :] >= c0[:, None]) & (col[None, :] < c0[:, None] + NA_COLS)
        rel_c = jnp.clip(col[None, :] - col[:, None] + NA_COLS - 1, 0, 2 * NA_COLS - 2)
        bias = rpb[:, rel_r[:, None, :, None], rel_c[None, :, None, :]]
        valid = valid_r[:, None, :, None] & valid_c[None, :, None, :]
        bias = jnp.where(valid[None], bias, NEG_INF)
        out.append(bias.reshape(rpb.shape[0], NA_QROWS * GRID_W, win * GRID_W))
    return jnp.stack(out, axis=1)


def _na_kernel(q_ref, k_ref, v_ref, bias_ref, o_ref, km_ref, *, n_ctx, rows):
    hd = LANES // 2
    scale = hd ** -0.5
    nq = NA_QROWS * GRID_W
    nb = rows // NA_QROWS
    nk = (NA_QROWS + NA_ROWS - 1) * GRID_W
    lane = lax.broadcasted_iota(jnp.int32, (1, LANES), 1)
    first = lane < hd
    k_all = k_ref[0]
    km_ref[0] = jnp.where(first, k_all, jnp.zeros_like(k_all))
    km_ref[1] = jnp.where(first, jnp.zeros_like(k_all), k_all)

    def attend(qb, ks, bias_ty):
        outs = []
        for hh in range(2):
            s_ctx = _dot_nt(qb, km_ref[hh, 0:n_ctx, :]) * scale
            m = jnp.max(s_ctx, axis=-1, keepdims=True)
            if ks is not None:
                s_loc = _dot_nt(qb, km_ref[hh, ks:ks + nk, :]) * scale + bias_ref[hh, bias_ty]
                m = jnp.maximum(m, jnp.max(s_loc, axis=-1, keepdims=True))
                p_loc = jnp.exp(s_loc - m)
            p_ctx = jnp.exp(s_ctx - m)
            l = jnp.sum(p_ctx, axis=-1, keepdims=True)
            o = _dot(p_ctx.astype(BF16), v_ref[0, 0:n_ctx, :])
            if ks is not None:
                l = l + jnp.sum(p_loc, axis=-1, keepdims=True)
                o = o + _dot(p_loc.astype(BF16), v_ref[0, ks:ks + nk, :])
            outs.append(o * (1.0 / l))
        return jnp.where(first, outs[0], outs[1])

    o_ref[0, 0:n_ctx, :] = attend(q_ref[0, 0:n_ctx, :], None, None).astype(o_ref.dtype)
    for rb in range(nb):
        qs = n_ctx + rb * nq
        ks = n_ctx + _na_window(rb, rows) * GRID_W
        ty = 0 if rb == 0 else (2 if rb == nb - 1 else 1)
        o_ref[0, qs:qs + nq, :] = attend(q_ref[0, qs:qs + nq, :], ks, ty).astype(o_ref.dtype)


def _na(qkv, rpb, n_ctx):
    b, p, d3 = qkv.shape
    d = d3 // 3
    rows = (p - n_ctx) // GRID_W
    assert d // NA_HEADS == LANES // 2 and rows % NA_QROWS == 0 and rows // NA_QROWS >= 3
    nhp = d // LANES
    bias = _na_bias(rpb.astype(F32), rows)
    nq, nk = bias.shape[2], bias.shape[3]
    kern = functools.partial(_na_kernel, n_ctx=n_ctx, rows=rows)
    return pl.pallas_call(
        kern,
        grid=(nhp, b),
        in_specs=[pl.BlockSpec((1, p, LANES), lambda hp, bi: (bi, 0, hp)),
                  pl.BlockSpec((1, p, LANES), lambda hp, bi: (bi, 0, nhp + hp)),
                  pl.BlockSpec((1, p, LANES), lambda hp, bi: (bi, 0, 2 * nhp + hp)),
                  pl.BlockSpec((2, 3, nq, nk), lambda hp, bi: (hp, 0, 0, 0))],
        out_specs=pl.BlockSpec((1, p, LANES), lambda hp, bi: (bi, 0, hp)),
        out_shape=jax.ShapeDtypeStruct((b, p, d), BF16),
        scratch_shapes=[pltpu.VMEM((2, p, LANES), BF16)],
        compiler_params=_params(("parallel", "parallel")),
        name="nbr_attn",
    )(qkv, qkv, qkv, bias)


def _ret_tables(n_ctx, s, dk, ch):
    quarter = dk // 4
    pos = jnp.arange(s)
    inv = ROPE_BASE ** (-jnp.arange(quarter, dtype=F32) / quarter)
    ang_r = (pos // GRID_W).astype(F32)[:, None] * inv
    ang_c = (pos % GRID_W).astype(F32)[:, None] * inv
    cos = jnp.concatenate([jnp.cos(ang_r)] * 2 + [jnp.cos(ang_c)] * 2, axis=-1)
    sin = jnp.concatenate([-jnp.sin(ang_r), jnp.sin(ang_r), -jnp.sin(ang_c), jnp.sin(ang_c)], axis=-1)
    cos = jnp.concatenate([jnp.ones((n_ctx, dk), F32), cos], axis=0)
    sin = jnp.concatenate([jnp.zeros((n_ctx, dk), F32), sin], axis=0)
    log_gamma = jnp.log1p(-(2.0 ** (-5.0 - jnp.arange(RET_HEADS, dtype=F32))))[:, None, None]
    pq = jnp.arange(ch, dtype=F32)
    col = jnp.broadcast_to(pq[:, None], (ch, dk))[None]
    tabs = jnp.stack([
        jnp.exp(jnp.abs(pq[:, None] - pq[None, :])[None] * log_gamma),
        jnp.exp((col + 1.0) * log_gamma),
        jnp.exp((ch - col) * log_gamma),
        jnp.exp((ch - 1.0 - col) * log_gamma),
        jnp.exp(col * log_gamma),
    ], axis=1)
    chunk_decay = jnp.exp(ch * log_gamma[:, 0, 0])
    return cos, sin, tabs, chunk_decay


def _ret_kernel(cd_ref, q_ref, k_ref, v_ref, g_ref, cos_ref, sin_ref, tab_ref, o_ref,
                sb_ref, st_ref, *, ch, nc):
    dk = q_ref.shape[-1]
    cd = cd_ref[pl.program_id(1)]
    k_scale = dk ** -0.5

    def rope(t, c0):
        parts = []
        for hf in range(dk // LANES):
            sl = slice(hf * LANES, (hf + 1) * LANES)
            th = t[:, sl]
            parts.append(th * cos_ref[pl.ds(c0, ch), sl] + pltpu.roll(th, LANES // 2, 1) * sin_ref[pl.ds(c0, ch), sl])
        return jnp.concatenate(parts, axis=-1)

    def load_k(c0):
        return rope(k_ref[0, pl.ds(c0, ch), :].astype(F32) * k_scale, c0)

    def kv_outer(kdec, c0):
        return _dot_tn(kdec.astype(BF16), v_ref[0, pl.ds(c0, ch), :])

    nz = 1
    sb_ref[0] = jnp.zeros(sb_ref.shape[1:], BF16)
    st_ref[...] = kv_outer(load_k(0) * tab_ref[0, 4], 0)

    def bwd(j, _):
        c = nc - 1 - j
        c0 = pl.multiple_of(c * ch, ch)
        sb_ref[c] = st_ref[...].astype(BF16)
        st_ref[...] = cd * st_ref[...] + kv_outer(load_k(c0) * tab_ref[0, 4], c0)
        return 0

    lax.fori_loop(0, nc - 1 - nz, bwd, 0)
    sb_ref[nz] = st_ref[...].astype(BF16)

    st_ref[...] = jnp.zeros(st_ref.shape, F32)

    def fwd(c, _):
        c0 = pl.multiple_of(c * ch, ch)
        q = rope(q_ref[0, pl.ds(c0, ch), :].astype(F32), c0)
        k = load_k(c0)
        v = v_ref[0, pl.ds(c0, ch), :]
        inner = _dot_nt(q.astype(BF16), k.astype(BF16)) * tab_ref[0, 0]
        o = (_dot(inner.astype(BF16), v)
             + _dot((q * tab_ref[0, 1]).astype(BF16), st_ref[...].astype(BF16))
             + _dot((q * tab_ref[0, 2]).astype(BF16), sb_ref[c]))
        st_ref[...] = cd * st_ref[...] + kv_outer(k * tab_ref[0, 3], c0)
        o = o * lax.rsqrt(jnp.mean(o * o, axis=-1, keepdims=True) + RMS_EPS)
        g = g_ref[0, pl.ds(c0, ch), :].astype(F32)
        o_ref[0, pl.ds(c0, ch), :] = (o * (g * jax.nn.sigmoid(g))).astype(o_ref.dtype)
        return 0

    lax.fori_loop(0, nc, fwd, 0)


def _ret(qkvg, n_ctx):
    b, p, d6 = qkvg.shape
    d = d6 // 6
    dk = d // RET_HEADS
    dv = 2 * dk
    ch = dk
    assert dk == 2 * LANES and n_ctx == ch and p % ch == 0
    nc = p // ch
    cos, sin, tabs, chunk_decay = _ret_tables(n_ctx, p - n_ctx, dk, ch)
    kern = functools.partial(_ret_kernel, ch=ch, nc=nc)
    nh = RET_HEADS
    return pl.pallas_call(
        kern,
        grid_spec=pltpu.PrefetchScalarGridSpec(
            num_scalar_prefetch=1,
            grid=(b, nh),
            in_specs=[pl.BlockSpec((1, p, dk), lambda bi, h, cd: (bi, 0, h)),
                      pl.BlockSpec((1, p, dk), lambda bi, h, cd: (bi, 0, nh + h)),
                      pl.BlockSpec((1, p, dv), lambda bi, h, cd: (bi, 0, nh + h)),
                      pl.BlockSpec((1, p, dv), lambda bi, h, cd: (bi, 0, 2 * nh + h)),
                      pl.BlockSpec((p, dk), lambda bi, h, cd: (0, 0)),
                      pl.BlockSpec((p, dk), lambda bi, h, cd: (0, 0)),
                      pl.BlockSpec((1, 5, ch, dk), lambda bi, h, cd: (h, 0, 0, 0))],
            out_specs=pl.BlockSpec((1, p, dv), lambda bi, h, cd: (bi, 0, h)),
            scratch_shapes=[pltpu.VMEM((nc, dk, dv), BF16), pltpu.VMEM((dk, dv), F32)]),
        out_shape=jax.ShapeDtypeStruct((b, p, nh * dv), BF16),
        compiler_params=_params(("parallel", "parallel")),
        name="retention",
    )(chunk_decay, qkvg, qkvg, qkvg, qkvg, cos, sin, tabs)


def _moe_kernel(be_ref, nu_ref, xs_ref, wg_ref, wu_ref, wd_ref, gate_ref, ys_ref, *, fc):
    i = pl.program_id(0)

    @pl.when(i < nu_ref[0])
    def _():
        x = xs_ref[...]
        f = wg_ref.shape[-1]
        acc = jnp.zeros(ys_ref.shape, F32)
        for c in range(f // fc):
            sl = slice(c * fc, (c + 1) * fc)
            g = _dot(x, wg_ref[0, :, sl])
            u = _dot(x, wu_ref[0, :, sl])
            a = (g * jax.nn.sigmoid(g) * u).astype(BF16)
            acc = acc + _dot(a, wd_ref[0, sl, :])
        ys_ref[...] = acc * gate_ref[...]

    @pl.when(i >= nu_ref[0])
    def _():
        ys_ref[...] = jnp.zeros(ys_ref.shape, F32)


def _moe_experts(xs, gate, block_expert, n_used, w_gate, w_up, w_down):
    n_rows, d = xs.shape
    f = w_gate.shape[-1]
    bm = MOE_BLOCK
    kern = functools.partial(_moe_kernel, fc=_pick(f, (512, 256, 128)))
    return pl.pallas_call(
        kern,
        grid_spec=pltpu.PrefetchScalarGridSpec(
            num_scalar_prefetch=2,
            grid=(n_rows // bm,),
            in_specs=[pl.BlockSpec((bm, d), lambda i, be, nu: (i, 0)),
                      pl.BlockSpec((1, d, f), lambda i, be, nu: (be[i], 0, 0)),
                      pl.BlockSpec((1, d, f), lambda i, be, nu: (be[i], 0, 0)),
                      pl.BlockSpec((1, f, d), lambda i, be, nu: (be[i], 0, 0)),
                      pl.BlockSpec((bm, 1), lambda i, be, nu: (i, 0))],
            out_specs=pl.BlockSpec((bm, d), lambda i, be, nu: (i, 0))),
        out_shape=jax.ShapeDtypeStruct((n_rows, d), F32),
        compiler_params=_params(("arbitrary",)),
        name="moe_experts",
    )(block_expert, n_used, xs, w_gate, w_up, w_down, gate.reshape(n_rows, 1))


def _route(logits, router_b):
    n = logits.shape[0]
    probs = jax.nn.softmax(logits, axis=-1)
    sel = (probs + router_b.astype(F32)).reshape(n, N_GROUPS, EXPERTS_PER_GROUP)
    group = jnp.argmax(lax.top_k(sel, TOP_K)[0].sum(-1), axis=-1)
    in_group = jnp.take_along_axis(sel, group[:, None, None], axis=1)[:, 0]
    _, local = lax.top_k(in_group, TOP_K)
    expert = group[:, None] * EXPERTS_PER_GROUP + local
    weight = jnp.take_along_axis(probs, expert, axis=-1)
    return expert, weight / jnp.sum(weight, axis=-1, keepdims=True)


def _moe(h_rows, tok_ids, logits, router_b, w_gate, w_up, w_down):
    n = tok_ids.shape[0]
    d = h_rows.shape[-1]
    bm = MOE_BLOCK
    expert, weight = _route(logits, router_b)
    flat_e = expert.reshape(-1)
    n_assign = n * TOP_K
    onehot = (flat_e[:, None] == jnp.arange(N_EXPERTS)[None, :]).astype(jnp.int32)
    csum = jnp.cumsum(onehot, axis=0)
    rank = jnp.take_along_axis(csum, flat_e[:, None], axis=1)[:, 0] - 1
    counts = csum[-1]
    padded = (counts + bm - 1) // bm * bm
    pad_end = jnp.cumsum(padded)
    dest = (pad_end - padded)[flat_e] + rank
    n_blocks = -(-n_assign // bm) + N_EXPERTS
    n_rows = n_blocks * bm
    flat_t = jnp.repeat(tok_ids.astype(jnp.int32), TOP_K)
    src = jnp.zeros((n_rows,), jnp.int32).at[dest].set(flat_t)
    gate = jnp.zeros((n_rows,), F32).at[dest].set(weight.reshape(-1))
    block_expert = jnp.minimum(
        jnp.searchsorted(pad_end, jnp.arange(n_blocks) * bm, side='right'), N_EXPERTS - 1).astype(jnp.int32)
    n_used = (pad_end[-1] // bm).astype(jnp.int32).reshape(1)
    xs = h_rows[src]
    ys = _moe_experts(xs, gate, block_expert, n_used, w_gate, w_up, w_down)
    return ys[dest].reshape(n, TOP_K * d)


def _combine_kernel(x_ref, yy_ref, mod_ref, g_ref, o_ref, *, tm, n_ctx, off, final):
    d = x_ref.shape[-1]
    is_ctx = _is_ctx((pl.program_id(1) + off) * tm, tm, n_ctx)
    f = yy_ref[0, :, 0:d]
    for k in range(1, TOP_K):
        f = f + yy_ref[0, :, k * d:(k + 1) * d]
    xn = x_ref[0] + _mod_row(mod_ref, is_ctx, 5) * f
    if final:
        xn = xn * lax.rsqrt(jnp.mean(xn * xn, axis=-1, keepdims=True) + RMS_EPS) * g_ref[...]
    o_ref[0] = xn


def _combine(t, yy, mod, g, n_ctx, final):
    b, p, d = t.shape
    rows = yy.shape[1]
    tm = 256
    assert n_ctx % tm == 0 and rows % tm == 0
    off = (p - rows) // tm
    kern = functools.partial(_combine_kernel, tm=tm, n_ctx=n_ctx, off=off, final=final)
    return pl.pallas_call(
        kern,
        grid=(b, rows // tm),
        in_specs=[pl.BlockSpec((1, tm, d), lambda bi, i: (bi, i + off, 0)),
                  pl.BlockSpec((1, tm, TOP_K * d), lambda bi, i: (bi, i, 0)),
                  pl.BlockSpec((1, 2 * N_MOD, d), lambda bi, i: (bi, 0, 0)),
                  pl.BlockSpec((1, d), lambda bi, i: (0, 0))],
        out_specs=pl.BlockSpec((1, tm, d), lambda bi, i: (bi, i, 0)),
        out_shape=jax.ShapeDtypeStruct((b, rows, d), F32),
        compiler_params=_params(("parallel", "parallel")),
        name="ffn_residual",
    )(t, yy, mod, g.reshape(1, d))


def kernel(x, c, ctx, c_ctx, ada_w, ada_b, norm_mix_g, norm_ffn_g, final_norm_g, lru_w_in, lru_conv_w, lru_conv_b, lru_gate_w, lru_gate_b, lru_lambda, lru_w_out, na_w_qkv, na_rpb, na_w_o, ret_w_qkvg, ret_w_o, router_w, router_b, moe_w_gate, moe_w_up, moe_w_down):
    b, s, d = x.shape
    n_ctx = ctx.shape[1]
    p = n_ctx + s
    depth = ada_w.shape[0]
    t = jnp.concatenate([ctx, x], axis=1)

    pad_rows = -(b + 1) % 16
    cc = jnp.concatenate([c, c_ctx[None, :], jnp.zeros((pad_rows, d), F32)], axis=0)
    mods = _ada(cc, ada_w, ada_b)
    mod_x = mods[:, :b].reshape(depth, b, N_MOD, d)
    mod_z = jnp.broadcast_to(mods[:, b].reshape(depth, 1, N_MOD, d), (depth, b, N_MOD, d))
    mods = jnp.concatenate([mod_z, mod_x], axis=2)

    rw32 = router_w.astype(F32)
    rw_hi = rw32.astype(BF16)
    rw_lo = (rw32 - rw_hi.astype(F32)).astype(BF16)
    rw = jnp.concatenate([rw_hi, rw_lo, jnp.zeros((d, ROUTER_LANES - 2 * N_EXPERTS), BF16)], axis=1)

    all_rows = jnp.arange(b * p, dtype=jnp.int32)
    out = None
    for i in range(depth):
        kind, j = i % 3, i // 3
        last = i == depth - 1
        mod = mods[i]
        if kind == 0:
            proj = _norm_proj(t, norm_mix_g[i], mod, lru_w_in[j].astype(BF16), n_ctx, act_cols=d)
            a = _lru(proj, lru_conv_w[j], lru_conv_b[j], lru_gate_w[j], lru_gate_b[j], lru_lambda[j], n_ctx)
            w_o = lru_w_out[j]
        elif kind == 1:
            qkv = _norm_proj(t, norm_mix_g[i], mod, na_w_qkv[j].astype(BF16), n_ctx)
            a = _na(qkv, na_rpb[j], n_ctx)
            w_o = na_w_o[j]
        else:
            qkvg = _norm_proj(t, norm_mix_g[i], mod, ret_w_qkvg[j].astype(BF16), n_ctx)
            a = _ret(qkvg, n_ctx)
            w_o = ret_w_o[j]
        t, h2, pl_ = _out_proj(a, w_o.astype(BF16), t, mod, norm_ffn_g[i], rw, n_ctx)
        logits = (pl_[..., :N_EXPERTS] + pl_[..., N_EXPERTS:2 * N_EXPERTS])
        if last:
            tok_ids = all_rows.reshape(b, p)[:, n_ctx:].reshape(-1)
            logits = logits[:, n_ctx:]
        else:
            tok_ids = all_rows
        yy = _moe(h2.reshape(b * p, d), tok_ids, logits.reshape(-1, N_EXPERTS), router_b,
                  moe_w_gate[i].astype(BF16), moe_w_up[i].astype(BF16), moe_w_down[i].astype(BF16))
        yy = yy.reshape(b, -1, TOP_K * d)
        if last:
            out = _combine(t, yy, mod, final_norm_g, n_ctx, final=True)
        else:
            t = _combine(t, yy, mod, final_norm_g, n_ctx, final=False)
    return out
```

```python
import functools

import jax
import jax.numpy as jnp
from jax import lax
from jax.experimental import pallas as pl
from jax.experimental.pallas import tpu as pltpu

F32 = jnp.float32
BF16 = jnp.bfloat16

GRID_W = 64
N_MOD = 6
RMS_EPS = 1e-6
LRU_BLOCKS = 8
CONV_W = 4
CONV_LEFT = CONV_W // 2
LRU_C = 8.0
NA_HEADS = 16
NA_ROWS = 8
NA_COLS = 16
NA_QROWS = 4
NEG_INF = -1e30
RET_HEADS = 4
ROPE_BASE = 10000.0
N_EXPERTS = 16
N_GROUPS = 4
EXPERTS_PER_GROUP = N_EXPERTS // N_GROUPS
TOP_K = 2
MOE_BLOCK = 512
LANES = 128
ROUTER_LANES = 128
SCAN_TILE = 256
VMEM_LIMIT = 56 * 1024 * 1024


def _pick(n, candidates):
    for c in candidates:
        if n % c == 0:
            return c
    raise ValueError(f"no tile in {candidates} divides {n}")


def _params(sem):
    return pltpu.CompilerParams(dimension_semantics=sem, vmem_limit_bytes=VMEM_LIMIT)


def _dot(a, b):
    return jnp.dot(a, b, preferred_element_type=F32)


def _dot_nt(a, b):
    return lax.dot_general(a, b, (((1,), (1,)), ((), ())), preferred_element_type=F32)


def _dot_tn(a, b):
    return lax.dot_general(a, b, (((0,), (0,)), ((), ())), preferred_element_type=F32)


def _is_ctx(pos0, tm, n_ctx):
    return (pos0 + lax.broadcasted_iota(jnp.int32, (tm, 1), 0)) < n_ctx


def _mod_row(mod_ref, is_ctx, k):
    return jnp.where(is_ctx, mod_ref[0, k:k + 1, :], mod_ref[0, N_MOD + k:N_MOD + k + 1, :])


def _norm_mod(x, g, mod_ref, is_ctx, k):
    y = x * lax.rsqrt(jnp.mean(x * x, axis=-1, keepdims=True) + RMS_EPS) * g
    return y * (1.0 + _mod_row(mod_ref, is_ctx, k + 1)) + _mod_row(mod_ref, is_ctx, k)


def _ada_kernel(cc_ref, w_ref, b_ref, o_ref):
    cc = cc_ref[...]
    s = (cc * jax.nn.sigmoid(cc)).astype(BF16)
    o_ref[0] = _dot(s, w_ref[0].astype(BF16)) + b_ref[0]


def _ada(cc, ada_w, ada_b):
    depth, d, n = ada_w.shape
    rows = cc.shape[0]
    tn = _pick(n, (1536, 1024, 512, 256, 128))
    return pl.pallas_call(
        _ada_kernel,
        grid=(depth, n // tn),
        in_specs=[pl.BlockSpec((rows, d), lambda i, j: (0, 0)),
                  pl.BlockSpec((1, d, tn), lambda i, j: (i, 0, j)),
                  pl.BlockSpec((1, 1, tn), lambda i, j: (i, 0, j))],
        out_specs=pl.BlockSpec((1, rows, tn), lambda i, j: (i, 0, j)),
        out_shape=jax.ShapeDtypeStruct((depth, rows, n), F32),
        compiler_params=_params(("parallel", "parallel")),
        name="ada_mod",
    )(cc, ada_w, ada_b.reshape(depth, 1, n))


def _proj_kernel(x_ref, g_ref, mod_ref, w_ref, o_ref, h_ref, *, tm, n_ctx, act_tiles):
    i, j = pl.program_id(1), pl.program_id(2)

    @pl.when(j == 0)
    def _():
        is_ctx = _is_ctx(i * tm, tm, n_ctx)
        h_ref[...] = _norm_mod(x_ref[0], g_ref[...], mod_ref, is_ctx, 0).astype(BF16)

    y = _dot(h_ref[...], w_ref[...])
    if act_tiles:
        @pl.when(j < act_tiles)
        def _():
            o_ref[0] = jax.nn.gelu(y).astype(o_ref.dtype)

        @pl.when(j >= act_tiles)
        def _():
            o_ref[0] = y.astype(o_ref.dtype)
    else:
        o_ref[0] = y.astype(o_ref.dtype)


def _norm_proj(t, g, mod, w, n_ctx, act_cols=0):
    b, p, d = t.shape
    n = w.shape[1]
    tm = _pick(p, (768, 512, 256))
    tn = _pick(n, (1024, 512))
    assert act_cols % tn == 0
    kern = functools.partial(_proj_kernel, tm=tm, n_ctx=n_ctx, act_tiles=act_cols // tn)
    return pl.pallas_call(
        kern,
        grid=(b, p // tm, n // tn),
        in_specs=[pl.BlockSpec((1, tm, d), lambda bi, i, j: (bi, i, 0)),
                  pl.BlockSpec((1, d), lambda bi, i, j: (0, 0)),
                  pl.BlockSpec((1, 2 * N_MOD, d), lambda bi, i, j: (bi, 0, 0)),
                  pl.BlockSpec((d, tn), lambda bi, i, j: (0, j))],
        out_specs=pl.BlockSpec((1, tm, tn), lambda bi, i, j: (bi, i, j)),
        out_shape=jax.ShapeDtypeStruct((b, p, n), BF16),
        scratch_shapes=[pltpu.VMEM((tm, d), BF16)],
        compiler_params=_params(("parallel", "parallel", "arbitrary")),
        name="norm_proj",
    )(t, g.reshape(1, d), mod, w)


def _out_kernel(a_ref, w_ref, x_ref, mod_ref, g_ref, rw_ref, xo_ref, h_ref, p_ref, *, tm, n_ctx):
    is_ctx = _is_ctx(pl.program_id(1) * tm, tm, n_ctx)
    y = _dot(a_ref[0], w_ref[...])
    xn = x_ref[0] + _mod_row(mod_ref, is_ctx, 2) * y
    xo_ref[0] = xn
    h = _norm_mod(xn, g_ref[...], mod_ref, is_ctx, 3)
    hi = h.astype(BF16)
    h_ref[0] = hi
    lo = (h - hi.astype(F32)).astype(BF16)
    p_ref[0] = _dot(hi, rw_ref[...]) + _dot(lo, rw_ref[...])


def _out_proj(a, w, t, mod, g, rw, n_ctx):
    b, p, d = t.shape
    k = a.shape[-1]
    tm = _pick(p, (768, 512, 256))
    kern = functools.partial(_out_kernel, tm=tm, n_ctx=n_ctx)
    return pl.pallas_call(
        kern,
        grid=(b, p // tm),
        in_specs=[pl.BlockSpec((1, tm, k), lambda bi, i: (bi, i, 0)),
                  pl.BlockSpec((k, d), lambda bi, i: (0, 0)),
                  pl.BlockSpec((1, tm, d), lambda bi, i: (bi, i, 0)),
                  pl.BlockSpec((1, 2 * N_MOD, d), lambda bi, i: (bi, 0, 0)),
                  pl.BlockSpec((1, d), lambda bi, i: (0, 0)),
                  pl.BlockSpec((d, ROUTER_LANES), lambda bi, i: (0, 0))],
        out_specs=[pl.BlockSpec((1, tm, d), lambda bi, i: (bi, i, 0)),
                   pl.BlockSpec((1, tm, d), lambda bi, i: (bi, i, 0)),
                   pl.BlockSpec((1, tm, ROUTER_LANES), lambda bi, i: (bi, i, 0))],
        out_shape=[jax.ShapeDtypeStruct((b, p, d), F32),
                   jax.ShapeDtypeStruct((b, p, d), BF16),
                   jax.ShapeDtypeStruct((b, p, ROUTER_LANES), F32)],
        compiler_params=_params(("parallel", "parallel")),
        name="out_proj",
    )(a, w, t, mod, g.reshape(1, d), rw)


def _lru_kernel(gt_ref, up_ref, cw_ref, cb_ref, wg_ref, gb_ref, lam_ref, o_ref,
                upad, hf_s, hb_s, af, bf, ab, bb, *, n_ctx, tt, nt):
    p = nt * tt
    tc = o_ref.shape[-1]
    pad = 8
    upad[0:pad, :] = jnp.zeros((pad, tc), F32)
    upad[pad + p:pad + p + pad, :] = jnp.zeros((pad, tc), F32)
    for t in range(nt):
        upad[pad + t * tt:pad + (t + 1) * tt, :] = up_ref[0, t * tt:(t + 1) * tt, :].astype(F32)

    neg_lam = -lam_ref[...]
    sp = jnp.maximum(neg_lam, 0.0) + jnp.log1p(jnp.exp(-jnp.abs(neg_lam)))
    rowi = lax.broadcasted_iota(jnp.int32, (tt, 1), 0)

    def conv_tile(t):
        r0 = t * tt
        first = r0 in (0, n_ctx)
        last = r0 + tt in (n_ctx, p)
        acc = jnp.broadcast_to(cb_ref[...], (tt, tc))
        for kk in range(CONV_W):
            d = kk - CONV_LEFT
            xs = upad[pad + r0 + d:pad + r0 + d + tt, :]
            if first and d < 0:
                xs = jnp.where(rowi >= -d, xs, 0.0)
            if last and d > 0:
                xs = jnp.where(rowi < tt - d, xs, 0.0)
            acc = acc + cw_ref[kk:kk + 1, :] * xs
        return acc

    def gates(u, d, a_ref, b_ref):
        ub = u.astype(BF16)
        for n in range(tc // LANES):
            sl = slice(n * LANES, (n + 1) * LANES)
            ri = _dot(ub[:, sl], wg_ref[d, n])
            r = jax.nn.sigmoid(ri[:, :LANES] + gb_ref[d, 0:1, sl])
            ig = jax.nn.sigmoid(ri[:, LANES:] + gb_ref[d, 1:2, sl])
            log_a = (-LRU_C) * r * sp[d:d + 1, sl]
            a = jnp.exp(log_a)
            one_minus_a2 = -jnp.tanh(log_a) * (a * a + 1.0)
            a_ref[:, sl] = a
            b_ref[:, sl] = jnp.sqrt(one_minus_a2) * (ig * u[:, sl])

    nz = n_ctx // tt
    fwd_order = list(range(nt))
    bwd_order = list(range(nz - 1, -1, -1)) + list(range(nt - 1, nz - 1, -1))
    carry = (jnp.zeros((1, tc), F32), jnp.zeros((1, tc), F32))
    for tf, tb in zip(fwd_order, bwd_order):
        gates(conv_tile(tf), 0, af, bf)
        gates(conv_tile(tb), 1, ab, bb)

        def step(j, c, tf=tf, tb=tb):
            hf, hb = c
            hf = af[pl.ds(j, 1), :] * hf + bf[pl.ds(j, 1), :]
            hf_s[pl.ds(tf * tt + j, 1), :] = hf
            jb = tt - 1 - j
            hb = ab[pl.ds(jb, 1), :] * hb + bb[pl.ds(jb, 1), :]
            hb_s[pl.ds(tb * tt + jb, 1), :] = hb
            return hf, hb

        carry = lax.fori_loop(0, tt, step, carry, unroll=8)

    for t in range(nt):
        rs = slice(t * tt, (t + 1) * tt)
        o_ref[0, rs, :] = ((hf_s[rs, :] + hb_s[rs, :]) * gt_ref[0, rs, :].astype(F32)).astype(o_ref.dtype)


def _lru(proj, conv_w, conv_b, gate_w, gate_b, lam, n_ctx):
    b, p, d2 = proj.shape
    d = d2 // 2
    tc = 512
    tt = SCAN_TILE
    assert p % tt == 0 and n_ctx % tt == 0 and d % tc == 0 and d // LRU_BLOCKS == LANES
    nt = p // tt
    nct = d // tc
    wg = jnp.concatenate([gate_w[:, 0], gate_w[:, 1]], axis=-1).astype(BF16)
    kern = functools.partial(_lru_kernel, n_ctx=n_ctx, tt=tt, nt=nt)
    return pl.pallas_call(
        kern,
        grid=(b, nct),
        in_specs=[pl.BlockSpec((1, p, tc), lambda bi, ci: (bi, 0, ci)),
                  pl.BlockSpec((1, p, tc), lambda bi, ci: (bi, 0, nct + ci)),
                  pl.BlockSpec((CONV_W, tc), lambda bi, ci: (0, ci)),
                  pl.BlockSpec((1, tc), lambda bi, ci: (0, ci)),
                  pl.BlockSpec((2, tc // LANES, LANES, 2 * LANES), lambda bi, ci: (0, ci, 0, 0)),
                  pl.BlockSpec((2, 2, tc), lambda bi, ci: (0, 0, ci)),
                  pl.BlockSpec((2, tc), lambda bi, ci: (0, ci))],
        out_specs=pl.BlockSpec((1, p, tc), lambda bi, ci: (bi, 0, ci)),
        out_shape=jax.ShapeDtypeStruct((b, p, d), BF16),
        scratch_shapes=[pltpu.VMEM((p + 16, tc), F32),
                        pltpu.VMEM((p, tc), F32), pltpu.VMEM((p, tc), F32),
                        pltpu.VMEM((tt, tc), F32), pltpu.VMEM((tt, tc), F32),
                        pltpu.VMEM((tt, tc), F32), pltpu.VMEM((tt, tc), F32)],
        compiler_params=_params(("parallel", "parallel")),
        name="rglru",
    )(proj, proj, conv_w, conv_b.reshape(1, d), wg, gate_b, lam)


def _na_window(rb, rows):
    win = NA_QROWS + NA_ROWS - 1
    return min(max(NA_QROWS * rb - NA_ROWS // 2, 0), rows - win)


def _na_bias(rpb, rows):
    nb = rows // NA_QROWS
    win = NA_QROWS + NA_ROWS - 1
    kr = min(NA_ROWS, rows)
    out = []
    for rb in (0, 1, nb - 1):
        w0 = _na_window(rb, rows)
        r_abs = NA_QROWS * rb + jnp.arange(NA_QROWS)
        r0 = jnp.clip(r_abs - kr // 2, 0, rows - kr)
        k_abs = w0 + jnp.arange(win)
        valid_r = (k_abs[None, :] >= r0[:, None]) & (k_abs[None, :] < r0[:, None] + kr)
        rel_r = jnp.clip(k_abs[None, :] - r_abs[:, None] + NA_ROWS - 1, 0, 2 * NA_ROWS - 2)
        col = jnp.arange(GRID_W)
        c0 = jnp.clip(col - NA_COLS // 2, 0, GRID_W - NA_COLS)
        valid_c = (col[None, :] >= c0[:, None]) & (col[None, :] < c0[:, None] + NA_COLS)
        rel_c = jnp.clip(col[None, :] - col[:, None] + NA_COLS - 1, 0, 2 * NA_COLS - 2)
        bias = rpb[:, rel_r[:, None, :, None], rel_c[None, :, None, :]]
        valid = valid_r[:, None, :, None] & valid_c[None, :, None, :]
        bias = jnp.where(valid[None], bias, NEG_INF)
        out.append(bias.reshape(rpb.shape[0], NA_QROWS * GRID_W, win * GRID_W))
    return jnp.stack(out, axis=1)


def _na_kernel(q_ref, k_ref, v_ref, bias_ref, o_ref, km_ref, *, n_ctx, rows):
    hd = LANES // 2
    scale = hd ** -0.5
    nq = NA_QROWS * GRID_W
    nb = rows // NA_QROWS
    nk = (NA_QROWS + NA_ROWS - 1) * GRID_W
    lane = lax.broadcasted_iota(jnp.int32, (1, LANES), 1)
    first = lane < hd
    k_all = k_ref[0]
    km_ref[0] = jnp.where(first, k_all, jnp.zeros_like(k_all))
    km_ref[1] = jnp.where(first, jnp.zeros_like(k_all), k_all)

    def attend(qb, ks, bias_ty):
        outs = []
        for hh in range(2):
            s_ctx = _dot_nt(qb, km_ref[hh, 0:n_ctx, :]) * scale
            m = jnp.max(s_ctx, axis=-1, keepdims=True)
            if ks is not None:
                s_loc = _dot_nt(qb, km_ref[hh, ks:ks + nk, :]) * scale + bias_ref[hh, bias_ty]
                m = jnp.maximum(m, jnp.max(s_loc, axis=-1, keepdims=True))
                p_loc = jnp.exp(s_loc - m)
            p_ctx = jnp.exp(s_ctx - m)
            l = jnp.sum(p_ctx, axis=-1, keepdims=True)
            o = _dot(p_ctx.astype(BF16), v_ref[0, 0:n_ctx, :])
            if ks is not None:
                l = l + jnp.sum(p_loc, axis=-1, keepdims=True)
                o = o + _dot(p_loc.astype(BF16), v_ref[0, ks:ks + nk, :])
            outs.append(o * (1.0 / l))
        return jnp.where(first, outs[0], outs[1])

    o_ref[0, 0:n_ctx, :] = attend(q_ref[0, 0:n_ctx, :], None, None).astype(o_ref.dtype)
    for rb in range(nb):
        qs = n_ctx + rb * nq
        ks = n_ctx + _na_window(rb, rows) * GRID_W
        ty = 0 if rb == 0 else (2 if rb == nb - 1 else 1)
        o_ref[0, qs:qs + nq, :] = attend(q_ref[0, qs:qs + nq, :], ks, ty).astype(o_ref.dtype)


def _na(qkv, rpb, n_ctx):
    b, p, d3 = qkv.shape
    d = d3 // 3
    rows = (p - n_ctx) // GRID_W
    assert d // NA_HEADS == LANES // 2 and rows % NA_QROWS == 0 and rows // NA_QROWS >= 3
    nhp = d // LANES
    bias = _na_bias(rpb.astype(F32), rows)
    nq, nk = bias.shape[2], bias.shape[3]
    kern = functools.partial(_na_kernel, n_ctx=n_ctx, rows=rows)
    return pl.pallas_call(
        kern,
        grid=(nhp, b),
        in_specs=[pl.BlockSpec((1, p, LANES), lambda hp, bi: (bi, 0, hp)),
                  pl.BlockSpec((1, p, LANES), lambda hp, bi: (bi, 0, nhp + hp)),
                  pl.BlockSpec((1, p, LANES), lambda hp, bi: (bi, 0, 2 * nhp + hp)),
                  pl.BlockSpec((2, 3, nq, nk), lambda hp, bi: (hp, 0, 0, 0))],
        out_specs=pl.BlockSpec((1, p, LANES), lambda hp, bi: (bi, 0, hp)),
        out_shape=jax.ShapeDtypeStruct((b, p, d), BF16),
        scratch_shapes=[pltpu.VMEM((2, p, LANES), BF16)],
        compiler_params=_params(("parallel", "parallel")),
        name="nbr_attn",
    )(qkv, qkv, qkv, bias)


def _ret_tables(n_ctx, s, dk, ch):
    quarter = dk // 4
    pos = jnp.arange(s)
    inv = ROPE_BASE ** (-jnp.arange(quarter, dtype=F32) / quarter)
    ang_r = (pos // GRID_W).astype(F32)[:, None] * inv
    ang_c = (pos % GRID_W).astype(F32)[:, None] * inv
    cos = jnp.concatenate([jnp.cos(ang_r)] * 2 + [jnp.cos(ang_c)] * 2, axis=-1)
    sin = jnp.concatenate([-jnp.sin(ang_r), jnp.sin(ang_r), -jnp.sin(ang_c), jnp.sin(ang_c)], axis=-1)
    cos = jnp.concatenate([jnp.ones((n_ctx, dk), F32), cos], axis=0)
    sin = jnp.concatenate([jnp.zeros((n_ctx, dk), F32), sin], axis=0)
    log_gamma = jnp.log1p(-(2.0 ** (-5.0 - jnp.arange(RET_HEADS, dtype=F32))))[:, None, None]
    pq = jnp.arange(ch, dtype=F32)
    col = jnp.broadcast_to(pq[:, None], (ch, dk))[None]
    tabs = jnp.stack([
        jnp.exp(jnp.abs(pq[:, None] - pq[None, :])[None] * log_gamma),
        jnp.exp((col + 1.0) * log_gamma),
        jnp.exp((ch - col) * log_gamma),
        jnp.exp((ch - 1.0 - col) * log_gamma),
        jnp.exp(col * log_gamma),
    ], axis=1)
    chunk_decay = jnp.exp(ch * log_gamma[:, 0, 0])
    return cos, sin, tabs, chunk_decay


def _ret_kernel(cd_ref, q_ref, k_ref, v_ref, g_ref, cos_ref, sin_ref, tab_ref, o_ref,
                sb_ref, st_ref, *, ch, nc):
    dk = q_ref.shape[-1]
    cd = cd_ref[pl.program_id(1)]
    k_scale = dk ** -0.5

    def rope(t, c0):
        parts = []
        for hf in range(dk // LANES):
            sl = slice(hf * LANES, (hf + 1) * LANES)
            th = t[:, sl]
            parts.append(th * cos_ref[pl.ds(c0, ch), sl] + pltpu.roll(th, LANES // 2, 1) * sin_ref[pl.ds(c0, ch), sl])
        return jnp.concatenate(parts, axis=-1)

    def load_k(c0):
        return rope(k_ref[0, pl.ds(c0, ch), :].astype(F32) * k_scale, c0)

    def kv_outer(kdec, c0):
        return _dot_tn(kdec.astype(BF16), v_ref[0, pl.ds(c0, ch), :])

    nz = 1
    sb_ref[0] = jnp.zeros(sb_ref.shape[1:], BF16)
    st_ref[...] = kv_outer(load_k(0) * tab_ref[0, 4], 0)

    def bwd(j, _):
        c = nc - 1 - j
        c0 = pl.multiple_of(c * ch, ch)
        sb_ref[c] = st_ref[...].astype(BF16)
        st_ref[...] = cd * st_ref[...] + kv_outer(load_k(c0) * tab_ref[0, 4], c0)
        return 0

    lax.fori_loop(0, nc - 1 - nz, bwd, 0)
    sb_ref[nz] = st_ref[...].astype(BF16)

    st_ref[...] = jnp.zeros(st_ref.shape, F32)

    def fwd(c, _):
        c0 = pl.multiple_of(c * ch, ch)
        q = rope(q_ref[0, pl.ds(c0, ch), :].astype(F32), c0)
        k = load_k(c0)
        v = v_ref[0, pl.ds(c0, ch), :]
        inner = _dot_nt(q.astype(BF16), k.astype(BF16)) * tab_ref[0, 0]
        o = (_dot(inner.astype(BF16), v)
             + _dot((q * tab_ref[0, 1]).astype(BF16), st_ref[...].astype(BF16))
             + _dot((q * tab_ref[0, 2]).astype(BF16), sb_ref[c]))
        st_ref[...] = cd * st_ref[...] + kv_outer(k * tab_ref[0, 3], c0)
        o = o * lax.rsqrt(jnp.mean(o * o, axis=-1, keepdims=True) + RMS_EPS)
        g = g_ref[0, pl.ds(c0, ch), :].astype(F32)
        o_ref[0, pl.ds(c0, ch), :] = (o * (g * jax.nn.sigmoid(g))).astype(o_ref.dtype)
        return 0

    lax.fori_loop(0, nc, fwd, 0)


def _ret(qkvg, n_ctx):
    b, p, d6 = qkvg.shape
    d = d6 // 6
    dk = d // RET_HEADS
    dv = 2 * dk
    ch = dk
    assert dk == 2 * LANES and n_ctx == ch and p % ch == 0
    nc = p // ch
    cos, sin, tabs, chunk_decay = _ret_tables(n_ctx, p - n_ctx, dk, ch)
    kern = functools.partial(_ret_kernel, ch=ch, nc=nc)
    nh = RET_HEADS
    return pl.pallas_call(
        kern,
        grid_spec=pltpu.PrefetchScalarGridSpec(
            num_scalar_prefetch=1,
            grid=(b, nh),
            in_specs=[pl.BlockSpec((1, p, dk), lambda bi, h, cd: (bi, 0, h)),
                      pl.BlockSpec((1, p, dk), lambda bi, h, cd: (bi, 0, nh + h)),
                      pl.BlockSpec((1, p, dv), lambda bi, h, cd: (bi, 0, nh + h)),
                      pl.BlockSpec((1, p, dv), lambda bi, h, cd: (bi, 0, 2 * nh + h)),
                      pl.BlockSpec((p, dk), lambda bi, h, cd: (0, 0)),
                      pl.BlockSpec((p, dk), lambda bi, h, cd: (0, 0)),
                      pl.BlockSpec((1, 5, ch, dk), lambda bi, h, cd: (h, 0, 0, 0))],
            out_specs=pl.BlockSpec((1, p, dv), lambda bi, h, cd: (bi, 0, h)),
            scratch_shapes=[pltpu.VMEM((nc, dk, dv), BF16), pltpu.VMEM((dk, dv), F32)]),
        out_shape=jax.ShapeDtypeStruct((b, p, nh * dv), BF16),
        compiler_params=_params(("parallel", "parallel")),
        name="retention",
    )(chunk_decay, qkvg, qkvg, qkvg, qkvg, cos, sin, tabs)


def _router_kernel(lg_ref, rb_ref, tri_ref, e_ref, w_ref, rk_ref, cnt_ref, run_ref):
    @pl.when(pl.program_id(0) == 0)
    def _():
        run_ref[...] = jnp.zeros(run_ref.shape, F32)

    lg = lg_ref[...]
    ex = jnp.exp(lg - jnp.max(lg, axis=0, keepdims=True))
    probs = ex / jnp.sum(ex, axis=0, keepdims=True)
    sel = probs + rb_ref[...]
    epg = EXPERTS_PER_GROUP
    best = grp = cur = curp = None
    for g in range(N_GROUPS):
        s = [sel[g * epg + i:g * epg + i + 1, :] for i in range(epg)]
        pr = [probs[g * epg + i:g * epg + i + 1, :] for i in range(epg)]
        top2 = None
        for i in range(epg):
            for j in range(i + 1, epg):
                top2 = s[i] + s[j] if top2 is None else jnp.maximum(top2, s[i] + s[j])
        if g == 0:
            best, grp, cur, curp = top2, jnp.zeros(top2.shape, jnp.int32), s, pr
        else:
            better = top2 > best
            best = jnp.where(better, top2, best)
            grp = jnp.where(better, g, grp)
            cur = [jnp.where(better, s[i], cur[i]) for i in range(epg)]
            curp = [jnp.where(better, pr[i], curp[i]) for i in range(epg)]
    b1, i1, p1 = cur[0], jnp.zeros(best.shape, jnp.int32), curp[0]
    for i in range(1, epg):
        gt = cur[i] > b1
        b1, i1, p1 = jnp.where(gt, cur[i], b1), jnp.where(gt, i, i1), jnp.where(gt, curp[i], p1)
    b2 = i2 = p2 = None
    for i in range(epg):
        v = jnp.where(i1 == i, -jnp.inf, cur[i])
        if b2 is None:
            b2, i2, p2 = v, jnp.zeros(best.shape, jnp.int32), curp[0]
        else:
            gt = v > b2
            b2, i2, p2 = jnp.where(gt, v, b2), jnp.where(gt, i, i2), jnp.where(gt, curp[i], p2)
    e1 = grp * epg + i1
    e2 = grp * epg + i2
    inv = 1.0 / (p1 + p2)
    e_ref[0:1, :] = e1
    e_ref[1:2, :] = e2
    w_ref[0:1, :] = p1 * inv
    w_ref[1:2, :] = p2 * inv

    eidx = lax.broadcasted_iota(jnp.int32, lg.shape, 0)
    eq1 = eidx == e1
    eq2 = eidx == e2
    member = jnp.where(eq1, 1.0, jnp.where(eq2, 1.0, 0.0))
    before = _dot(member.astype(BF16), tri_ref[...]) + run_ref[...]
    rk_ref[0:1, :] = jnp.sum(jnp.where(eq1, before, 0.0), axis=0, keepdims=True).astype(jnp.int32)
    rk_ref[1:2, :] = jnp.sum(jnp.where(eq2, before, 0.0), axis=0, keepdims=True).astype(jnp.int32)
    run_ref[...] = run_ref[...] + jnp.sum(member, axis=1, keepdims=True)
    cnt_ref[...] = run_ref[...]


def _router(logits_t, router_b):
    e, n = logits_t.shape
    tt = 512
    assert n % tt == 0
    tri = (jnp.arange(tt)[:, None] < jnp.arange(tt)[None, :]).astype(BF16)
    kn = jax.ShapeDtypeStruct((TOP_K, n), jnp.int32)
    return pl.pallas_call(
        _router_kernel,
        grid=(n // tt,),
        in_specs=[pl.BlockSpec((e, tt), lambda i: (0, i)),
                  pl.BlockSpec((e, 1), lambda i: (0, 0)),
                  pl.BlockSpec((tt, tt), lambda i: (0, 0))],
        out_specs=[pl.BlockSpec((TOP_K, tt), lambda i: (0, i)),
                   pl.BlockSpec((TOP_K, tt), lambda i: (0, i)),
                   pl.BlockSpec((TOP_K, tt), lambda i: (0, i)),
                   pl.BlockSpec((e, 1), lambda i: (0, 0))],
        out_shape=[kn, jax.ShapeDtypeStruct((TOP_K, n), F32), kn, jax.ShapeDtypeStruct((e, 1), F32)],
        scratch_shapes=[pltpu.VMEM((e, 1), F32)],
        compiler_params=_params(("arbitrary",)),
        name="router",
    )(logits_t, router_b.astype(F32).reshape(e, 1), tri)


def _moe_kernel(be_ref, nu_ref, xs_ref, wg_ref, wu_ref, wd_ref, ys_ref, *, fc):
    i = pl.program_id(0)

    @pl.when(i < nu_ref[0])
    def _():
        x = xs_ref[...]
        f = wg_ref.shape[-1]
        acc = jnp.zeros(ys_ref.shape, F32)
        for c in range(f // fc):
            sl = slice(c * fc, (c + 1) * fc)
            g = _dot(x, wg_ref[0, :, sl])
            u = _dot(x, wu_ref[0, :, sl])
            a = (g * jax.nn.sigmoid(g) * u).astype(BF16)
            acc = acc + _dot(a, wd_ref[0, sl, :])
        ys_ref[...] = acc

    @pl.when(i >= nu_ref[0])
    def _():
        ys_ref[...] = jnp.zeros(ys_ref.shape, F32)


def _moe_experts(xs, block_expert, n_used, w_gate, w_up, w_down):
    n_rows, d = xs.shape
    f = w_gate.shape[-1]
    bm = MOE_BLOCK
    kern = functools.partial(_moe_kernel, fc=_pick(f, (512, 256, 128)))
    return pl.pallas_call(
        kern,
        grid_spec=pltpu.PrefetchScalarGridSpec(
            num_scalar_prefetch=2,
            grid=(n_rows // bm,),
            in_specs=[pl.BlockSpec((bm, d), lambda i, be, nu: (i, 0)),
                      pl.BlockSpec((1, d, f), lambda i, be, nu: (be[i], 0, 0)),
                      pl.BlockSpec((1, d, f), lambda i, be, nu: (be[i], 0, 0)),
                      pl.BlockSpec((1, f, d), lambda i, be, nu: (be[i], 0, 0))],
            out_specs=pl.BlockSpec((bm, d), lambda i, be, nu: (i, 0))),
        out_shape=jax.ShapeDtypeStruct((n_rows, d), F32),
        compiler_params=_params(("arbitrary",)),
        name="moe_experts",
    )(block_expert, n_used, xs, w_gate, w_up, w_down)


def _moe(h_rows, tok_ids, logits_t, router_b, w_gate, w_up, w_down):
    n = tok_ids.shape[0]
    bm = MOE_BLOCK
    expert, weight, rank, counts = _router(logits_t, router_b)
    counts = counts[:, 0].astype(jnp.int32)
    padded = (counts + bm - 1) // bm * bm
    pad_end = jnp.cumsum(padded)
    pad_start = pad_end - padded
    dest = rank + jnp.sum(jnp.where(expert[..., None] == jnp.arange(N_EXPERTS), pad_start, 0), axis=-1)
    n_blocks = -(-(n * TOP_K) // bm) + N_EXPERTS
    n_rows = n_blocks * bm
    tok = tok_ids.astype(jnp.int32)
    src = jnp.zeros((n_rows,), jnp.int32).at[dest.reshape(-1)].set(jnp.tile(tok, TOP_K), unique_indices=True)
    block_expert = jnp.minimum(
        jnp.searchsorted(pad_end, jnp.arange(n_blocks) * bm, side='right'), N_EXPERTS - 1).astype(jnp.int32)
    n_used = (pad_end[-1] // bm).astype(jnp.int32).reshape(1)
    ys = _moe_experts(h_rows[src], block_expert, n_used, w_gate, w_up, w_down)
    return [ys[dest[k]] for k in range(TOP_K)], weight


def _combine_kernel(x_ref, y0_ref, y1_ref, w_ref, mod_ref, g_ref, o_ref, *, tm, n_ctx, off, final):
    is_ctx = _is_ctx((pl.program_id(1) + off) * tm, tm, n_ctx)
    w = w_ref[0]
    f = w[:, 0:1] * y0_ref[0] + w[:, 1:2] * y1_ref[0]
    xn = x_ref[0] + _mod_row(mod_ref, is_ctx, 5) * f
    if final:
        xn = xn * lax.rsqrt(jnp.mean(xn * xn, axis=-1, keepdims=True) + RMS_EPS) * g_ref[...]
    o_ref[0] = xn


def _combine(t, ys, w, mod, g, n_ctx, final):
    b, p, d = t.shape
    rows = ys[0].shape[1]
    tm = 256
    assert n_ctx % tm == 0 and rows % tm == 0 and len(ys) == 2
    off = (p - rows) // tm
    kern = functools.partial(_combine_kernel, tm=tm, n_ctx=n_ctx, off=off, final=final)
    row_spec = pl.BlockSpec((1, tm, d), lambda bi, i: (bi, i, 0))
    return pl.pallas_call(
        kern,
        grid=(b, rows // tm),
        in_specs=[pl.BlockSpec((1, tm, d), lambda bi, i: (bi, i + off, 0)),
                  row_spec, row_spec,
                  pl.BlockSpec((1, tm, TOP_K), lambda bi, i: (bi, i, 0)),
                  pl.BlockSpec((1, 2 * N_MOD, d), lambda bi, i: (bi, 0, 0)),
                  pl.BlockSpec((1, d), lambda bi, i: (0, 0))],
        out_specs=row_spec,
        out_shape=jax.ShapeDtypeStruct((b, rows, d), F32),
        compiler_params=_params(("parallel", "parallel")),
        name="ffn_residual",
    )(t, ys[0], ys[1], w, mod, g.reshape(1, d))


def kernel(x, c, ctx, c_ctx, ada_w, ada_b, norm_mix_g, norm_ffn_g, final_norm_g, lru_w_in, lru_conv_w, lru_conv_b, lru_gate_w, lru_gate_b, lru_lambda, lru_w_out, na_w_qkv, na_rpb, na_w_o, ret_w_qkvg, ret_w_o, router_w, router_b, moe_w_gate, moe_w_up, moe_w_down):
    b, s, d = x.shape
    n_ctx = ctx.shape[1]
    p = n_ctx + s
    depth = ada_w.shape[0]
    t = jnp.concatenate([ctx, x], axis=1)

    pad_rows = -(b + 1) % 16
    cc = jnp.concatenate([c, c_ctx[None, :], jnp.zeros((pad_rows, d), F32)], axis=0)
    mods = _ada(cc, ada_w, ada_b)
    mod_x = mods[:, :b].reshape(depth, b, N_MOD, d)
    mod_z = jnp.broadcast_to(mods[:, b].reshape(depth, 1, N_MOD, d), (depth, b, N_MOD, d))
    mods = jnp.concatenate([mod_z, mod_x], axis=2)

    rw32 = router_w.astype(F32)
    rw_hi = rw32.astype(BF16)
    rw_lo = (rw32 - rw_hi.astype(F32)).astype(BF16)
    rw = jnp.concatenate([rw_hi, rw_lo, jnp.zeros((d, ROUTER_LANES - 2 * N_EXPERTS), BF16)], axis=1)

    all_rows = jnp.arange(b * p, dtype=jnp.int32)
    out = None
    for i in range(depth):
        kind, j = i % 3, i // 3
        last = i == depth - 1
        mod = mods[i]
        if kind == 0:
            proj = _norm_proj(t, norm_mix_g[i], mod, lru_w_in[j].astype(BF16), n_ctx, act_cols=d)
            a = _lru(proj, lru_conv_w[j], lru_conv_b[j], lru_gate_w[j], lru_gate_b[j], lru_lambda[j], n_ctx)
            w_o = lru_w_out[j]
        elif kind == 1:
            qkv = _norm_proj(t, norm_mix_g[i], mod, na_w_qkv[j].astype(BF16), n_ctx)
            a = _na(qkv, na_rpb[j], n_ctx)
            w_o = na_w_o[j]
        else:
            qkvg = _norm_proj(t, norm_mix_g[i], mod, ret_w_qkvg[j].astype(BF16), n_ctx)
            a = _ret(qkvg, n_ctx)
            w_o = ret_w_o[j]
        t, h2, pl_ = _out_proj(a, w_o.astype(BF16), t, mod, norm_ffn_g[i], rw, n_ctx)
        logits = (pl_[..., :N_EXPERTS] + pl_[..., N_EXPERTS:2 * N_EXPERTS])
        if last:
            tok_ids = all_rows.reshape(b, p)[:, n_ctx:].reshape(-1)
            logits = logits[:, n_ctx:]
        else:
            tok_ids = all_rows
        ys, wts = _moe(h2.reshape(b * p, d), tok_ids, logits.reshape(-1, N_EXPERTS).T, router_b,
                       moe_w_gate[i].astype(BF16), moe_w_up[i].astype(BF16), moe_w_down[i].astype(BF16))
        ys = [y.reshape(b, -1, d) for y in ys]
        wts = wts.T.reshape(b, -1, TOP_K)
        if last:
            out = _combine(t, ys, wts, mod, final_norm_g, n_ctx, final=True)
        else:
            t = _combine(t, ys, wts, mod, final_norm_g, n_ctx, final=False)
    return out
```

```python
import functools

import jax
import jax.numpy as jnp
from jax import lax
from jax.experimental import pallas as pl
from jax.experimental.pallas import tpu as pltpu

F32 = jnp.float32
BF16 = jnp.bfloat16

GRID_W = 64
N_MOD = 6
RMS_EPS = 1e-6
LRU_BLOCKS = 8
CONV_W = 4
CONV_LEFT = CONV_W // 2
LRU_C = 8.0
NA_HEADS = 16
NA_ROWS = 8
NA_COLS = 16
NA_QROWS = 4
NEG_INF = -1e30
RET_HEADS = 4
ROPE_BASE = 10000.0
N_EXPERTS = 16
N_GROUPS = 4
EXPERTS_PER_GROUP = N_EXPERTS // N_GROUPS
TOP_K = 2
MOE_BLOCK = 512
LANES = 128
ROUTER_LANES = 128
SCAN_TILE = 256
VMEM_LIMIT = 56 * 1024 * 1024


def _pick(n, candidates):
    for c in candidates:
        if n % c == 0:
            return c
    raise ValueError(f"no tile in {candidates} divides {n}")


def _params(sem):
    return pltpu.CompilerParams(dimension_semantics=sem, vmem_limit_bytes=VMEM_LIMIT)


def _at(w, i):
    return lax.index_in_dim(w, i, axis=0, keepdims=False)


def _dot(a, b):
    return jnp.dot(a, b, preferred_element_type=F32)


def _dot_nt(a, b):
    return lax.dot_general(a, b, (((1,), (1,)), ((), ())), preferred_element_type=F32)


def _dot_tn(a, b):
    return lax.dot_general(a, b, (((0,), (0,)), ((), ())), preferred_element_type=F32)


def _is_ctx(pos0, tm, n_ctx):
    return (pos0 + lax.broadcasted_iota(jnp.int32, (tm, 1), 0)) < n_ctx


def _mod_row(mod_ref, is_ctx, k):
    return jnp.where(is_ctx, mod_ref[0, k:k + 1, :], mod_ref[0, N_MOD + k:N_MOD + k + 1, :])


def _norm_mod(x, g, mod_ref, is_ctx, k):
    y = x * lax.rsqrt(jnp.mean(x * x, axis=-1, keepdims=True) + RMS_EPS) * g
    return y * (1.0 + _mod_row(mod_ref, is_ctx, k + 1)) + _mod_row(mod_ref, is_ctx, k)


def _ada_kernel(cc_ref, w_ref, b_ref, o_ref):
    cc = cc_ref[...]
    s = (cc * jax.nn.sigmoid(cc)).astype(BF16)
    o_ref[0] = _dot(s, w_ref[0].astype(BF16)) + b_ref[0]


def _ada(cc, ada_w, ada_b):
    depth, d, n = ada_w.shape
    rows = cc.shape[0]
    tn = _pick(n, (1536, 1024, 512, 256, 128))
    return pl.pallas_call(
        _ada_kernel,
        grid=(depth, n // tn),
        in_specs=[pl.BlockSpec((rows, d), lambda i, j: (0, 0)),
                  pl.BlockSpec((1, d, tn), lambda i, j: (i, 0, j)),
                  pl.BlockSpec((1, 1, tn), lambda i, j: (i, 0, j))],
        out_specs=pl.BlockSpec((1, rows, tn), lambda i, j: (i, 0, j)),
        out_shape=jax.ShapeDtypeStruct((depth, rows, n), F32),
        compiler_params=_params(("parallel", "parallel")),
        name="ada_mod",
    )(cc, ada_w, ada_b.reshape(depth, 1, n))


def _proj_kernel(x_ref, g_ref, mod_ref, w_ref, o_ref, h_ref, *, tm, n_ctx, act_tiles):
    i, j = pl.program_id(1), pl.program_id(2)

    @pl.when(j == 0)
    def _():
        is_ctx = _is_ctx(i * tm, tm, n_ctx)
        h_ref[...] = _norm_mod(x_ref[0], g_ref[...], mod_ref, is_ctx, 0).astype(BF16)

    y = _dot(h_ref[...], w_ref[...])
    if act_tiles:
        @pl.when(j < act_tiles)
        def _():
            o_ref[0] = jax.nn.gelu(y).astype(o_ref.dtype)

        @pl.when(j >= act_tiles)
        def _():
            o_ref[0] = y.astype(o_ref.dtype)
    else:
        o_ref[0] = y.astype(o_ref.dtype)


def _norm_proj(t, g, mod, w, n_ctx, act_cols=0):
    b, p, d = t.shape
    n = w.shape[1]
    tm = _pick(p, (768, 512, 256))
    tn = _pick(n, (1024, 512))
    assert act_cols % tn == 0
    kern = functools.partial(_proj_kernel, tm=tm, n_ctx=n_ctx, act_tiles=act_cols // tn)
    return pl.pallas_call(
        kern,
        grid=(b, p // tm, n // tn),
        in_specs=[pl.BlockSpec((1, tm, d), lambda bi, i, j: (bi, i, 0)),
                  pl.BlockSpec((1, d), lambda bi, i, j: (0, 0)),
                  pl.BlockSpec((1, 2 * N_MOD, d), lambda bi, i, j: (bi, 0, 0)),
                  pl.BlockSpec((d, tn), lambda bi, i, j: (0, j))],
        out_specs=pl.BlockSpec((1, tm, tn), lambda bi, i, j: (bi, i, j)),
        out_shape=jax.ShapeDtypeStruct((b, p, n), BF16),
        scratch_shapes=[pltpu.VMEM((tm, d), BF16)],
        compiler_params=_params(("parallel", "parallel", "arbitrary")),
        name="norm_proj",
    )(t, g.reshape(1, d), mod, w)


def _out_kernel(a_ref, w_ref, x_ref, mod_ref, g_ref, rw_ref, xo_ref, h_ref, p_ref, *, tm, n_ctx):
    is_ctx = _is_ctx(pl.program_id(1) * tm, tm, n_ctx)
    y = _dot(a_ref[0], w_ref[...])
    xn = x_ref[0] + _mod_row(mod_ref, is_ctx, 2) * y
    xo_ref[0] = xn
    h = _norm_mod(xn, g_ref[...], mod_ref, is_ctx, 3)
    hi = h.astype(BF16)
    h_ref[0] = hi
    lo = (h - hi.astype(F32)).astype(BF16)
    p_ref[0] = _dot(hi, rw_ref[...]) + _dot(lo, rw_ref[...])


def _out_proj(a, w, t, mod, g, rw, n_ctx):
    b, p, d = t.shape
    k = a.shape[-1]
    tm = _pick(p, (768, 512, 256))
    kern = functools.partial(_out_kernel, tm=tm, n_ctx=n_ctx)
    return pl.pallas_call(
        kern,
        grid=(b, p // tm),
        in_specs=[pl.BlockSpec((1, tm, k), lambda bi, i: (bi, i, 0)),
                  pl.BlockSpec((k, d), lambda bi, i: (0, 0)),
                  pl.BlockSpec((1, tm, d), lambda bi, i: (bi, i, 0)),
                  pl.BlockSpec((1, 2 * N_MOD, d), lambda bi, i: (bi, 0, 0)),
                  pl.BlockSpec((1, d), lambda bi, i: (0, 0)),
                  pl.BlockSpec((d, ROUTER_LANES), lambda bi, i: (0, 0))],
        out_specs=[pl.BlockSpec((1, tm, d), lambda bi, i: (bi, i, 0)),
                   pl.BlockSpec((1, tm, d), lambda bi, i: (bi, i, 0)),
                   pl.BlockSpec((1, tm, ROUTER_LANES), lambda bi, i: (bi, i, 0))],
        out_shape=[jax.ShapeDtypeStruct((b, p, d), F32),
                   jax.ShapeDtypeStruct((b, p, d), BF16),
                   jax.ShapeDtypeStruct((b, p, ROUTER_LANES), F32)],
        compiler_params=_params(("parallel", "parallel")),
        name="out_proj",
    )(a, w, t, mod, g.reshape(1, d), rw)


def _lru_kernel(gt_ref, up_ref, cw_ref, cb_ref, wg_ref, gb_ref, lam_ref, o_ref,
                upad, hf_s, hb_s, af, bf, ab, bb, *, n_ctx, tt, nt):
    p = nt * tt
    tc = o_ref.shape[-1]
    pad = 8
    upad[0:pad, :] = jnp.zeros((pad, tc), F32)
    upad[pad + p:pad + p + pad, :] = jnp.zeros((pad, tc), F32)
    for t in range(nt):
        upad[pad + t * tt:pad + (t + 1) * tt, :] = up_ref[0, t * tt:(t + 1) * tt, :].astype(F32)

    neg_lam = -lam_ref[...]
    sp = jnp.maximum(neg_lam, 0.0) + jnp.log1p(jnp.exp(-jnp.abs(neg_lam)))
    rowi = lax.broadcasted_iota(jnp.int32, (tt, 1), 0)

    def conv_tile(t):
        r0 = t * tt
        first = r0 in (0, n_ctx)
        last = r0 + tt in (n_ctx, p)
        acc = jnp.broadcast_to(cb_ref[...], (tt, tc))
        for kk in range(CONV_W):
            d = kk - CONV_LEFT
            xs = upad[pad + r0 + d:pad + r0 + d + tt, :]
            if first and d < 0:
                xs = jnp.where(rowi >= -d, xs, 0.0)
            if last and d > 0:
                xs = jnp.where(rowi < tt - d, xs, 0.0)
            acc = acc + cw_ref[kk:kk + 1, :] * xs
        return acc

    def gates(u, d, a_ref, b_ref):
        ub = u.astype(BF16)
        for n in range(tc // LANES):
            sl = slice(n * LANES, (n + 1) * LANES)
            ri = _dot(ub[:, sl], wg_ref[d, n])
            r = jax.nn.sigmoid(ri[:, :LANES] + gb_ref[d, 0:1, sl])
            ig = jax.nn.sigmoid(ri[:, LANES:] + gb_ref[d, 1:2, sl])
            log_a = (-LRU_C) * r * sp[d:d + 1, sl]
            a = jnp.exp(log_a)
            one_minus_a2 = -jnp.tanh(log_a) * (a * a + 1.0)
            a_ref[:, sl] = a
            b_ref[:, sl] = jnp.sqrt(one_minus_a2) * (ig * u[:, sl])

    nz = n_ctx // tt
    fwd_order = list(range(nt))
    bwd_order = list(range(nz - 1, -1, -1)) + list(range(nt - 1, nz - 1, -1))
    carry = (jnp.zeros((1, tc), F32), jnp.zeros((1, tc), F32))
    for tf, tb in zip(fwd_order, bwd_order):
        gates(conv_tile(tf), 0, af, bf)
        gates(conv_tile(tb), 1, ab, bb)

        def step(j, c, tf=tf, tb=tb):
            hf, hb = c
            hf = af[pl.ds(j, 1), :] * hf + bf[pl.ds(j, 1), :]
            hf_s[pl.ds(tf * tt + j, 1), :] = hf
            jb = tt - 1 - j
            hb = ab[pl.ds(jb, 1), :] * hb + bb[pl.ds(jb, 1), :]
            hb_s[pl.ds(tb * tt + jb, 1), :] = hb
            return hf, hb

        carry = lax.fori_loop(0, tt, step, carry, unroll=8)

    for t in range(nt):
        rs = slice(t * tt, (t + 1) * tt)
        o_ref[0, rs, :] = ((hf_s[rs, :] + hb_s[rs, :]) * gt_ref[0, rs, :].astype(F32)).astype(o_ref.dtype)


def _lru(proj, conv_w, conv_b, gate_w, gate_b, lam, n_ctx):
    b, p, d2 = proj.shape
    d = d2 // 2
    tc = 512
    tt = SCAN_TILE
    assert p % tt == 0 and n_ctx % tt == 0 and d % tc == 0 and d // LRU_BLOCKS == LANES
    nt = p // tt
    nct = d // tc
    wg = jnp.concatenate([gate_w[:, 0], gate_w[:, 1]], axis=-1).astype(BF16)
    kern = functools.partial(_lru_kernel, n_ctx=n_ctx, tt=tt, nt=nt)
    return pl.pallas_call(
        kern,
        grid=(b, nct),
        in_specs=[pl.BlockSpec((1, p, tc), lambda bi, ci: (bi, 0, ci)),
                  pl.BlockSpec((1, p, tc), lambda bi, ci: (bi, 0, nct + ci)),
                  pl.BlockSpec((CONV_W, tc), lambda bi, ci: (0, ci)),
                  pl.BlockSpec((1, tc), lambda bi, ci: (0, ci)),
                  pl.BlockSpec((2, tc // LANES, LANES, 2 * LANES), lambda bi, ci: (0, ci, 0, 0)),
                  pl.BlockSpec((2, 2, tc), lambda bi, ci: (0, 0, ci)),
                  pl.BlockSpec((2, tc), lambda bi, ci: (0, ci))],
        out_specs=pl.BlockSpec((1, p, tc), lambda bi, ci: (bi, 0, ci)),
        out_shape=jax.ShapeDtypeStruct((b, p, d), BF16),
        scratch_shapes=[pltpu.VMEM((p + 16, tc), F32),
                        pltpu.VMEM((p, tc), F32), pltpu.VMEM((p, tc), F32),
                        pltpu.VMEM((tt, tc), F32), pltpu.VMEM((tt, tc), F32),
                        pltpu.VMEM((tt, tc), F32), pltpu.VMEM((tt, tc), F32)],
        compiler_params=_params(("parallel", "parallel")),
        name="rglru",
    )(proj, proj, conv_w, conv_b.reshape(1, d), wg, gate_b, lam)


def _na_window(rb, rows):
    win = NA_QROWS + NA_ROWS - 1
    return min(max(NA_QROWS * rb - NA_ROWS // 2, 0), rows - win)


def _na_bias(rpb, rows):
    nb = rows // NA_QROWS
    win = NA_QROWS + NA_ROWS - 1
    kr = min(NA_ROWS, rows)
    out = []
    for rb in (0, 1, nb - 1):
        w0 = _na_window(rb, rows)
        r_abs = NA_QROWS * rb + jnp.arange(NA_QROWS)
        r0 = jnp.clip(r_abs - kr // 2, 0, rows - kr)
        k_abs = w0 + jnp.arange(win)
        valid_r = (k_abs[None, :] >= r0[:, None]) & (k_abs[None, :] < r0[:, None] + kr)
        rel_r = jnp.clip(k_abs[None, :] - r_abs[:, None] + NA_ROWS - 1, 0, 2 * NA_ROWS - 2)
        col = jnp.arange(GRID_W)
        c0 = jnp.clip(col - NA_COLS // 2, 0, GRID_W - NA_COLS)
        valid_c = (col[None, :] >= c0[:, None]) & (col[None, :] < c0[:, None] + NA_COLS)
        rel_c = jnp.clip(col[None, :] - col[:, None] + NA_COLS - 1, 0, 2 * NA_COLS - 2)
        bias = rpb[:, rel_r[:, None, :, None], rel_c[None, :, None, :]]
        valid = valid_r[:, None, :, None] & valid_c[None, :, None, :]
        bias = jnp.where(valid[None], bias, NEG_INF)
        out.append(bias.reshape(rpb.shape[0], NA_QROWS * GRID_W, win * GRID_W))
    return jnp.stack(out, axis=1)


def _na_kernel(q_ref, k_ref, v_ref, bias_ref, o_ref, km_ref, *, n_ctx, rows):
    hd = LANES // 2
    scale = hd ** -0.5
    nq = NA_QROWS * GRID_W
    nb = rows // NA_QROWS
    nk = (NA_QROWS + NA_ROWS - 1) * GRID_W
    lane = lax.broadcasted_iota(jnp.int32, (1, LANES), 1)
    first = lane < hd
    k_all = k_ref[0]
    km_ref[0] = jnp.where(first, k_all, jnp.zeros_like(k_all))
    km_ref[1] = jnp.where(first, jnp.zeros_like(k_all), k_all)

    def attend(qb, ks, bias_ty):
        outs = []
        for hh in range(2):
            s_ctx = _dot_nt(qb, km_ref[hh, 0:n_ctx, :]) * scale
            m = jnp.max(s_ctx, axis=-1, keepdims=True)
            if ks is not None:
                s_loc = _dot_nt(qb, km_ref[hh, ks:ks + nk, :]) * scale + bias_ref[hh, bias_ty]
                m = jnp.maximum(m, jnp.max(s_loc, axis=-1, keepdims=True))
                p_loc = jnp.exp(s_loc - m)
            p_ctx = jnp.exp(s_ctx - m)
            l = jnp.sum(p_ctx, axis=-1, keepdims=True)
            o = _dot(p_ctx.astype(BF16), v_ref[0, 0:n_ctx, :])
            if ks is not None:
                l = l + jnp.sum(p_loc, axis=-1, keepdims=True)
                o = o + _dot(p_loc.astype(BF16), v_ref[0, ks:ks + nk, :])
            outs.append(o * (1.0 / l))
        return jnp.where(first, outs[0], outs[1])

    o_ref[0, 0:n_ctx, :] = attend(q_ref[0, 0:n_ctx, :], None, None).astype(o_ref.dtype)
    for rb in range(nb):
        qs = n_ctx + rb * nq
        ks = n_ctx + _na_window(rb, rows) * GRID_W
        ty = 0 if rb == 0 else (2 if rb == nb - 1 else 1)
        o_ref[0, qs:qs + nq, :] = attend(q_ref[0, qs:qs + nq, :], ks, ty).astype(o_ref.dtype)


def _na(qkv, rpb, n_ctx):
    b, p, d3 = qkv.shape
    d = d3 // 3
    rows = (p - n_ctx) // GRID_W
    assert d // NA_HEADS == LANES // 2 and rows % NA_QROWS == 0 and rows // NA_QROWS >= 3
    nhp = d // LANES
    bias = _na_bias(rpb.astype(F32), rows)
    nq, nk = bias.shape[2], bias.shape[3]
    kern = functools.partial(_na_kernel, n_ctx=n_ctx, rows=rows)
    return pl.pallas_call(
        kern,
        grid=(nhp, b),
        in_specs=[pl.BlockSpec((1, p, LANES), lambda hp, bi: (bi, 0, hp)),
                  pl.BlockSpec((1, p, LANES), lambda hp, bi: (bi, 0, nhp + hp)),
                  pl.BlockSpec((1, p, LANES), lambda hp, bi: (bi, 0, 2 * nhp + hp)),
                  pl.BlockSpec((2, 3, nq, nk), lambda hp, bi: (hp, 0, 0, 0))],
        out_specs=pl.BlockSpec((1, p, LANES), lambda hp, bi: (bi, 0, hp)),
        out_shape=jax.ShapeDtypeStruct((b, p, d), BF16),
        scratch_shapes=[pltpu.VMEM((2, p, LANES), BF16)],
        compiler_params=_params(("parallel", "parallel")),
        name="nbr_attn",
    )(qkv, qkv, qkv, bias)


def _ret_tables(n_ctx, s, dk, ch):
    quarter = dk // 4
    pos = jnp.arange(s)
    inv = ROPE_BASE ** (-jnp.arange(quarter, dtype=F32) / quarter)
    ang_r = (pos // GRID_W).astype(F32)[:, None] * inv
    ang_c = (pos % GRID_W).astype(F32)[:, None] * inv
    cos = jnp.concatenate([jnp.cos(ang_r)] * 2 + [jnp.cos(ang_c)] * 2, axis=-1)
    sin = jnp.concatenate([-jnp.sin(ang_r), jnp.sin(ang_r), -jnp.sin(ang_c), jnp.sin(ang_c)], axis=-1)
    cos = jnp.concatenate([jnp.ones((n_ctx, dk), F32), cos], axis=0)
    sin = jnp.concatenate([jnp.zeros((n_ctx, dk), F32), sin], axis=0)
    log_gamma = jnp.log1p(-(2.0 ** (-5.0 - jnp.arange(RET_HEADS, dtype=F32))))[:, None, None]
    pq = jnp.arange(ch, dtype=F32)
    col = jnp.broadcast_to(pq[:, None], (ch, dk))[None]
    tabs = jnp.stack([
        jnp.exp(jnp.abs(pq[:, None] - pq[None, :])[None] * log_gamma),
        jnp.exp((col + 1.0) * log_gamma),
        jnp.exp((ch - col) * log_gamma),
        jnp.exp((ch - 1.0 - col) * log_gamma),
        jnp.exp(col * log_gamma),
    ], axis=1)
    chunk_decay = jnp.exp(ch * log_gamma[:, 0, 0])
    return cos, sin, tabs, chunk_decay


def _ret_kernel(cd_ref, q_ref, k_ref, v_ref, g_ref, cos_ref, sin_ref, tab_ref, o_ref,
                sb_ref, st_ref, *, ch, nc):
    dk = q_ref.shape[-1]
    cd = cd_ref[pl.program_id(1)]
    k_scale = dk ** -0.5

    def rope(t, c0):
        parts = []
        for hf in range(dk // LANES):
            sl = slice(hf * LANES, (hf + 1) * LANES)
            th = t[:, sl]
            parts.append(th * cos_ref[pl.ds(c0, ch), sl] + pltpu.roll(th, LANES // 2, 1) * sin_ref[pl.ds(c0, ch), sl])
        return jnp.concatenate(parts, axis=-1)

    def load_k(c0):
        return rope(k_ref[0, pl.ds(c0, ch), :].astype(F32) * k_scale, c0)

    def kv_outer(kdec, c0):
        return _dot_tn(kdec.astype(BF16), v_ref[0, pl.ds(c0, ch), :])

    nz = 1
    sb_ref[0] = jnp.zeros(sb_ref.shape[1:], BF16)
    st_ref[...] = kv_outer(load_k(0) * tab_ref[0, 4], 0)

    def bwd(j, _):
        c = nc - 1 - j
        c0 = pl.multiple_of(c * ch, ch)
        sb_ref[c] = st_ref[...].astype(BF16)
        st_ref[...] = cd * st_ref[...] + kv_outer(load_k(c0) * tab_ref[0, 4], c0)
        return 0

    lax.fori_loop(0, nc - 1 - nz, bwd, 0)
    sb_ref[nz] = st_ref[...].astype(BF16)

    st_ref[...] = jnp.zeros(st_ref.shape, F32)

    def fwd(c, _):
        c0 = pl.multiple_of(c * ch, ch)
        q = rope(q_ref[0, pl.ds(c0, ch), :].astype(F32), c0)
        k = load_k(c0)
        v = v_ref[0, pl.ds(c0, ch), :]
        inner = _dot_nt(q.astype(BF16), k.astype(BF16)) * tab_ref[0, 0]
        o = (_dot(inner.astype(BF16), v)
             + _dot((q * tab_ref[0, 1]).astype(BF16), st_ref[...].astype(BF16))
             + _dot((q * tab_ref[0, 2]).astype(BF16), sb_ref[c]))
        st_ref[...] = cd * st_ref[...] + kv_outer(k * tab_ref[0, 3], c0)
        o = o * lax.rsqrt(jnp.mean(o * o, axis=-1, keepdims=True) + RMS_EPS)
        g = g_ref[0, pl.ds(c0, ch), :].astype(F32)
        o_ref[0, pl.ds(c0, ch), :] = (o * (g * jax.nn.sigmoid(g))).astype(o_ref.dtype)
        return 0

    lax.fori_loop(0, nc, fwd, 0)


def _ret(qkvg, n_ctx):
    b, p, d6 = qkvg.shape
    d = d6 // 6
    dk = d // RET_HEADS
    dv = 2 * dk
    ch = dk
    assert dk == 2 * LANES and n_ctx == ch and p % ch == 0
    nc = p // ch
    cos, sin, tabs, chunk_decay = _ret_tables(n_ctx, p - n_ctx, dk, ch)
    kern = functools.partial(_ret_kernel, ch=ch, nc=nc)
    nh = RET_HEADS
    return pl.pallas_call(
        kern,
        grid_spec=pltpu.PrefetchScalarGridSpec(
            num_scalar_prefetch=1,
            grid=(b, nh),
            in_specs=[pl.BlockSpec((1, p, dk), lambda bi, h, cd: (bi, 0, h)),
                      pl.BlockSpec((1, p, dk), lambda bi, h, cd: (bi, 0, nh + h)),
                      pl.BlockSpec((1, p, dv), lambda bi, h, cd: (bi, 0, nh + h)),
                      pl.BlockSpec((1, p, dv), lambda bi, h, cd: (bi, 0, 2 * nh + h)),
                      pl.BlockSpec((p, dk), lambda bi, h, cd: (0, 0)),
                      pl.BlockSpec((p, dk), lambda bi, h, cd: (0, 0)),
                      pl.BlockSpec((1, 5, ch, dk), lambda bi, h, cd: (h, 0, 0, 0))],
            out_specs=pl.BlockSpec((1, p, dv), lambda bi, h, cd: (bi, 0, h)),
            scratch_shapes=[pltpu.VMEM((nc, dk, dv), BF16), pltpu.VMEM((dk, dv), F32)]),
        out_shape=jax.ShapeDtypeStruct((b, p, nh * dv), BF16),
        compiler_params=_params(("parallel", "parallel")),
        name="retention",
    )(chunk_decay, qkvg, qkvg, qkvg, qkvg, cos, sin, tabs)


def _router_kernel(lg_ref, rb_ref, tri_ref, e_ref, w_ref, rk_ref, cnt_ref, run_ref):
    @pl.when(pl.program_id(0) == 0)
    def _():
        run_ref[...] = jnp.zeros(run_ref.shape, F32)

    lg = lg_ref[...]
    ex = jnp.exp(lg - jnp.max(lg, axis=0, keepdims=True))
    probs = ex / jnp.sum(ex, axis=0, keepdims=True)
    sel = probs + rb_ref[...]
    epg = EXPERTS_PER_GROUP
    best = grp = cur = curp = None
    for g in range(N_GROUPS):
        s = [sel[g * epg + i:g * epg + i + 1, :] for i in range(epg)]
        pr = [probs[g * epg + i:g * epg + i + 1, :] for i in range(epg)]
        top2 = None
        for i in range(epg):
            for j in range(i + 1, epg):
                top2 = s[i] + s[j] if top2 is None else jnp.maximum(top2, s[i] + s[j])
        if g == 0:
            best, grp, cur, curp = top2, jnp.zeros(top2.shape, jnp.int32), s, pr
        else:
            better = top2 > best
            best = jnp.where(better, top2, best)
            grp = jnp.where(better, g, grp)
            cur = [jnp.where(better, s[i], cur[i]) for i in range(epg)]
            curp = [jnp.where(better, pr[i], curp[i]) for i in range(epg)]
    b1, i1, p1 = cur[0], jnp.zeros(best.shape, jnp.int32), curp[0]
    for i in range(1, epg):
        gt = cur[i] > b1
        b1, i1, p1 = jnp.where(gt, cur[i], b1), jnp.where(gt, i, i1), jnp.where(gt, curp[i], p1)
    b2 = i2 = p2 = None
    for i in range(epg):
        v = jnp.where(i1 == i, -jnp.inf, cur[i])
        if b2 is None:
            b2, i2, p2 = v, jnp.zeros(best.shape, jnp.int32), curp[0]
        else:
            gt = v > b2
            b2, i2, p2 = jnp.where(gt, v, b2), jnp.where(gt, i, i2), jnp.where(gt, curp[i], p2)
    e1 = grp * epg + i1
    e2 = grp * epg + i2
    inv = 1.0 / (p1 + p2)
    e_ref[0:1, :] = e1
    e_ref[1:2, :] = e2
    w_ref[0:1, :] = p1 * inv
    w_ref[1:2, :] = p2 * inv

    eidx = lax.broadcasted_iota(jnp.int32, lg.shape, 0)
    eq1 = eidx == e1
    eq2 = eidx == e2
    member = jnp.where(eq1, 1.0, jnp.where(eq2, 1.0, 0.0))
    before = _dot(member.astype(BF16), tri_ref[...]) + run_ref[...]
    rk_ref[0:1, :] = jnp.sum(jnp.where(eq1, before, 0.0), axis=0, keepdims=True).astype(jnp.int32)
    rk_ref[1:2, :] = jnp.sum(jnp.where(eq2, before, 0.0), axis=0, keepdims=True).astype(jnp.int32)
    run_ref[...] = run_ref[...] + jnp.sum(member, axis=1, keepdims=True)
    cnt_ref[...] = run_ref[...]


def _router(logits_t, router_b):
    e, n = logits_t.shape
    tt = 512
    assert n % tt == 0
    tri = (jnp.arange(tt)[:, None] < jnp.arange(tt)[None, :]).astype(BF16)
    kn = jax.ShapeDtypeStruct((TOP_K, n), jnp.int32)
    return pl.pallas_call(
        _router_kernel,
        grid=(n // tt,),
        in_specs=[pl.BlockSpec((e, tt), lambda i: (0, i)),
                  pl.BlockSpec((e, 1), lambda i: (0, 0)),
                  pl.BlockSpec((tt, tt), lambda i: (0, 0))],
        out_specs=[pl.BlockSpec((TOP_K, tt), lambda i: (0, i)),
                   pl.BlockSpec((TOP_K, tt), lambda i: (0, i)),
                   pl.BlockSpec((TOP_K, tt), lambda i: (0, i)),
                   pl.BlockSpec((e, 1), lambda i: (0, 0))],
        out_shape=[kn, jax.ShapeDtypeStruct((TOP_K, n), F32), kn, jax.ShapeDtypeStruct((e, 1), F32)],
        scratch_shapes=[pltpu.VMEM((e, 1), F32)],
        compiler_params=_params(("arbitrary",)),
        name="router",
    )(logits_t, router_b.astype(F32).reshape(e, 1), tri)


def _moe_kernel(be_ref, nu_ref, xs_ref, wg_ref, wu_ref, wd_ref, ys_ref, wgb, wub, wdb, *, fc):
    i = pl.program_id(0)
    used = i < nu_ref[0]
    new_expert = jnp.logical_or(i == 0, be_ref[i] != be_ref[jnp.maximum(i - 1, 0)])

    @pl.when(jnp.logical_and(used, new_expert))
    def _():
        wgb[...] = wg_ref[0, 0].astype(BF16)
        wub[...] = wu_ref[0, 0].astype(BF16)
        wdb[...] = wd_ref[0, 0].astype(BF16)

    @pl.when(used)
    def _():
        x = xs_ref[...]
        f = wgb.shape[-1]
        acc = jnp.zeros(ys_ref.shape, F32)
        for c in range(f // fc):
            sl = slice(c * fc, (c + 1) * fc)
            g = _dot(x, wgb[:, sl])
            u = _dot(x, wub[:, sl])
            a = (g * jax.nn.sigmoid(g) * u).astype(BF16)
            acc = acc + _dot(a, wdb[sl, :])
        ys_ref[...] = acc

    @pl.when(jnp.logical_not(used))
    def _():
        ys_ref[...] = jnp.zeros(ys_ref.shape, F32)


def _moe_experts(xs, block_expert, n_used, w_gate, w_up, w_down, layer):
    n_rows, d = xs.shape
    f = w_gate.shape[-1]
    bm = MOE_BLOCK
    kern = functools.partial(_moe_kernel, fc=_pick(f, (512, 256, 128)))
    return pl.pallas_call(
        kern,
        grid_spec=pltpu.PrefetchScalarGridSpec(
            num_scalar_prefetch=2,
            grid=(n_rows // bm,),
            in_specs=[pl.BlockSpec((bm, d), lambda i, be, nu: (i, 0)),
                      pl.BlockSpec((1, 1, d, f), lambda i, be, nu: (layer, be[i], 0, 0)),
                      pl.BlockSpec((1, 1, d, f), lambda i, be, nu: (layer, be[i], 0, 0)),
                      pl.BlockSpec((1, 1, f, d), lambda i, be, nu: (layer, be[i], 0, 0))],
            out_specs=pl.BlockSpec((bm, d), lambda i, be, nu: (i, 0)),
            scratch_shapes=[pltpu.VMEM((d, f), BF16), pltpu.VMEM((d, f), BF16), pltpu.VMEM((f, d), BF16)]),
        out_shape=jax.ShapeDtypeStruct((n_rows, d), F32),
        compiler_params=_params(("arbitrary",)),
        name="moe_experts",
    )(block_expert, n_used, xs, w_gate, w_up, w_down)


def _moe(h_rows, tok_ids, logits_t, router_b, w_gate, w_up, w_down, layer):
    n = tok_ids.shape[0]
    bm = MOE_BLOCK
    expert, weight, rank, counts = _router(logits_t, router_b)
    counts = counts[:, 0].astype(jnp.int32)
    padded = (counts + bm - 1) // bm * bm
    pad_end = jnp.cumsum(padded)
    pad_start = pad_end - padded
    dest = rank + jnp.sum(jnp.where(expert[..., None] == jnp.arange(N_EXPERTS), pad_start, 0), axis=-1)
    n_blocks = -(-(n * TOP_K) // bm) + N_EXPERTS
    n_rows = n_blocks * bm
    tok = tok_ids.astype(jnp.int32)
    src = (jnp.arange(n_rows, dtype=jnp.int32) % h_rows.shape[0]).at[dest.reshape(-1)].set(
        jnp.tile(tok, TOP_K), unique_indices=True)
    block_expert = jnp.minimum(
        jnp.searchsorted(pad_end, jnp.arange(n_blocks) * bm, side='right'), N_EXPERTS - 1).astype(jnp.int32)
    n_used = (pad_end[-1] // bm).astype(jnp.int32).reshape(1)
    ys = _moe_experts(h_rows[src], block_expert, n_used, w_gate, w_up, w_down, layer)
    return [ys[dest[k]] for k in range(TOP_K)], weight


def _combine_kernel(x_ref, y0_ref, y1_ref, w_ref, mod_ref, g_ref, o_ref, *, tm, n_ctx, off, final):
    is_ctx = _is_ctx((pl.program_id(1) + off) * tm, tm, n_ctx)
    w = w_ref[0]
    f = w[:, 0:1] * y0_ref[0] + w[:, 1:2] * y1_ref[0]
    xn = x_ref[0] + _mod_row(mod_ref, is_ctx, 5) * f
    if final:
        xn = xn * lax.rsqrt(jnp.mean(xn * xn, axis=-1, keepdims=True) + RMS_EPS) * g_ref[...]
    o_ref[0] = xn


def _combine(t, ys, w, mod, g, n_ctx, final):
    b, p, d = t.shape
    rows = ys[0].shape[1]
    tm = 256
    assert n_ctx % tm == 0 and rows % tm == 0 and len(ys) == 2
    off = (p - rows) // tm
    kern = functools.partial(_combine_kernel, tm=tm, n_ctx=n_ctx, off=off, final=final)
    row_spec = pl.BlockSpec((1, tm, d), lambda bi, i: (bi, i, 0))
    return pl.pallas_call(
        kern,
        grid=(b, rows // tm),
        in_specs=[pl.BlockSpec((1, tm, d), lambda bi, i: (bi, i + off, 0)),
                  row_spec, row_spec,
                  pl.BlockSpec((1, tm, TOP_K), lambda bi, i: (bi, i, 0)),
                  pl.BlockSpec((1, 2 * N_MOD, d), lambda bi, i: (bi, 0, 0)),
                  pl.BlockSpec((1, d), lambda bi, i: (0, 0))],
        out_specs=row_spec,
        out_shape=jax.ShapeDtypeStruct((b, rows, d), F32),
        compiler_params=_params(("parallel", "parallel")),
        name="ffn_residual",
    )(t, ys[0], ys[1], w, mod, g.reshape(1, d))


def kernel(x, c, ctx, c_ctx, ada_w, ada_b, norm_mix_g, norm_ffn_g, final_norm_g, lru_w_in, lru_conv_w, lru_conv_b, lru_gate_w, lru_gate_b, lru_lambda, lru_w_out, na_w_qkv, na_rpb, na_w_o, ret_w_qkvg, ret_w_o, router_w, router_b, moe_w_gate, moe_w_up, moe_w_down):
    b, s, d = x.shape
    n_ctx = ctx.shape[1]
    p = n_ctx + s
    depth = ada_w.shape[0]
    t = jnp.concatenate([ctx, x], axis=1)

    pad_rows = -(b + 1) % 16
    cc = jnp.concatenate([c, c_ctx[None, :], jnp.zeros((pad_rows, d), F32)], axis=0)
    mods = _ada(cc, ada_w, ada_b)
    mod_x = mods[:, :b].reshape(depth, b, N_MOD, d)
    mod_z = jnp.broadcast_to(mods[:, b].reshape(depth, 1, N_MOD, d), (depth, b, N_MOD, d))
    mods = jnp.concatenate([mod_z, mod_x], axis=2)

    rw32 = router_w.astype(F32)
    rw_hi = rw32.astype(BF16)
    rw_lo = (rw32 - rw_hi.astype(F32)).astype(BF16)
    rw = jnp.concatenate([rw_hi, rw_lo, jnp.zeros((d, ROUTER_LANES - 2 * N_EXPERTS), BF16)], axis=1)

    all_rows = jnp.arange(b * p, dtype=jnp.int32)
    out = None
    for i in range(depth):
        kind, j = i % 3, i // 3
        last = i == depth - 1
        mod = _at(mods, i)
        if kind == 0:
            proj = _norm_proj(t, _at(norm_mix_g, i), mod, _at(lru_w_in, j).astype(BF16), n_ctx, act_cols=d)
            a = _lru(proj, _at(lru_conv_w, j), _at(lru_conv_b, j), _at(lru_gate_w, j), _at(lru_gate_b, j),
                     _at(lru_lambda, j), n_ctx)
            w_o = _at(lru_w_out, j)
        elif kind == 1:
            qkv = _norm_proj(t, _at(norm_mix_g, i), mod, _at(na_w_qkv, j).astype(BF16), n_ctx)
            a = _na(qkv, _at(na_rpb, j), n_ctx)
            w_o = _at(na_w_o, j)
        else:
            qkvg = _norm_proj(t, _at(norm_mix_g, i), mod, _at(ret_w_qkvg, j).astype(BF16), n_ctx)
            a = _ret(qkvg, n_ctx)
            w_o = _at(ret_w_o, j)
        t, h2, pl_ = _out_proj(a, w_o.astype(BF16), t, mod, _at(norm_ffn_g, i), rw, n_ctx)
        logits = (pl_[..., :N_EXPERTS] + pl_[..., N_EXPERTS:2 * N_EXPERTS])
        if last:
            tok_ids = all_rows.reshape(b, p)[:, n_ctx:].reshape(-1)
            logits = logits[:, n_ctx:]
        else:
            tok_ids = all_rows
        ys, wts = _moe(h2.reshape(b * p, d), tok_ids, logits.reshape(-1, N_EXPERTS).T, router_b,
                       moe_w_gate, moe_w_up, moe_w_down, i)
        ys = [y.reshape(b, -1, d) for y in ys]
        wts = wts.T.reshape(b, -1, TOP_K)
        if last:
            out = _combine(t, ys, wts, mod, final_norm_g, n_ctx, final=True)
        else:
            t = _combine(t, ys, wts, mod, final_norm_g, n_ctx, final=False)
    return out
```

```python
import functools

import jax
import jax.numpy as jnp
import numpy as np
from jax import lax
from jax.experimental import pallas as pl
from jax.experimental.pallas import tpu as pltpu

F32 = jnp.float32
BF16 = jnp.bfloat16

GRID_W = 64
N_MOD = 6
RMS_EPS = 1e-6
LRU_BLOCKS = 8
CONV_W = 4
CONV_LEFT = CONV_W // 2
LRU_C = 8.0
NA_HEADS = 16
NA_ROWS = 8
NA_COLS = 16
NA_QROWS = 4
NEG_INF = -1e30
RET_HEADS = 4
ROPE_BASE = 10000.0
N_EXPERTS = 16
N_GROUPS = 4
EXPERTS_PER_GROUP = N_EXPERTS // N_GROUPS
TOP_K = 2
MOE_BLOCK = 512
LANES = 128
ROUTER_LANES = 128
SCAN_TILE = 256
VMEM_LIMIT = 56 * 1024 * 1024


def _pick(n, candidates):
    for c in candidates:
        if n % c == 0:
            return c
    raise ValueError(f"no tile in {candidates} divides {n}")


def _params(sem):
    return pltpu.CompilerParams(dimension_semantics=sem, vmem_limit_bytes=VMEM_LIMIT)


def _at(w, i):
    return lax.index_in_dim(w, i, axis=0, keepdims=False)


def _dot(a, b):
    return jnp.dot(a, b, preferred_element_type=F32)


def _dot_nt(a, b):
    return lax.dot_general(a, b, (((1,), (1,)), ((), ())), preferred_element_type=F32)


def _dot_tn(a, b):
    return lax.dot_general(a, b, (((0,), (0,)), ((), ())), preferred_element_type=F32)


def _is_ctx(pos0, tm, n_ctx):
    return (pos0 + lax.broadcasted_iota(jnp.int32, (tm, 1), 0)) < n_ctx


def _mod_row(mod_ref, is_ctx, k):
    return jnp.where(is_ctx, mod_ref[0, k:k + 1, :], mod_ref[0, N_MOD + k:N_MOD + k + 1, :])


def _norm_mod(x, g, mod_ref, is_ctx, k):
    y = x * lax.rsqrt(jnp.mean(x * x, axis=-1, keepdims=True) + RMS_EPS) * g
    return y * (1.0 + _mod_row(mod_ref, is_ctx, k + 1)) + _mod_row(mod_ref, is_ctx, k)


def _ada_kernel(cc_ref, w_ref, b_ref, o_ref):
    cc = cc_ref[...]
    s = (cc * jax.nn.sigmoid(cc)).astype(BF16)
    o_ref[0] = _dot(s, w_ref[0].astype(BF16)) + b_ref[0]


def _ada(cc, ada_w, ada_b):
    depth, d, n = ada_w.shape
    rows = cc.shape[0]
    tn = _pick(n, (1536, 1024, 512, 256, 128))
    return pl.pallas_call(
        _ada_kernel,
        grid=(depth, n // tn),
        in_specs=[pl.BlockSpec((rows, d), lambda i, j: (0, 0)),
                  pl.BlockSpec((1, d, tn), lambda i, j: (i, 0, j)),
                  pl.BlockSpec((1, 1, tn), lambda i, j: (i, 0, j))],
        out_specs=pl.BlockSpec((1, rows, tn), lambda i, j: (i, 0, j)),
        out_shape=jax.ShapeDtypeStruct((depth, rows, n), F32),
        compiler_params=_params(("parallel", "parallel")),
        name="ada_mod",
    )(cc, ada_w, ada_b.reshape(depth, 1, n))


def _ffn_sum(y0_ref, y1_ref, gw_ref):
    gw = gw_ref[0]
    return gw[:, 0:1] * y0_ref[0].astype(F32) + gw[:, 1:2] * y1_ref[0].astype(F32)


def _proj_kernel(*refs, tm, n_ctx, act_tiles, pending):
    if pending:
        x_ref, y0_ref, y1_ref, gw_ref, pmod_ref, g_ref, mod_ref, w_ref, o_ref, xo_ref, h_ref = refs
    else:
        x_ref, g_ref, mod_ref, w_ref, o_ref, h_ref = refs
    i, j = pl.program_id(1), pl.program_id(2)

    @pl.when(j == 0)
    def _():
        is_ctx = _is_ctx(i * tm, tm, n_ctx)
        x = x_ref[0]
        if pending:
            x = x + _mod_row(pmod_ref, is_ctx, 5) * _ffn_sum(y0_ref, y1_ref, gw_ref)
            xo_ref[0] = x
        h_ref[...] = _norm_mod(x, g_ref[...], mod_ref, is_ctx, 0).astype(BF16)

    y = _dot(h_ref[...], w_ref[...])
    if act_tiles:
        @pl.when(j < act_tiles)
        def _():
            o_ref[0] = jax.nn.gelu(y).astype(o_ref.dtype)

        @pl.when(j >= act_tiles)
        def _():
            o_ref[0] = y.astype(o_ref.dtype)
    else:
        o_ref[0] = y.astype(o_ref.dtype)


def _norm_proj(t, g, mod, w, n_ctx, act_cols=0, pending=None):
    b, p, d = t.shape
    n = w.shape[1]
    tm = _pick(p, (768, 512, 256))
    tn = _pick(n, (1024, 512))
    assert act_cols % tn == 0
    kern = functools.partial(_proj_kernel, tm=tm, n_ctx=n_ctx, act_tiles=act_cols // tn, pending=pending is not None)
    row_spec = pl.BlockSpec((1, tm, d), lambda bi, i, j: (bi, i, 0))
    mod_spec = pl.BlockSpec((1, 2 * N_MOD, d), lambda bi, i, j: (bi, 0, 0))
    in_specs = [row_spec]
    args = [t]
    if pending is not None:
        in_specs += [row_spec, row_spec, pl.BlockSpec((1, tm, TOP_K), lambda bi, i, j: (bi, i, 0)), mod_spec]
        args += [pending[0], pending[1], pending[2], pending[3]]
    in_specs += [pl.BlockSpec((1, d), lambda bi, i, j: (0, 0)), mod_spec,
                 pl.BlockSpec((d, tn), lambda bi, i, j: (0, j))]
    args += [g.reshape(1, d), mod, w]
    out_specs = [pl.BlockSpec((1, tm, tn), lambda bi, i, j: (bi, i, j))]
    out_shape = [jax.ShapeDtypeStruct((b, p, n), BF16)]
    if pending is not None:
        out_specs.append(row_spec)
        out_shape.append(jax.ShapeDtypeStruct((b, p, d), F32))
    res = pl.pallas_call(
        kern,
        grid=(b, p // tm, n // tn),
        in_specs=in_specs,
        out_specs=out_specs,
        out_shape=out_shape,
        scratch_shapes=[pltpu.VMEM((tm, d), BF16)],
        compiler_params=_params(("parallel", "parallel", "arbitrary")),
        name="norm_proj",
    )(*args)
    return (res[0], res[1]) if pending is not None else (res[0], t)


def _out_kernel(a_ref, w_ref, x_ref, mod_ref, g_ref, rw_ref, xo_ref, h_ref, p_ref, *, tm, n_ctx):
    is_ctx = _is_ctx(pl.program_id(1) * tm, tm, n_ctx)
    y = _dot(a_ref[0], w_ref[...])
    xn = x_ref[0] + _mod_row(mod_ref, is_ctx, 2) * y
    xo_ref[0] = xn
    h = _norm_mod(xn, g_ref[...], mod_ref, is_ctx, 3)
    hi = h.astype(BF16)
    h_ref[0] = hi
    lo = (h - hi.astype(F32)).astype(BF16)
    p_ref[0] = _dot(hi, rw_ref[...]) + _dot(lo, rw_ref[...])


def _out_proj(a, w, t, mod, g, rw, n_ctx):
    b, p, d = t.shape
    k = a.shape[-1]
    tm = _pick(p, (768, 512, 256))
    kern = functools.partial(_out_kernel, tm=tm, n_ctx=n_ctx)
    return pl.pallas_call(
        kern,
        grid=(b, p // tm),
        in_specs=[pl.BlockSpec((1, tm, k), lambda bi, i: (bi, i, 0)),
                  pl.BlockSpec((k, d), lambda bi, i: (0, 0)),
                  pl.BlockSpec((1, tm, d), lambda bi, i: (bi, i, 0)),
                  pl.BlockSpec((1, 2 * N_MOD, d), lambda bi, i: (bi, 0, 0)),
                  pl.BlockSpec((1, d), lambda bi, i: (0, 0)),
                  pl.BlockSpec((d, ROUTER_LANES), lambda bi, i: (0, 0))],
        out_specs=[pl.BlockSpec((1, tm, d), lambda bi, i: (bi, i, 0)),
                   pl.BlockSpec((1, tm, d), lambda bi, i: (bi, i, 0)),
                   pl.BlockSpec((1, tm, ROUTER_LANES), lambda bi, i: (bi, i, 0))],
        out_shape=[jax.ShapeDtypeStruct((b, p, d), F32),
                   jax.ShapeDtypeStruct((b, p, d), BF16),
                   jax.ShapeDtypeStruct((b, p, ROUTER_LANES), F32)],
        compiler_params=_params(("parallel", "parallel")),
        name="out_proj",
    )(a, w, t, mod, g.reshape(1, d), rw)


def _lru_kernel(gt_ref, up_ref, cw_ref, cb_ref, wg_ref, gb_ref, lam_ref, o_ref,
                upad, hf_s, hb_s, af, bf, ab, bb, *, n_ctx, tt, nt):
    p = nt * tt
    tc = o_ref.shape[-1]
    pad = 8
    upad[0:pad, :] = jnp.zeros((pad, tc), F32)
    upad[pad + p:pad + p + pad, :] = jnp.zeros((pad, tc), F32)
    for t in range(nt):
        upad[pad + t * tt:pad + (t + 1) * tt, :] = up_ref[0, t * tt:(t + 1) * tt, :].astype(F32)

    neg_lam = -lam_ref[...]
    sp = jnp.maximum(neg_lam, 0.0) + jnp.log1p(jnp.exp(-jnp.abs(neg_lam)))
    rowi = lax.broadcasted_iota(jnp.int32, (tt, 1), 0)

    def conv_tile(t):
        r0 = t * tt
        first = r0 in (0, n_ctx)
        last = r0 + tt in (n_ctx, p)
        acc = jnp.broadcast_to(cb_ref[...], (tt, tc))
        for kk in range(CONV_W):
            d = kk - CONV_LEFT
            xs = upad[pad + r0 + d:pad + r0 + d + tt, :]
            if first and d < 0:
                xs = jnp.where(rowi >= -d, xs, 0.0)
            if last and d > 0:
                xs = jnp.where(rowi < tt - d, xs, 0.0)
            acc = acc + cw_ref[kk:kk + 1, :] * xs
        return acc

    def gates(u, d, a_ref, b_ref):
        ub = u.astype(BF16)
        for n in range(tc // LANES):
            sl = slice(n * LANES, (n + 1) * LANES)
            ri = _dot(ub[:, sl], wg_ref[d, n])
            r = jax.nn.sigmoid(ri[:, :LANES] + gb_ref[d, 0:1, sl])
            ig = jax.nn.sigmoid(ri[:, LANES:] + gb_ref[d, 1:2, sl])
            log_a = (-LRU_C) * r * sp[d:d + 1, sl]
            a = jnp.exp(log_a)
            one_minus_a2 = -jnp.tanh(log_a) * (a * a + 1.0)
            a_ref[:, sl] = a
            b_ref[:, sl] = jnp.sqrt(one_minus_a2) * (ig * u[:, sl])

    nz = n_ctx // tt
    fwd_order = list(range(nt))
    bwd_order = list(range(nz - 1, -1, -1)) + list(range(nt - 1, nz - 1, -1))
    carry = (jnp.zeros((1, tc), F32), jnp.zeros((1, tc), F32))
    for tf, tb in zip(fwd_order, bwd_order):
        gates(conv_tile(tf), 0, af, bf)
        gates(conv_tile(tb), 1, ab, bb)

        def step(j, c, tf=tf, tb=tb):
            hf, hb = c
            hf = af[pl.ds(j, 1), :] * hf + bf[pl.ds(j, 1), :]
            hf_s[pl.ds(tf * tt + j, 1), :] = hf
            jb = tt - 1 - j
            hb = ab[pl.ds(jb, 1), :] * hb + bb[pl.ds(jb, 1), :]
            hb_s[pl.ds(tb * tt + jb, 1), :] = hb
            return hf, hb

        carry = lax.fori_loop(0, tt, step, carry, unroll=8)

    for t in range(nt):
        rs = slice(t * tt, (t + 1) * tt)
        o_ref[0, rs, :] = ((hf_s[rs, :] + hb_s[rs, :]) * gt_ref[0, rs, :].astype(F32)).astype(o_ref.dtype)


def _lru(proj, conv_w, conv_b, gate_w, gate_b, lam, n_ctx):
    b, p, d2 = proj.shape
    d = d2 // 2
    tc = 512
    tt = SCAN_TILE
    assert p % tt == 0 and n_ctx % tt == 0 and d % tc == 0 and d // LRU_BLOCKS == LANES
    nt = p // tt
    nct = d // tc
    wg = jnp.concatenate([gate_w[:, 0], gate_w[:, 1]], axis=-1).astype(BF16)
    kern = functools.partial(_lru_kernel, n_ctx=n_ctx, tt=tt, nt=nt)
    return pl.pallas_call(
        kern,
        grid=(b, nct),
        in_specs=[pl.BlockSpec((1, p, tc), lambda bi, ci: (bi, 0, ci)),
                  pl.BlockSpec((1, p, tc), lambda bi, ci: (bi, 0, nct + ci)),
                  pl.BlockSpec((CONV_W, tc), lambda bi, ci: (0, ci)),
                  pl.BlockSpec((1, tc), lambda bi, ci: (0, ci)),
                  pl.BlockSpec((2, tc // LANES, LANES, 2 * LANES), lambda bi, ci: (0, ci, 0, 0)),
                  pl.BlockSpec((2, 2, tc), lambda bi, ci: (0, 0, ci)),
                  pl.BlockSpec((2, tc), lambda bi, ci: (0, ci))],
        out_specs=pl.BlockSpec((1, p, tc), lambda bi, ci: (bi, 0, ci)),
        out_shape=jax.ShapeDtypeStruct((b, p, d), BF16),
        scratch_shapes=[pltpu.VMEM((p + 16, tc), F32),
                        pltpu.VMEM((p, tc), F32), pltpu.VMEM((p, tc), F32),
                        pltpu.VMEM((tt, tc), F32), pltpu.VMEM((tt, tc), F32),
                        pltpu.VMEM((tt, tc), F32), pltpu.VMEM((tt, tc), F32)],
        compiler_params=_params(("parallel", "parallel")),
        name="rglru",
    )(proj, proj, conv_w, conv_b.reshape(1, d), wg, gate_b, lam)


def _na_window(rb, rows):
    win = NA_QROWS + NA_ROWS - 1
    return min(max(NA_QROWS * rb - NA_ROWS // 2, 0), rows - win)


def _na_bias(rpb, rows):
    nb = rows // NA_QROWS
    win = NA_QROWS + NA_ROWS - 1
    kr = min(NA_ROWS, rows)
    n_rel_r, n_rel_c = 2 * NA_ROWS - 1, 2 * NA_COLS - 1
    col = np.arange(GRID_W)
    c0 = np.clip(col - NA_COLS // 2, 0, GRID_W - NA_COLS)
    valid_c = (col[None, :] >= c0[:, None]) & (col[None, :] < c0[:, None] + NA_COLS)
    rel_c = np.clip(col[None, :] - col[:, None] + NA_COLS - 1, 0, n_rel_c - 1)
    pick_c = (rel_c[..., None] == np.arange(n_rel_c)).astype(np.float32)
    pick_r, valid = [], []
    for rb in (0, 1, nb - 1):
        r_abs = NA_QROWS * rb + np.arange(NA_QROWS)
        r0 = np.clip(r_abs - kr // 2, 0, rows - kr)
        k_abs = _na_window(rb, rows) + np.arange(win)
        valid_r = (k_abs[None, :] >= r0[:, None]) & (k_abs[None, :] < r0[:, None] + kr)
        rel_r = np.clip(k_abs[None, :] - r_abs[:, None] + NA_ROWS - 1, 0, n_rel_r - 1)
        pick_r.append((rel_r[..., None] == np.arange(n_rel_r)).astype(np.float32))
        valid.append(valid_r[:, None, :, None] & valid_c[None, :, None, :])
    rows_sel = jnp.einsum('hab,tqwa->htqwb', rpb, jnp.asarray(np.stack(pick_r)), precision=lax.Precision.HIGHEST)
    bias = jnp.einsum('htqwb,ckb->htqcwk', rows_sel, jnp.asarray(pick_c), precision=lax.Precision.HIGHEST)
    bias = jnp.where(jnp.asarray(np.stack(valid))[None], bias, NEG_INF)
    return bias.reshape(rpb.shape[0], 3, NA_QROWS * GRID_W, win * GRID_W)


def _na_kernel(q_ref, k_ref, v_ref, bias_ref, o_ref, km_ref, *, n_ctx, rows):
    hd = LANES // 2
    scale = hd ** -0.5
    nq = NA_QROWS * GRID_W
    nb = rows // NA_QROWS
    nk = (NA_QROWS + NA_ROWS - 1) * GRID_W
    lane = lax.broadcasted_iota(jnp.int32, (1, LANES), 1)
    first = lane < hd
    k_all = k_ref[0]
    km_ref[0] = jnp.where(first, k_all, jnp.zeros_like(k_all))
    km_ref[1] = jnp.where(first, jnp.zeros_like(k_all), k_all)

    def attend(qb, ks, bias_ty):
        outs = []
        for hh in range(2):
            s_ctx = _dot_nt(qb, km_ref[hh, 0:n_ctx, :]) * scale
            m = jnp.max(s_ctx, axis=-1, keepdims=True)
            if ks is not None:
                s_loc = _dot_nt(qb, km_ref[hh, ks:ks + nk, :]) * scale + bias_ref[hh, bias_ty]
                m = jnp.maximum(m, jnp.max(s_loc, axis=-1, keepdims=True))
                p_loc = jnp.exp(s_loc - m)
            p_ctx = jnp.exp(s_ctx - m)
            l = jnp.sum(p_ctx, axis=-1, keepdims=True)
            o = _dot(p_ctx.astype(BF16), v_ref[0, 0:n_ctx, :])
            if ks is not None:
                l = l + jnp.sum(p_loc, axis=-1, keepdims=True)
                o = o + _dot(p_loc.astype(BF16), v_ref[0, ks:ks + nk, :])
            outs.append(o * (1.0 / l))
        return jnp.where(first, outs[0], outs[1])

    o_ref[0, 0:n_ctx, :] = attend(q_ref[0, 0:n_ctx, :], None, None).astype(o_ref.dtype)
    for rb in range(nb):
        qs = n_ctx + rb * nq
        ks = n_ctx + _na_window(rb, rows) * GRID_W
        ty = 0 if rb == 0 else (2 if rb == nb - 1 else 1)
        o_ref[0, qs:qs + nq, :] = attend(q_ref[0, qs:qs + nq, :], ks, ty).astype(o_ref.dtype)


def _na(qkv, rpb, n_ctx):
    b, p, d3 = qkv.shape
    d = d3 // 3
    rows = (p - n_ctx) // GRID_W
    assert d // NA_HEADS == LANES // 2 and rows % NA_QROWS == 0 and rows // NA_QROWS >= 3
    nhp = d // LANES
    bias = _na_bias(rpb.astype(F32), rows)
    nq, nk = bias.shape[2], bias.shape[3]
    kern = functools.partial(_na_kernel, n_ctx=n_ctx, rows=rows)
    return pl.pallas_call(
        kern,
        grid=(nhp, b),
        in_specs=[pl.BlockSpec((1, p, LANES), lambda hp, bi: (bi, 0, hp)),
                  pl.BlockSpec((1, p, LANES), lambda hp, bi: (bi, 0, nhp + hp)),
                  pl.BlockSpec((1, p, LANES), lambda hp, bi: (bi, 0, 2 * nhp + hp)),
                  pl.BlockSpec((2, 3, nq, nk), lambda hp, bi: (hp, 0, 0, 0))],
        out_specs=pl.BlockSpec((1, p, LANES), lambda hp, bi: (bi, 0, hp)),
        out_shape=jax.ShapeDtypeStruct((b, p, d), BF16),
        scratch_shapes=[pltpu.VMEM((2, p, LANES), BF16)],
        compiler_params=_params(("parallel", "parallel")),
        name="nbr_attn",
    )(qkv, qkv, qkv, bias)


def _ret_tables(n_ctx, s, dk, ch):
    quarter = dk // 4
    pos = jnp.arange(s)
    inv = ROPE_BASE ** (-jnp.arange(quarter, dtype=F32) / quarter)
    ang_r = (pos // GRID_W).astype(F32)[:, None] * inv
    ang_c = (pos % GRID_W).astype(F32)[:, None] * inv
    cos = jnp.concatenate([jnp.cos(ang_r)] * 2 + [jnp.cos(ang_c)] * 2, axis=-1)
    sin = jnp.concatenate([-jnp.sin(ang_r), jnp.sin(ang_r), -jnp.sin(ang_c), jnp.sin(ang_c)], axis=-1)
    cos = jnp.concatenate([jnp.ones((n_ctx, dk), F32), cos], axis=0)
    sin = jnp.concatenate([jnp.zeros((n_ctx, dk), F32), sin], axis=0)
    log_gamma = jnp.log1p(-(2.0 ** (-5.0 - jnp.arange(RET_HEADS, dtype=F32))))[:, None, None]
    pq = jnp.arange(ch, dtype=F32)
    col = jnp.broadcast_to(pq[:, None], (ch, dk))[None]
    tabs = jnp.stack([
        jnp.exp(jnp.abs(pq[:, None] - pq[None, :])[None] * log_gamma),
        jnp.exp((col + 1.0) * log_gamma),
        jnp.exp((ch - col) * log_gamma),
        jnp.exp((ch - 1.0 - col) * log_gamma),
        jnp.exp(col * log_gamma),
    ], axis=1)
    chunk_decay = jnp.exp(ch * log_gamma[:, 0, 0])
    return cos, sin, tabs, chunk_decay


def _ret_kernel(cd_ref, q_ref, k_ref, v_ref, g_ref, cos_ref, sin_ref, tab_ref, o_ref,
                sb_ref, st_ref, *, ch, nc):
    dk = q_ref.shape[-1]
    cd = cd_ref[pl.program_id(1)]
    k_scale = dk ** -0.5

    def rope(t, c0):
        parts = []
        for hf in range(dk // LANES):
            sl = slice(hf * LANES, (hf + 1) * LANES)
            th = t[:, sl]
            parts.append(th * cos_ref[pl.ds(c0, ch), sl] + pltpu.roll(th, LANES // 2, 1) * sin_ref[pl.ds(c0, ch), sl])
        return jnp.concatenate(parts, axis=-1)

    def load_k(c0):
        return rope(k_ref[0, pl.ds(c0, ch), :].astype(F32) * k_scale, c0)

    def kv_outer(kdec, c0):
        return _dot_tn(kdec.astype(BF16), v_ref[0, pl.ds(c0, ch), :])

    nz = 1
    sb_ref[0] = jnp.zeros(sb_ref.shape[1:], BF16)
    st_ref[...] = kv_outer(load_k(0) * tab_ref[0, 4], 0)

    def bwd(j, _):
        c = nc - 1 - j
        c0 = pl.multiple_of(c * ch, ch)
        sb_ref[c] = st_ref[...].astype(BF16)
        st_ref[...] = cd * st_ref[...] + kv_outer(load_k(c0) * tab_ref[0, 4], c0)
        return 0

    lax.fori_loop(0, nc - 1 - nz, bwd, 0)
    sb_ref[nz] = st_ref[...].astype(BF16)

    st_ref[...] = jnp.zeros(st_ref.shape, F32)

    def fwd(c, _):
        c0 = pl.multiple_of(c * ch, ch)
        q = rope(q_ref[0, pl.ds(c0, ch), :].astype(F32), c0)
        k = load_k(c0)
        v = v_ref[0, pl.ds(c0, ch), :]
        inner = _dot_nt(q.astype(BF16), k.astype(BF16)) * tab_ref[0, 0]
        o = (_dot(inner.astype(BF16), v)
             + _dot((q * tab_ref[0, 1]).astype(BF16), st_ref[...].astype(BF16))
             + _dot((q * tab_ref[0, 2]).astype(BF16), sb_ref[c]))
        st_ref[...] = cd * st_ref[...] + kv_outer(k * tab_ref[0, 3], c0)
        o = o * lax.rsqrt(jnp.mean(o * o, axis=-1, keepdims=True) + RMS_EPS)
        g = g_ref[0, pl.ds(c0, ch), :].astype(F32)
        o_ref[0, pl.ds(c0, ch), :] = (o * (g * jax.nn.sigmoid(g))).astype(o_ref.dtype)
        return 0

    lax.fori_loop(0, nc, fwd, 0)


def _ret(qkvg, n_ctx):
    b, p, d6 = qkvg.shape
    d = d6 // 6
    dk = d // RET_HEADS
    dv = 2 * dk
    ch = dk
    assert dk == 2 * LANES and n_ctx == ch and p % ch == 0
    nc = p // ch
    cos, sin, tabs, chunk_decay = _ret_tables(n_ctx, p - n_ctx, dk, ch)
    kern = functools.partial(_ret_kernel, ch=ch, nc=nc)
    nh = RET_HEADS
    return pl.pallas_call(
        kern,
        grid_spec=pltpu.PrefetchScalarGridSpec(
            num_scalar_prefetch=1,
            grid=(b, nh),
            in_specs=[pl.BlockSpec((1, p, dk), lambda bi, h, cd: (bi, 0, h)),
                      pl.BlockSpec((1, p, dk), lambda bi, h, cd: (bi, 0, nh + h)),
                      pl.BlockSpec((1, p, dv), lambda bi, h, cd: (bi, 0, nh + h)),
                      pl.BlockSpec((1, p, dv), lambda bi, h, cd: (bi, 0, 2 * nh + h)),
                      pl.BlockSpec((p, dk), lambda bi, h, cd: (0, 0)),
                      pl.BlockSpec((p, dk), lambda bi, h, cd: (0, 0)),
                      pl.BlockSpec((1, 5, ch, dk), lambda bi, h, cd: (h, 0, 0, 0))],
            out_specs=pl.BlockSpec((1, p, dv), lambda bi, h, cd: (bi, 0, h)),
            scratch_shapes=[pltpu.VMEM((nc, dk, dv), BF16), pltpu.VMEM((dk, dv), F32)]),
        out_shape=jax.ShapeDtypeStruct((b, p, nh * dv), BF16),
        compiler_params=_params(("parallel", "parallel")),
        name="retention",
    )(chunk_decay, qkvg, qkvg, qkvg, qkvg, cos, sin, tabs)


def _router_kernel(lg_ref, rb_ref, tri_ref, e_ref, w_ref, rk_ref, cnt_ref, run_ref):
    @pl.when(pl.program_id(0) == 0)
    def _():
        run_ref[...] = jnp.zeros(run_ref.shape, F32)

    lg = lg_ref[...]
    ex = jnp.exp(lg - jnp.max(lg, axis=0, keepdims=True))
    probs = ex / jnp.sum(ex, axis=0, keepdims=True)
    sel = probs + rb_ref[...]
    epg = EXPERTS_PER_GROUP
    best = grp = cur = curp = None
    for g in range(N_GROUPS):
        s = [sel[g * epg + i:g * epg + i + 1, :] for i in range(epg)]
        pr = [probs[g * epg + i:g * epg + i + 1, :] for i in range(epg)]
        top2 = None
        for i in range(epg):
            for j in range(i + 1, epg):
                top2 = s[i] + s[j] if top2 is None else jnp.maximum(top2, s[i] + s[j])
        if g == 0:
            best, grp, cur, curp = top2, jnp.zeros(top2.shape, jnp.int32), s, pr
        else:
            better = top2 > best
            best = jnp.where(better, top2, best)
            grp = jnp.where(better, g, grp)
            cur = [jnp.where(better, s[i], cur[i]) for i in range(epg)]
            curp = [jnp.where(better, pr[i], curp[i]) for i in range(epg)]
    b1, i1, p1 = cur[0], jnp.zeros(best.shape, jnp.int32), curp[0]
    for i in range(1, epg):
        gt = cur[i] > b1
        b1, i1, p1 = jnp.where(gt, cur[i], b1), jnp.where(gt, i, i1), jnp.where(gt, curp[i], p1)
    b2 = i2 = p2 = None
    for i in range(epg):
        v = jnp.where(i1 == i, -jnp.inf, cur[i])
        if b2 is None:
            b2, i2, p2 = v, jnp.zeros(best.shape, jnp.int32), curp[0]
        else:
            gt = v > b2
            b2, i2, p2 = jnp.where(gt, v, b2), jnp.where(gt, i, i2), jnp.where(gt, curp[i], p2)
    e1 = grp * epg + i1
    e2 = grp * epg + i2
    inv = 1.0 / (p1 + p2)
    e_ref[0:1, :] = e1
    e_ref[1:2, :] = e2
    w_ref[0:1, :] = p1 * inv
    w_ref[1:2, :] = p2 * inv

    eidx = lax.broadcasted_iota(jnp.int32, lg.shape, 0)
    eq1 = eidx == e1
    eq2 = eidx == e2
    member = jnp.where(eq1, 1.0, jnp.where(eq2, 1.0, 0.0))
    before = _dot(member.astype(BF16), tri_ref[...]) + run_ref[...]
    rk_ref[0:1, :] = jnp.sum(jnp.where(eq1, before, 0.0), axis=0, keepdims=True).astype(jnp.int32)
    rk_ref[1:2, :] = jnp.sum(jnp.where(eq2, before, 0.0), axis=0, keepdims=True).astype(jnp.int32)
    run_ref[...] = run_ref[...] + jnp.sum(member, axis=1, keepdims=True)
    cnt_ref[...] = run_ref[...]


def _router(logits_t, router_b):
    e, n = logits_t.shape
    tt = 512
    assert n % tt == 0
    tri = (jnp.arange(tt)[:, None] < jnp.arange(tt)[None, :]).astype(BF16)
    kn = jax.ShapeDtypeStruct((TOP_K, n), jnp.int32)
    return pl.pallas_call(
        _router_kernel,
        grid=(n // tt,),
        in_specs=[pl.BlockSpec((e, tt), lambda i: (0, i)),
                  pl.BlockSpec((e, 1), lambda i: (0, 0)),
                  pl.BlockSpec((tt, tt), lambda i: (0, 0))],
        out_specs=[pl.BlockSpec((TOP_K, tt), lambda i: (0, i)),
                   pl.BlockSpec((TOP_K, tt), lambda i: (0, i)),
                   pl.BlockSpec((TOP_K, tt), lambda i: (0, i)),
                   pl.BlockSpec((e, 1), lambda i: (0, 0))],
        out_shape=[kn, jax.ShapeDtypeStruct((TOP_K, n), F32), kn, jax.ShapeDtypeStruct((e, 1), F32)],
        scratch_shapes=[pltpu.VMEM((e, 1), F32)],
        compiler_params=_params(("arbitrary",)),
        name="router",
    )(logits_t, router_b.astype(F32).reshape(e, 1), tri)


def _moe_kernel(be_ref, nu_ref, xs_ref, wg_ref, wu_ref, wd_ref, ys_ref, wgb, wub, wdb, *, fc):
    i = pl.program_id(0)
    used = i < nu_ref[0]
    new_expert = jnp.logical_or(i == 0, be_ref[i] != be_ref[jnp.maximum(i - 1, 0)])

    @pl.when(jnp.logical_and(used, new_expert))
    def _():
        wgb[...] = wg_ref[0, 0].astype(BF16)
        wub[...] = wu_ref[0, 0].astype(BF16)
        wdb[...] = wd_ref[0, 0].astype(BF16)

    @pl.when(used)
    def _():
        x = xs_ref[...]
        f = wgb.shape[-1]
        acc = jnp.zeros(ys_ref.shape, F32)
        for c in range(f // fc):
            sl = slice(c * fc, (c + 1) * fc)
            g = _dot(x, wgb[:, sl])
            u = _dot(x, wub[:, sl])
            a = (g * jax.nn.sigmoid(g) * u).astype(BF16)
            acc = acc + _dot(a, wdb[sl, :])
        ys_ref[...] = acc.astype(ys_ref.dtype)

    @pl.when(jnp.logical_not(used))
    def _():
        ys_ref[...] = jnp.zeros(ys_ref.shape, ys_ref.dtype)


def _moe_experts(xs, block_expert, n_used, w_gate, w_up, w_down, layer):
    n_rows, d = xs.shape
    f = w_gate.shape[-1]
    bm = MOE_BLOCK
    kern = functools.partial(_moe_kernel, fc=_pick(f, (512, 256, 128)))
    return pl.pallas_call(
        kern,
        grid_spec=pltpu.PrefetchScalarGridSpec(
            num_scalar_prefetch=2,
            grid=(n_rows // bm,),
            in_specs=[pl.BlockSpec((bm, d), lambda i, be, nu: (i, 0)),
                      pl.BlockSpec((1, 1, d, f), lambda i, be, nu: (layer, be[i], 0, 0)),
                      pl.BlockSpec((1, 1, d, f), lambda i, be, nu: (layer, be[i], 0, 0)),
                      pl.BlockSpec((1, 1, f, d), lambda i, be, nu: (layer, be[i], 0, 0))],
            out_specs=pl.BlockSpec((bm, d), lambda i, be, nu: (i, 0)),
            scratch_shapes=[pltpu.VMEM((d, f), BF16), pltpu.VMEM((d, f), BF16), pltpu.VMEM((f, d), BF16)]),
        out_shape=jax.ShapeDtypeStruct((n_rows, d), BF16),
        compiler_params=_params(("arbitrary",)),
        name="moe_experts",
    )(block_expert, n_used, xs, w_gate, w_up, w_down)


def _moe(h_rows, tok_ids, logits_t, router_b, w_gate, w_up, w_down, layer):
    n = tok_ids.shape[0]
    bm = MOE_BLOCK
    expert, weight, rank, counts = _router(logits_t, router_b)
    counts = counts[:, 0].astype(jnp.int32)
    padded = (counts + bm - 1) // bm * bm
    pad_end = jnp.cumsum(padded)
    pad_start = pad_end - padded
    dest = rank + jnp.sum(jnp.where(expert[..., None] == jnp.arange(N_EXPERTS), pad_start, 0), axis=-1)
    n_blocks = -(-(n * TOP_K) // bm) + N_EXPERTS
    n_rows = n_blocks * bm
    tok = tok_ids.astype(jnp.int32)
    src = (jnp.arange(n_rows, dtype=jnp.int32) % h_rows.shape[0]).at[dest.reshape(-1)].set(
        jnp.tile(tok, TOP_K), unique_indices=True)
    block_start = jnp.arange(n_blocks, dtype=jnp.int32) * bm
    block_expert = jnp.minimum(
        jnp.sum((pad_end[None, :] <= block_start[:, None]).astype(jnp.int32), axis=1), N_EXPERTS - 1)
    n_used = (pad_end[-1] // bm).astype(jnp.int32).reshape(1)
    ys = _moe_experts(h_rows[src], block_expert, n_used, w_gate, w_up, w_down, layer)
    return [ys[dest[k]] for k in range(TOP_K)], weight


def _final_kernel(x_ref, y0_ref, y1_ref, gw_ref, mod_ref, g_ref, o_ref):
    gate = mod_ref[0, 2 * N_MOD - 1:2 * N_MOD, :]
    xn = x_ref[0] + gate * _ffn_sum(y0_ref, y1_ref, gw_ref)
    o_ref[0] = xn * lax.rsqrt(jnp.mean(xn * xn, axis=-1, keepdims=True) + RMS_EPS) * g_ref[...]


def _final(t, ys, gw, mod, g, n_ctx):
    b, p, d = t.shape
    rows = p - n_ctx
    tm = _pick(n_ctx, (512, 256))
    assert rows % tm == 0
    off = n_ctx // tm
    row_spec = pl.BlockSpec((1, tm, d), lambda bi, i: (bi, i, 0))
    return pl.pallas_call(
        _final_kernel,
        grid=(b, rows // tm),
        in_specs=[pl.BlockSpec((1, tm, d), lambda bi, i: (bi, i + off, 0)),
                  row_spec, row_spec,
                  pl.BlockSpec((1, tm, TOP_K), lambda bi, i: (bi, i, 0)),
                  pl.BlockSpec((1, 2 * N_MOD, d), lambda bi, i: (bi, 0, 0)),
                  pl.BlockSpec((1, d), lambda bi, i: (0, 0))],
        out_specs=row_spec,
        out_shape=jax.ShapeDtypeStruct((b, rows, d), F32),
        compiler_params=_params(("parallel", "parallel")),
        name="ffn_final",
    )(t, ys[0], ys[1], gw, mod, g.reshape(1, d))


def kernel(x, c, ctx, c_ctx, ada_w, ada_b, norm_mix_g, norm_ffn_g, final_norm_g, lru_w_in, lru_conv_w, lru_conv_b, lru_gate_w, lru_gate_b, lru_lambda, lru_w_out, na_w_qkv, na_rpb, na_w_o, ret_w_qkvg, ret_w_o, router_w, router_b, moe_w_gate, moe_w_up, moe_w_down):
    b, s, d = x.shape
    n_ctx = ctx.shape[1]
    p = n_ctx + s
    depth = ada_w.shape[0]
    t = jnp.concatenate([ctx, x], axis=1)

    pad_rows = -(b + 1) % 16
    cc = jnp.concatenate([c, c_ctx[None, :], jnp.zeros((pad_rows, d), F32)], axis=0)
    mods = _ada(cc, ada_w, ada_b)
    mod_x = mods[:, :b].reshape(depth, b, N_MOD, d)
    mod_z = jnp.broadcast_to(mods[:, b].reshape(depth, 1, N_MOD, d), (depth, b, N_MOD, d))
    mods = jnp.concatenate([mod_z, mod_x], axis=2)

    rw32 = router_w.astype(F32)
    rw_hi = rw32.astype(BF16)
    rw_lo = (rw32 - rw_hi.astype(F32)).astype(BF16)
    rw = jnp.concatenate([rw_hi, rw_lo, jnp.zeros((d, ROUTER_LANES - 2 * N_EXPERTS), BF16)], axis=1)

    all_rows = jnp.arange(b * p, dtype=jnp.int32)
    pending = None
    for i in range(depth):
        kind, j = i % 3, i // 3
        last = i == depth - 1
        mod = _at(mods, i)
        if kind == 0:
            proj, t = _norm_proj(t, _at(norm_mix_g, i), mod, _at(lru_w_in, j).astype(BF16), n_ctx,
                                 act_cols=d, pending=pending)
            a = _lru(proj, _at(lru_conv_w, j), _at(lru_conv_b, j), _at(lru_gate_w, j), _at(lru_gate_b, j),
                     _at(lru_lambda, j), n_ctx)
            w_o = _at(lru_w_out, j)
        elif kind == 1:
            qkv, t = _norm_proj(t, _at(norm_mix_g, i), mod, _at(na_w_qkv, j).astype(BF16), n_ctx, pending=pending)
            a = _na(qkv, _at(na_rpb, j), n_ctx)
            w_o = _at(na_w_o, j)
        else:
            qkvg, t = _norm_proj(t, _at(norm_mix_g, i), mod, _at(ret_w_qkvg, j).astype(BF16), n_ctx, pending=pending)
            a = _ret(qkvg, n_ctx)
            w_o = _at(ret_w_o, j)
        t, h2, pl_ = _out_proj(a, w_o.astype(BF16), t, mod, _at(norm_ffn_g, i), rw, n_ctx)
        logits = (pl_[..., :N_EXPERTS] + pl_[..., N_EXPERTS:2 * N_EXPERTS])
        if last:
            tok_ids = all_rows.reshape(b, p)[:, n_ctx:].reshape(-1)
            logits = logits[:, n_ctx:]
        else:
            tok_ids = all_rows
        ys, wts = _moe(h2.reshape(b * p, d), tok_ids, logits.reshape(-1, N_EXPERTS).T, router_b,
                       moe_w_gate, moe_w_up, moe_w_down, i)
        ys = [y.reshape(b, -1, d) for y in ys]
        wts = wts.T.reshape(b, -1, TOP_K)
        pending = (ys[0], ys[1], wts, mod)
    return _final(t, ys, wts, mod, final_norm_g, n_ctx)
```

```python
import functools

import jax
import jax.numpy as jnp
import numpy as np
from jax import lax
from jax.experimental import pallas as pl
from jax.experimental.pallas import tpu as pltpu

F32 = jnp.float32
BF16 = jnp.bfloat16

GRID_W = 64
N_MOD = 6
RMS_EPS = 1e-6
LRU_BLOCKS = 8
CONV_W = 4
CONV_LEFT = CONV_W // 2
LRU_C = 8.0
NA_HEADS = 16
NA_ROWS = 8
NA_COLS = 16
NA_QROWS = 4
NEG_INF = -1e30
RET_HEADS = 4
ROPE_BASE = 10000.0
N_EXPERTS = 16
N_GROUPS = 4
EXPERTS_PER_GROUP = N_EXPERTS // N_GROUPS
TOP_K = 2
MOE_BLOCK = 512
LANES = 128
ROUTER_LANES = 128
SCAN_TILE = 256
VMEM_LIMIT = 56 * 1024 * 1024


def _pick(n, candidates):
    for c in candidates:
        if n % c == 0:
            return c
    raise ValueError(f"no tile in {candidates} divides {n}")


def _params(sem):
    return pltpu.CompilerParams(dimension_semantics=sem, vmem_limit_bytes=VMEM_LIMIT)


def _at(w, i):
    return lax.index_in_dim(w, i, axis=0, keepdims=False)


def _dot(a, b):
    return jnp.dot(a, b, preferred_element_type=F32)


def _dot_nt(a, b):
    return lax.dot_general(a, b, (((1,), (1,)), ((), ())), preferred_element_type=F32)


def _dot_tn(a, b):
    return lax.dot_general(a, b, (((0,), (0,)), ((), ())), preferred_element_type=F32)


def _is_ctx(pos0, tm, n_ctx):
    return (pos0 + lax.broadcasted_iota(jnp.int32, (tm, 1), 0)) < n_ctx


def _mod_row(mod_ref, is_ctx, k):
    return jnp.where(is_ctx, mod_ref[0, k:k + 1, :], mod_ref[0, N_MOD + k:N_MOD + k + 1, :])


def _norm_mod(x, g, mod_ref, is_ctx, k):
    y = x * lax.rsqrt(jnp.mean(x * x, axis=-1, keepdims=True) + RMS_EPS) * g
    return y * (1.0 + _mod_row(mod_ref, is_ctx, k + 1)) + _mod_row(mod_ref, is_ctx, k)


def _ada_kernel(cc_ref, w_ref, b_ref, o_ref):
    cc = cc_ref[...]
    s = (cc * jax.nn.sigmoid(cc)).astype(BF16)
    o_ref[0] = _dot(s, w_ref[0].astype(BF16)) + b_ref[0]


def _ada(cc, ada_w, ada_b):
    depth, d, n = ada_w.shape
    rows = cc.shape[0]
    tn = _pick(n, (1536, 1024, 512, 256, 128))
    return pl.pallas_call(
        _ada_kernel,
        grid=(depth, n // tn),
        in_specs=[pl.BlockSpec((rows, d), lambda i, j: (0, 0)),
                  pl.BlockSpec((1, d, tn), lambda i, j: (i, 0, j)),
                  pl.BlockSpec((1, 1, tn), lambda i, j: (i, 0, j))],
        out_specs=pl.BlockSpec((1, rows, tn), lambda i, j: (i, 0, j)),
        out_shape=jax.ShapeDtypeStruct((depth, rows, n), F32),
        compiler_params=_params(("parallel", "parallel")),
        name="ada_mod",
    )(cc, ada_w, ada_b.reshape(depth, 1, n))


def _ffn_sum(y0_ref, y1_ref, gw_ref, rows):
    gw = gw_ref[0, rows, :]
    return gw[:, 0:1] * y0_ref[0, rows, :].astype(F32) + gw[:, 1:2] * y1_ref[0, rows, :].astype(F32)


def _proj_kernel(*refs, tm, rs, tn, n_ctx, act_cols, pending):
    if pending:
        x_ref, y0_ref, y1_ref, gw_ref, pmod_ref, g_ref, mod_ref, w_ref, o_ref, xo_ref = refs
    else:
        x_ref, g_ref, mod_ref, w_ref, o_ref = refs
    i = pl.program_id(1)
    n = w_ref.shape[1]
    for s in range(tm // rs):
        rows = slice(s * rs, (s + 1) * rs)
        is_ctx = _is_ctx(i * tm + s * rs, rs, n_ctx)
        x = x_ref[0, rows, :]
        if pending:
            x = x + _mod_row(pmod_ref, is_ctx, 5) * _ffn_sum(y0_ref, y1_ref, gw_ref, rows)
            xo_ref[0, rows, :] = x
        h = _norm_mod(x, g_ref[...], mod_ref, is_ctx, 0).astype(BF16)
        for c in range(n // tn):
            cols = slice(c * tn, (c + 1) * tn)
            y = _dot(h, w_ref[:, cols])
            if c * tn < act_cols:
                y = jax.nn.gelu(y)
            o_ref[0, rows, cols] = y.astype(o_ref.dtype)


def _norm_proj(t, g, mod, w, n_ctx, act_cols=0, pending=None):
    b, p, d = t.shape
    n = w.shape[1]
    tm = _pick(p, (768, 512, 256)) if n <= 4 * d else _pick(p, (384, 256))
    rs = _pick(tm, (256, 128))
    tn = _pick(n, (1024, 512))
    assert act_cols % tn == 0
    kern = functools.partial(_proj_kernel, tm=tm, rs=rs, tn=tn, n_ctx=n_ctx, act_cols=act_cols,
                             pending=pending is not None)
    row_spec = pl.BlockSpec((1, tm, d), lambda bi, i: (bi, i, 0))
    mod_spec = pl.BlockSpec((1, 2 * N_MOD, d), lambda bi, i: (bi, 0, 0))
    in_specs = [row_spec]
    args = [t]
    if pending is not None:
        in_specs += [row_spec, row_spec, pl.BlockSpec((1, tm, TOP_K), lambda bi, i: (bi, i, 0)), mod_spec]
        args += [pending[0], pending[1], pending[2], pending[3]]
    in_specs += [pl.BlockSpec((1, d), lambda bi, i: (0, 0)), mod_spec,
                 pl.BlockSpec((d, n), lambda bi, i: (0, 0), pipeline_mode=pl.Buffered(1))]
    args += [g.reshape(1, d), mod, w]
    out_specs = [pl.BlockSpec((1, tm, n), lambda bi, i: (bi, i, 0))]
    out_shape = [jax.ShapeDtypeStruct((b, p, n), BF16)]
    if pending is not None:
        out_specs.append(row_spec)
        out_shape.append(jax.ShapeDtypeStruct((b, p, d), F32))
    res = pl.pallas_call(
        kern,
        grid=(b, p // tm),
        in_specs=in_specs,
        out_specs=out_specs,
        out_shape=out_shape,
        compiler_params=_params(("parallel", "parallel")),
        name="norm_proj",
    )(*args)
    return (res[0], res[1]) if pending is not None else (res[0], t)


def _out_kernel(a_ref, w_ref, x_ref, mod_ref, g_ref, rw_ref, xo_ref, h_ref, p_ref, *, tm, rs, n_ctx):
    for s in range(tm // rs):
        rows = slice(s * rs, (s + 1) * rs)
        is_ctx = _is_ctx(pl.program_id(1) * tm + s * rs, rs, n_ctx)
        y = _dot(a_ref[0, rows, :], w_ref[...])
        xn = x_ref[0, rows, :] + _mod_row(mod_ref, is_ctx, 2) * y
        xo_ref[0, rows, :] = xn
        h = _norm_mod(xn, g_ref[...], mod_ref, is_ctx, 3)
        hi = h.astype(BF16)
        h_ref[0, rows, :] = hi
        lo = (h - hi.astype(F32)).astype(BF16)
        p_ref[0, rows, :] = _dot(hi, rw_ref[...]) + _dot(lo, rw_ref[...])


def _out_proj(a, w, t, mod, g, rw, n_ctx):
    b, p, d = t.shape
    k = a.shape[-1]
    tm = _pick(p, (768, 512, 256))
    kern = functools.partial(_out_kernel, tm=tm, rs=_pick(tm, (256, 128)), n_ctx=n_ctx)
    return pl.pallas_call(
        kern,
        grid=(b, p // tm),
        in_specs=[pl.BlockSpec((1, tm, k), lambda bi, i: (bi, i, 0)),
                  pl.BlockSpec((k, d), lambda bi, i: (0, 0)),
                  pl.BlockSpec((1, tm, d), lambda bi, i: (bi, i, 0)),
                  pl.BlockSpec((1, 2 * N_MOD, d), lambda bi, i: (bi, 0, 0)),
                  pl.BlockSpec((1, d), lambda bi, i: (0, 0)),
                  pl.BlockSpec((d, ROUTER_LANES), lambda bi, i: (0, 0))],
        out_specs=[pl.BlockSpec((1, tm, d), lambda bi, i: (bi, i, 0)),
                   pl.BlockSpec((1, tm, d), lambda bi, i: (bi, i, 0)),
                   pl.BlockSpec((1, tm, ROUTER_LANES), lambda bi, i: (bi, i, 0))],
        out_shape=[jax.ShapeDtypeStruct((b, p, d), F32),
                   jax.ShapeDtypeStruct((b, p, d), BF16),
                   jax.ShapeDtypeStruct((b, p, ROUTER_LANES), F32)],
        compiler_params=_params(("parallel", "parallel")),
        name="out_proj",
    )(a, w, t, mod, g.reshape(1, d), rw)


def _lru_kernel(gt_ref, up_ref, cw_ref, cb_ref, wg_ref, gb_ref, lam_ref, o_ref,
                upad, hf_s, hb_s, af, bf, ab, bb, *, n_ctx, tt, nt):
    p = nt * tt
    tc = o_ref.shape[-1]
    pad = 8
    upad[0:pad, :] = jnp.zeros((pad, tc), F32)
    upad[pad + p:pad + p + pad, :] = jnp.zeros((pad, tc), F32)
    for t in range(nt):
        upad[pad + t * tt:pad + (t + 1) * tt, :] = up_ref[0, t * tt:(t + 1) * tt, :].astype(F32)

    neg_lam = -lam_ref[...]
    sp = jnp.maximum(neg_lam, 0.0) + jnp.log1p(jnp.exp(-jnp.abs(neg_lam)))
    rowi = lax.broadcasted_iota(jnp.int32, (tt, 1), 0)

    def conv_tile(t):
        r0 = t * tt
        first = r0 in (0, n_ctx)
        last = r0 + tt in (n_ctx, p)
        acc = jnp.broadcast_to(cb_ref[...], (tt, tc))
        for kk in range(CONV_W):
            d = kk - CONV_LEFT
            xs = upad[pad + r0 + d:pad + r0 + d + tt, :]
            if first and d < 0:
                xs = jnp.where(rowi >= -d, xs, 0.0)
            if last and d > 0:
                xs = jnp.where(rowi < tt - d, xs, 0.0)
            acc = acc + cw_ref[kk:kk + 1, :] * xs
        return acc

    def gates(u, d, a_ref, b_ref):
        ub = u.astype(BF16)
        for n in range(tc // LANES):
            sl = slice(n * LANES, (n + 1) * LANES)
            ri = _dot(ub[:, sl], wg_ref[d, n])
            r = jax.nn.sigmoid(ri[:, :LANES] + gb_ref[d, 0:1, sl])
            ig = jax.nn.sigmoid(ri[:, LANES:] + gb_ref[d, 1:2, sl])
            log_a = (-LRU_C) * r * sp[d:d + 1, sl]
            a = jnp.exp(log_a)
            one_minus_a2 = -jnp.tanh(log_a) * (a * a + 1.0)
            a_ref[:, sl] = a
            b_ref[:, sl] = jnp.sqrt(one_minus_a2) * (ig * u[:, sl])

    nz = n_ctx // tt
    fwd_order = list(range(nt))
    bwd_order = list(range(nz - 1, -1, -1)) + list(range(nt - 1, nz - 1, -1))
    carry = (jnp.zeros((1, tc), F32), jnp.zeros((1, tc), F32))
    for tf, tb in zip(fwd_order, bwd_order):
        gates(conv_tile(tf), 0, af, bf)
        gates(conv_tile(tb), 1, ab, bb)

        def step(j, c, tf=tf, tb=tb):
            hf, hb = c
            hf = af[pl.ds(j, 1), :] * hf + bf[pl.ds(j, 1), :]
            hf_s[pl.ds(tf * tt + j, 1), :] = hf
            jb = tt - 1 - j
            hb = ab[pl.ds(jb, 1), :] * hb + bb[pl.ds(jb, 1), :]
            hb_s[pl.ds(tb * tt + jb, 1), :] = hb
            return hf, hb

        carry = lax.fori_loop(0, tt, step, carry, unroll=8)

    for t in range(nt):
        rs = slice(t * tt, (t + 1) * tt)
        o_ref[0, rs, :] = ((hf_s[rs, :] + hb_s[rs, :]) * gt_ref[0, rs, :].astype(F32)).astype(o_ref.dtype)


def _lru(proj, conv_w, conv_b, gate_w, gate_b, lam, n_ctx):
    b, p, d2 = proj.shape
    d = d2 // 2
    tc = 512
    tt = SCAN_TILE
    assert p % tt == 0 and n_ctx % tt == 0 and d % tc == 0 and d // LRU_BLOCKS == LANES
    nt = p // tt
    nct = d // tc
    wg = jnp.concatenate([gate_w[:, 0], gate_w[:, 1]], axis=-1).astype(BF16)
    kern = functools.partial(_lru_kernel, n_ctx=n_ctx, tt=tt, nt=nt)
    return pl.pallas_call(
        kern,
        grid=(b, nct),
        in_specs=[pl.BlockSpec((1, p, tc), lambda bi, ci: (bi, 0, ci)),
                  pl.BlockSpec((1, p, tc), lambda bi, ci: (bi, 0, nct + ci)),
                  pl.BlockSpec((CONV_W, tc), lambda bi, ci: (0, ci)),
                  pl.BlockSpec((1, tc), lambda bi, ci: (0, ci)),
                  pl.BlockSpec((2, tc // LANES, LANES, 2 * LANES), lambda bi, ci: (0, ci, 0, 0)),
                  pl.BlockSpec((2, 2, tc), lambda bi, ci: (0, 0, ci)),
                  pl.BlockSpec((2, tc), lambda bi, ci: (0, ci))],
        out_specs=pl.BlockSpec((1, p, tc), lambda bi, ci: (bi, 0, ci)),
        out_shape=jax.ShapeDtypeStruct((b, p, d), BF16),
        scratch_shapes=[pltpu.VMEM((p + 16, tc), F32),
                        pltpu.VMEM((p, tc), F32), pltpu.VMEM((p, tc), F32),
                        pltpu.VMEM((tt, tc), F32), pltpu.VMEM((tt, tc), F32),
                        pltpu.VMEM((tt, tc), F32), pltpu.VMEM((tt, tc), F32)],
        compiler_params=_params(("parallel", "parallel")),
        name="rglru",
    )(proj, proj, conv_w, conv_b.reshape(1, d), wg, gate_b, lam)


def _na_window(rb, rows):
    win = NA_QROWS + NA_ROWS - 1
    return min(max(NA_QROWS * rb - NA_ROWS // 2, 0), rows - win)


def _na_bias(rpb, rows):
    nb = rows // NA_QROWS
    win = NA_QROWS + NA_ROWS - 1
    kr = min(NA_ROWS, rows)
    n_rel_r, n_rel_c = 2 * NA_ROWS - 1, 2 * NA_COLS - 1
    col = np.arange(GRID_W)
    c0 = np.clip(col - NA_COLS // 2, 0, GRID_W - NA_COLS)
    valid_c = (col[None, :] >= c0[:, None]) & (col[None, :] < c0[:, None] + NA_COLS)
    rel_c = np.clip(col[None, :] - col[:, None] + NA_COLS - 1, 0, n_rel_c - 1)
    pick_c = (rel_c[..., None] == np.arange(n_rel_c)).astype(np.float32)
    pick_r, valid = [], []
    for rb in (0, 1, nb - 1):
        r_abs = NA_QROWS * rb + np.arange(NA_QROWS)
        r0 = np.clip(r_abs - kr // 2, 0, rows - kr)
        k_abs = _na_window(rb, rows) + np.arange(win)
        valid_r = (k_abs[None, :] >= r0[:, None]) & (k_abs[None, :] < r0[:, None] + kr)
        rel_r = np.clip(k_abs[None, :] - r_abs[:, None] + NA_ROWS - 1, 0, n_rel_r - 1)
        pick_r.append((rel_r[..., None] == np.arange(n_rel_r)).astype(np.float32))
        valid.append(valid_r[:, None, :, None] & valid_c[None, :, None, :])
    rows_sel = jnp.einsum('hab,tqwa->htqwb', rpb, jnp.asarray(np.stack(pick_r)), precision=lax.Precision.HIGHEST)
    bias = jnp.einsum('htqwb,ckb->htqcwk', rows_sel, jnp.asarray(pick_c), precision=lax.Precision.HIGHEST)
    bias = jnp.where(jnp.asarray(np.stack(valid))[None], bias, NEG_INF)
    return bias.reshape(rpb.shape[0], 3, NA_QROWS * GRID_W, win * GRID_W)


def _na_kernel(q_ref, k_ref, v_ref, bias_ref, o_ref, km_ref, *, n_ctx, rows):
    hd = LANES // 2
    scale = hd ** -0.5
    nq = NA_QROWS * GRID_W
    nb = rows // NA_QROWS
    nk = (NA_QROWS + NA_ROWS - 1) * GRID_W
    lane = lax.broadcasted_iota(jnp.int32, (1, LANES), 1)
    first = lane < hd
    k_all = k_ref[0]
    km_ref[0] = jnp.where(first, k_all, jnp.zeros_like(k_all))
    km_ref[1] = jnp.where(first, jnp.zeros_like(k_all), k_all)

    def attend(qb, ks, bias_ty):
        outs = []
        for hh in range(2):
            s_ctx = _dot_nt(qb, km_ref[hh, 0:n_ctx, :]) * scale
            m = jnp.max(s_ctx, axis=-1, keepdims=True)
            if ks is not None:
                s_loc = _dot_nt(qb, km_ref[hh, ks:ks + nk, :]) * scale + bias_ref[hh, bias_ty]
                m = jnp.maximum(m, jnp.max(s_loc, axis=-1, keepdims=True))
                p_loc = jnp.exp(s_loc - m)
            p_ctx = jnp.exp(s_ctx - m)
            l = jnp.sum(p_ctx, axis=-1, keepdims=True)
            o = _dot(p_ctx.astype(BF16), v_ref[0, 0:n_ctx, :])
            if ks is not None:
                l = l + jnp.sum(p_loc, axis=-1, keepdims=True)
                o = o + _dot(p_loc.astype(BF16), v_ref[0, ks:ks + nk, :])
            outs.append(o * (1.0 / l))
        return jnp.where(first, outs[0], outs[1])

    o_ref[0, 0:n_ctx, :] = attend(q_ref[0, 0:n_ctx, :], None, None).astype(o_ref.dtype)
    for rb in range(nb):
        qs = n_ctx + rb * nq
        ks = n_ctx + _na_window(rb, rows) * GRID_W
        ty = 0 if rb == 0 else (2 if rb == nb - 1 else 1)
        o_ref[0, qs:qs + nq, :] = attend(q_ref[0, qs:qs + nq, :], ks, ty).astype(o_ref.dtype)


def _na(qkv, rpb, n_ctx):
    b, p, d3 = qkv.shape
    d = d3 // 3
    rows = (p - n_ctx) // GRID_W
    assert d // NA_HEADS == LANES // 2 and rows % NA_QROWS == 0 and rows // NA_QROWS >= 3
    nhp = d // LANES
    bias = _na_bias(rpb.astype(F32), rows)
    nq, nk = bias.shape[2], bias.shape[3]
    kern = functools.partial(_na_kernel, n_ctx=n_ctx, rows=rows)
    return pl.pallas_call(
        kern,
        grid=(nhp, b),
        in_specs=[pl.BlockSpec((1, p, LANES), lambda hp, bi: (bi, 0, hp)),
                  pl.BlockSpec((1, p, LANES), lambda hp, bi: (bi, 0, nhp + hp)),
                  pl.BlockSpec((1, p, LANES), lambda hp, bi: (bi, 0, 2 * nhp + hp)),
                  pl.BlockSpec((2, 3, nq, nk), lambda hp, bi: (hp, 0, 0, 0))],
        out_specs=pl.BlockSpec((1, p, LANES), lambda hp, bi: (bi, 0, hp)),
        out_shape=jax.ShapeDtypeStruct((b, p, d), BF16),
        scratch_shapes=[pltpu.VMEM((2, p, LANES), BF16)],
        compiler_params=_params(("parallel", "parallel")),
        name="nbr_attn",
    )(qkv, qkv, qkv, bias)


def _ret_tables(n_ctx, s, dk, ch):
    quarter = dk // 4
    pos = jnp.arange(s)
    inv = ROPE_BASE ** (-jnp.arange(quarter, dtype=F32) / quarter)
    ang_r = (pos // GRID_W).astype(F32)[:, None] * inv
    ang_c = (pos % GRID_W).astype(F32)[:, None] * inv
    cos = jnp.concatenate([jnp.cos(ang_r)] * 2 + [jnp.cos(ang_c)] * 2, axis=-1)
    sin = jnp.concatenate([-jnp.sin(ang_r), jnp.sin(ang_r), -jnp.sin(ang_c), jnp.sin(ang_c)], axis=-1)
    cos = jnp.concatenate([jnp.ones((n_ctx, dk), F32), cos], axis=0)
    sin = jnp.concatenate([jnp.zeros((n_ctx, dk), F32), sin], axis=0)
    log_gamma = jnp.log1p(-(2.0 ** (-5.0 - jnp.arange(RET_HEADS, dtype=F32))))[:, None, None]
    pq = jnp.arange(ch, dtype=F32)
    col = jnp.broadcast_to(pq[:, None], (ch, dk))[None]
    tabs = jnp.stack([
        jnp.exp(jnp.abs(pq[:, None] - pq[None, :])[None] * log_gamma),
        jnp.exp((col + 1.0) * log_gamma),
        jnp.exp((ch - col) * log_gamma),
        jnp.exp((ch - 1.0 - col) * log_gamma),
        jnp.exp(col * log_gamma),
    ], axis=1)
    chunk_decay = jnp.exp(ch * log_gamma[:, 0, 0])
    return cos, sin, tabs, chunk_decay


def _ret_kernel(cd_ref, q_ref, k_ref, v_ref, g_ref, cos_ref, sin_ref, tab_ref, o_ref,
                sb_ref, st_ref, *, ch, nc):
    dk = q_ref.shape[-1]
    cd = cd_ref[pl.program_id(1)]
    k_scale = dk ** -0.5

    def rope(t, c0):
        parts = []
        for hf in range(dk // LANES):
            sl = slice(hf * LANES, (hf + 1) * LANES)
            th = t[:, sl]
            parts.append(th * cos_ref[pl.ds(c0, ch), sl] + pltpu.roll(th, LANES // 2, 1) * sin_ref[pl.ds(c0, ch), sl])
        return jnp.concatenate(parts, axis=-1)

    def load_k(c0):
        return rope(k_ref[0, pl.ds(c0, ch), :].astype(F32) * k_scale, c0)

    def kv_outer(kdec, c0):
        return _dot_tn(kdec.astype(BF16), v_ref[0, pl.ds(c0, ch), :])

    nz = 1
    sb_ref[0] = jnp.zeros(sb_ref.shape[1:], BF16)
    st_ref[...] = kv_outer(load_k(0) * tab_ref[0, 4], 0)

    def bwd(j, _):
        c = nc - 1 - j
        c0 = pl.multiple_of(c * ch, ch)
        sb_ref[c] = st_ref[...].astype(BF16)
        st_ref[...] = cd * st_ref[...] + kv_outer(load_k(c0) * tab_ref[0, 4], c0)
        return 0

    lax.fori_loop(0, nc - 1 - nz, bwd, 0)
    sb_ref[nz] = st_ref[...].astype(BF16)

    st_ref[...] = jnp.zeros(st_ref.shape, F32)

    def fwd(c, _):
        c0 = pl.multiple_of(c * ch, ch)
        q = rope(q_ref[0, pl.ds(c0, ch), :].astype(F32), c0)
        k = load_k(c0)
        v = v_ref[0, pl.ds(c0, ch), :]
        inner = _dot_nt(q.astype(BF16), k.astype(BF16)) * tab_ref[0, 0]
        o = (_dot(inner.astype(BF16), v)
             + _dot((q * tab_ref[0, 1]).astype(BF16), st_ref[...].astype(BF16))
             + _dot((q * tab_ref[0, 2]).astype(BF16), sb_ref[c]))
        st_ref[...] = cd * st_ref[...] + kv_outer(k * tab_ref[0, 3], c0)
        o = o * lax.rsqrt(jnp.mean(o * o, axis=-1, keepdims=True) + RMS_EPS)
        g = g_ref[0, pl.ds(c0, ch), :].astype(F32)
        o_ref[0, pl.ds(c0, ch), :] = (o * (g * jax.nn.sigmoid(g))).astype(o_ref.dtype)
        return 0

    lax.fori_loop(0, nc, fwd, 0)


def _ret(qkvg, n_ctx):
    b, p, d6 = qkvg.shape
    d = d6 // 6
    dk = d // RET_HEADS
    dv = 2 * dk
    ch = dk
    assert dk == 2 * LANES and n_ctx == ch and p % ch == 0
    nc = p // ch
    cos, sin, tabs, chunk_decay = _ret_tables(n_ctx, p - n_ctx, dk, ch)
    kern = functools.partial(_ret_kernel, ch=ch, nc=nc)
    nh = RET_HEADS
    return pl.pallas_call(
        kern,
        grid_spec=pltpu.PrefetchScalarGridSpec(
            num_scalar_prefetch=1,
            grid=(b, nh),
            in_specs=[pl.BlockSpec((1, p, dk), lambda bi, h, cd: (bi, 0, h)),
                      pl.BlockSpec((1, p, dk), lambda bi, h, cd: (bi, 0, nh + h)),
                      pl.BlockSpec((1, p, dv), lambda bi, h, cd: (bi, 0, nh + h)),
                      pl.BlockSpec((1, p, dv), lambda bi, h, cd: (bi, 0, 2 * nh + h)),
                      pl.BlockSpec((p, dk), lambda bi, h, cd: (0, 0)),
                      pl.BlockSpec((p, dk), lambda bi, h, cd: (0, 0)),
                      pl.BlockSpec((1, 5, ch, dk), lambda bi, h, cd: (h, 0, 0, 0))],
            out_specs=pl.BlockSpec((1, p, dv), lambda bi, h, cd: (bi, 0, h)),
            scratch_shapes=[pltpu.VMEM((nc, dk, dv), BF16), pltpu.VMEM((dk, dv), F32)]),
        out_shape=jax.ShapeDtypeStruct((b, p, nh * dv), BF16),
        compiler_params=_params(("parallel", "parallel")),
        name="retention",
    )(chunk_decay, qkvg, qkvg, qkvg, qkvg, cos, sin, tabs)


def _router_kernel(lg_ref, rb_ref, tri_ref, e_ref, w_ref, rk_ref, cnt_ref, run_ref):
    @pl.when(pl.program_id(0) == 0)
    def _():
        run_ref[...] = jnp.zeros(run_ref.shape, F32)

    lg = lg_ref[...]
    ex = jnp.exp(lg - jnp.max(lg, axis=0, keepdims=True))
    probs = ex / jnp.sum(ex, axis=0, keepdims=True)
    sel = probs + rb_ref[...]
    epg = EXPERTS_PER_GROUP
    best = grp = cur = curp = None
    for g in range(N_GROUPS):
        s = [sel[g * epg + i:g * epg + i + 1, :] for i in range(epg)]
        pr = [probs[g * epg + i:g * epg + i + 1, :] for i in range(epg)]
        top2 = None
        for i in range(epg):
            for j in range(i + 1, epg):
                top2 = s[i] + s[j] if top2 is None else jnp.maximum(top2, s[i] + s[j])
        if g == 0:
            best, grp, cur, curp = top2, jnp.zeros(top2.shape, jnp.int32), s, pr
        else:
            better = top2 > best
            best = jnp.where(better, top2, best)
            grp = jnp.where(better, g, grp)
            cur = [jnp.where(better, s[i], cur[i]) for i in range(epg)]
            curp = [jnp.where(better, pr[i], curp[i]) for i in range(epg)]
    b1, i1, p1 = cur[0], jnp.zeros(best.shape, jnp.int32), curp[0]
    for i in range(1, epg):
        gt = cur[i] > b1
        b1, i1, p1 = jnp.where(gt, cur[i], b1), jnp.where(gt, i, i1), jnp.where(gt, curp[i], p1)
    b2 = i2 = p2 = None
    for i in range(epg):
        v = jnp.where(i1 == i, -jnp.inf, cur[i])
        if b2 is None:
            b2, i2, p2 = v, jnp.zeros(best.shape, jnp.int32), curp[0]
        else:
            gt = v > b2
            b2, i2, p2 = jnp.where(gt, v, b2), jnp.where(gt, i, i2), jnp.where(gt, curp[i], p2)
    e1 = grp * epg + i1
    e2 = grp * epg + i2
    inv = 1.0 / (p1 + p2)
    e_ref[0:1, :] = e1
    e_ref[1:2, :] = e2
    w_ref[0:1, :] = p1 * inv
    w_ref[1:2, :] = p2 * inv

    eidx = lax.broadcasted_iota(jnp.int32, lg.shape, 0)
    eq1 = eidx == e1
    eq2 = eidx == e2
    member = jnp.where(eq1, 1.0, jnp.where(eq2, 1.0, 0.0))
    before = _dot(member.astype(BF16), tri_ref[...]) + run_ref[...]
    rk_ref[0:1, :] = jnp.sum(jnp.where(eq1, before, 0.0), axis=0, keepdims=True).astype(jnp.int32)
    rk_ref[1:2, :] = jnp.sum(jnp.where(eq2, before, 0.0), axis=0, keepdims=True).astype(jnp.int32)
    run_ref[...] = run_ref[...] + jnp.sum(member, axis=1, keepdims=True)
    cnt_ref[...] = run_ref[...]


def _router(logits_t, router_b):
    e, n = logits_t.shape
    tt = 512
    assert n % tt == 0
    tri = (jnp.arange(tt)[:, None] < jnp.arange(tt)[None, :]).astype(BF16)
    kn = jax.ShapeDtypeStruct((TOP_K, n), jnp.int32)
    return pl.pallas_call(
        _router_kernel,
        grid=(n // tt,),
        in_specs=[pl.BlockSpec((e, tt), lambda i: (0, i)),
                  pl.BlockSpec((e, 1), lambda i: (0, 0)),
                  pl.BlockSpec((tt, tt), lambda i: (0, 0))],
        out_specs=[pl.BlockSpec((TOP_K, tt), lambda i: (0, i)),
                   pl.BlockSpec((TOP_K, tt), lambda i: (0, i)),
                   pl.BlockSpec((TOP_K, tt), lambda i: (0, i)),
                   pl.BlockSpec((e, 1), lambda i: (0, 0))],
        out_shape=[kn, jax.ShapeDtypeStruct((TOP_K, n), F32), kn, jax.ShapeDtypeStruct((e, 1), F32)],
        scratch_shapes=[pltpu.VMEM((e, 1), F32)],
        compiler_params=_params(("arbitrary",)),
        name="router",
    )(logits_t, router_b.astype(F32).reshape(e, 1), tri)


def _moe_kernel(be_ref, nu_ref, xs_ref, wg_ref, wu_ref, wd_ref, ys_ref, wgb, wub, wdb, *, fc):
    i = pl.program_id(0)
    used = i < nu_ref[0]
    new_expert = jnp.logical_or(i == 0, be_ref[i] != be_ref[jnp.maximum(i - 1, 0)])

    @pl.when(jnp.logical_and(used, new_expert))
    def _():
        wgb[...] = wg_ref[0, 0].astype(BF16)
        wub[...] = wu_ref[0, 0].astype(BF16)
        wdb[...] = wd_ref[0, 0].astype(BF16)

    @pl.when(used)
    def _():
        x = xs_ref[...]
        f = wgb.shape[-1]
        acc = jnp.zeros(ys_ref.shape, F32)
        for c in range(f // fc):
            sl = slice(c * fc, (c + 1) * fc)
            g = _dot(x, wgb[:, sl])
            u = _dot(x, wub[:, sl])
            a = (g * jax.nn.sigmoid(g) * u).astype(BF16)
            acc = acc + _dot(a, wdb[sl, :])
        ys_ref[...] = acc.astype(ys_ref.dtype)

    @pl.when(jnp.logical_not(used))
    def _():
        ys_ref[...] = jnp.zeros(ys_ref.shape, ys_ref.dtype)


def _moe_experts(xs, block_expert, n_used, w_gate, w_up, w_down, layer):
    n_rows, d = xs.shape
    f = w_gate.shape[-1]
    bm = MOE_BLOCK
    kern = functools.partial(_moe_kernel, fc=_pick(f, (512, 256, 128)))
    return pl.pallas_call(
        kern,
        grid_spec=pltpu.PrefetchScalarGridSpec(
            num_scalar_prefetch=2,
            grid=(n_rows // bm,),
            in_specs=[pl.BlockSpec((bm, d), lambda i, be, nu: (i, 0)),
                      pl.BlockSpec((1, 1, d, f), lambda i, be, nu: (layer, be[i], 0, 0)),
                      pl.BlockSpec((1, 1, d, f), lambda i, be, nu: (layer, be[i], 0, 0)),
                      pl.BlockSpec((1, 1, f, d), lambda i, be, nu: (layer, be[i], 0, 0))],
            out_specs=pl.BlockSpec((bm, d), lambda i, be, nu: (i, 0)),
            scratch_shapes=[pltpu.VMEM((d, f), BF16), pltpu.VMEM((d, f), BF16), pltpu.VMEM((f, d), BF16)]),
        out_shape=jax.ShapeDtypeStruct((n_rows, d), BF16),
        compiler_params=_params(("arbitrary",)),
        name="moe_experts",
    )(block_expert, n_used, xs, w_gate, w_up, w_down)


def _moe(h_rows, tok_ids, logits_t, router_b, w_gate, w_up, w_down, layer):
    n = tok_ids.shape[0]
    bm = MOE_BLOCK
    expert, weight, rank, counts = _router(logits_t, router_b)
    counts = counts[:, 0].astype(jnp.int32)
    padded = (counts + bm - 1) // bm * bm
    pad_end = jnp.cumsum(padded)
    pad_start = pad_end - padded
    dest = rank + jnp.sum(jnp.where(expert[..., None] == jnp.arange(N_EXPERTS), pad_start, 0), axis=-1)
    n_blocks = -(-(n * TOP_K) // bm) + N_EXPERTS
    n_rows = n_blocks * bm
    tok = tok_ids.astype(jnp.int32)
    src = (jnp.arange(n_rows, dtype=jnp.int32) % h_rows.shape[0]).at[dest.reshape(-1)].set(
        jnp.tile(tok, TOP_K), unique_indices=True)
    block_start = jnp.arange(n_blocks, dtype=jnp.int32) * bm
    block_expert = jnp.minimum(
        jnp.sum((pad_end[None, :] <= block_start[:, None]).astype(jnp.int32), axis=1), N_EXPERTS - 1)
    n_used = (pad_end[-1] // bm).astype(jnp.int32).reshape(1)
    ys = _moe_experts(h_rows[src], block_expert, n_used, w_gate, w_up, w_down, layer)
    return [ys[dest[k]] for k in range(TOP_K)], weight


def _final_kernel(x_ref, y0_ref, y1_ref, gw_ref, mod_ref, g_ref, o_ref):
    gate = mod_ref[0, 2 * N_MOD - 1:2 * N_MOD, :]
    xn = x_ref[0] + gate * _ffn_sum(y0_ref, y1_ref, gw_ref, slice(None))
    o_ref[0] = xn * lax.rsqrt(jnp.mean(xn * xn, axis=-1, keepdims=True) + RMS_EPS) * g_ref[...]


def _final(t, ys, gw, mod, g, n_ctx):
    b, p, d = t.shape
    rows = p - n_ctx
    tm = _pick(n_ctx, (512, 256))
    assert rows % tm == 0
    off = n_ctx // tm
    row_spec = pl.BlockSpec((1, tm, d), lambda bi, i: (bi, i, 0))
    return pl.pallas_call(
        _final_kernel,
        grid=(b, rows // tm),
        in_specs=[pl.BlockSpec((1, tm, d), lambda bi, i: (bi, i + off, 0)),
                  row_spec, row_spec,
                  pl.BlockSpec((1, tm, TOP_K), lambda bi, i: (bi, i, 0)),
                  pl.BlockSpec((1, 2 * N_MOD, d), lambda bi, i: (bi, 0, 0)),
                  pl.BlockSpec((1, d), lambda bi, i: (0, 0))],
        out_specs=row_spec,
        out_shape=jax.ShapeDtypeStruct((b, rows, d), F32),
        compiler_params=_params(("parallel", "parallel")),
        name="ffn_final",
    )(t, ys[0], ys[1], gw, mod, g.reshape(1, d))


def kernel(x, c, ctx, c_ctx, ada_w, ada_b, norm_mix_g, norm_ffn_g, final_norm_g, lru_w_in, lru_conv_w, lru_conv_b, lru_gate_w, lru_gate_b, lru_lambda, lru_w_out, na_w_qkv, na_rpb, na_w_o, ret_w_qkvg, ret_w_o, router_w, router_b, moe_w_gate, moe_w_up, moe_w_down):
    b, s, d = x.shape
    n_ctx = ctx.shape[1]
    p = n_ctx + s
    depth = ada_w.shape[0]
    t = jnp.concatenate([ctx, x], axis=1)

    pad_rows = -(b + 1) % 16
    cc = jnp.concatenate([c, c_ctx[None, :], jnp.zeros((pad_rows, d), F32)], axis=0)
    mods = _ada(cc, ada_w, ada_b)
    mod_x = mods[:, :b].reshape(depth, b, N_MOD, d)
    mod_z = jnp.broadcast_to(mods[:, b].reshape(depth, 1, N_MOD, d), (depth, b, N_MOD, d))
    mods = jnp.concatenate([mod_z, mod_x], axis=2)

    rw32 = router_w.astype(F32)
    rw_hi = rw32.astype(BF16)
    rw_lo = (rw32 - rw_hi.astype(F32)).astype(BF16)
    rw = jnp.concatenate([rw_hi, rw_lo, jnp.zeros((d, ROUTER_LANES - 2 * N_EXPERTS), BF16)], axis=1)

    all_rows = jnp.arange(b * p, dtype=jnp.int32)
    pending = None
    for i in range(depth):
        kind, j = i % 3, i // 3
        last = i == depth - 1
        mod = _at(mods, i)
        if kind == 0:
            proj, t = _norm_proj(t, _at(norm_mix_g, i), mod, _at(lru_w_in, j).astype(BF16), n_ctx,
                                 act_cols=d, pending=pending)
            a = _lru(proj, _at(lru_conv_w, j), _at(lru_conv_b, j), _at(lru_gate_w, j), _at(lru_gate_b, j),
                     _at(lru_lambda, j), n_ctx)
            w_o = _at(lru_w_out, j)
        elif kind == 1:
            qkv, t = _norm_proj(t, _at(norm_mix_g, i), mod, _at(na_w_qkv, j).astype(BF16), n_ctx, pending=pending)
            a = _na(qkv, _at(na_rpb, j), n_ctx)
            w_o = _at(na_w_o, j)
        else:
            qkvg, t = _norm_proj(t, _at(norm_mix_g, i), mod, _at(ret_w_qkvg, j).astype(BF16), n_ctx, pending=pending)
            a = _ret(qkvg, n_ctx)
            w_o = _at(ret_w_o, j)
        t, h2, pl_ = _out_proj(a, w_o.astype(BF16), t, mod, _at(norm_ffn_g, i), rw, n_ctx)
        logits = (pl_[..., :N_EXPERTS] + pl_[..., N_EXPERTS:2 * N_EXPERTS])
        if last:
            tok_ids = all_rows.reshape(b, p)[:, n_ctx:].reshape(-1)
            logits = logits[:, n_ctx:]
        else:
            tok_ids = all_rows
        ys, wts = _moe(h2.reshape(b * p, d), tok_ids, logits.reshape(-1, N_EXPERTS).T, router_b,
                       moe_w_gate, moe_w_up, moe_w_down, i)
        ys = [y.reshape(b, -1, d) for y in ys]
        wts = wts.T.reshape(b, -1, TOP_K)
        pending = (ys[0], ys[1], wts, mod)
    return _final(t, ys, wts, mod, final_norm_g, n_ctx)
```

```python
import functools

import jax
import jax.numpy as jnp
import numpy as np
from jax import lax
from jax.experimental import pallas as pl
from jax.experimental.pallas import tpu as pltpu

F32 = jnp.float32
BF16 = jnp.bfloat16

GRID_W = 64
N_MOD = 6
RMS_EPS = 1e-6
LRU_BLOCKS = 8
CONV_W = 4
CONV_LEFT = CONV_W // 2
LRU_C = 8.0
NA_HEADS = 16
NA_ROWS = 8
NA_COLS = 16
NA_QROWS = 4
NEG_INF = -1e30
RET_HEADS = 4
ROPE_BASE = 10000.0
N_EXPERTS = 16
N_GROUPS = 4
EXPERTS_PER_GROUP = N_EXPERTS // N_GROUPS
TOP_K = 2
MOE_BLOCK = 512
LANES = 128
ROUTER_LANES = 128
SCAN_TILE = 256
VMEM_LIMIT = 56 * 1024 * 1024


def _pick(n, candidates):
    for c in candidates:
        if n % c == 0:
            return c
    raise ValueError(f"no tile in {candidates} divides {n}")


def _params(sem):
    return pltpu.CompilerParams(dimension_semantics=sem, vmem_limit_bytes=VMEM_LIMIT)


def _at(w, i):
    return lax.index_in_dim(w, i, axis=0, keepdims=False)


def _dot(a, b):
    return jnp.dot(a, b, preferred_element_type=F32)


def _dot_nt(a, b):
    return lax.dot_general(a, b, (((1,), (1,)), ((), ())), preferred_element_type=F32)


def _dot_tn(a, b):
    return lax.dot_general(a, b, (((0,), (0,)), ((), ())), preferred_element_type=F32)


def _is_ctx(pos0, tm, n_ctx):
    return (pos0 + lax.broadcasted_iota(jnp.int32, (tm, 1), 0)) < n_ctx


def _mod_row(mod_ref, is_ctx, k):
    return jnp.where(is_ctx, mod_ref[0, k:k + 1, :], mod_ref[0, N_MOD + k:N_MOD + k + 1, :])


def _norm_mod(x, g, mod_ref, is_ctx, k):
    y = x * lax.rsqrt(jnp.mean(x * x, axis=-1, keepdims=True) + RMS_EPS) * g
    return y * (1.0 + _mod_row(mod_ref, is_ctx, k + 1)) + _mod_row(mod_ref, is_ctx, k)


def _ada_kernel(cc_ref, w_ref, b_ref, o_ref):
    cc = cc_ref[...]
    s = (cc * jax.nn.sigmoid(cc)).astype(BF16)
    o_ref[0] = _dot(s, w_ref[0].astype(BF16)) + b_ref[0]


def _ada(cc, ada_w, ada_b):
    depth, d, n = ada_w.shape
    rows = cc.shape[0]
    tn = _pick(n, (1536, 1024, 512, 256, 128))
    return pl.pallas_call(
        _ada_kernel,
        grid=(depth, n // tn),
        in_specs=[pl.BlockSpec((rows, d), lambda i, j: (0, 0)),
                  pl.BlockSpec((1, d, tn), lambda i, j: (i, 0, j)),
                  pl.BlockSpec((1, 1, tn), lambda i, j: (i, 0, j))],
        out_specs=pl.BlockSpec((1, rows, tn), lambda i, j: (i, 0, j)),
        out_shape=jax.ShapeDtypeStruct((depth, rows, n), F32),
        compiler_params=_params(("parallel", "parallel")),
        name="ada_mod",
    )(cc, ada_w, ada_b.reshape(depth, 1, n))


def _ffn_sum(y0_ref, y1_ref, gw_ref, rows):
    gw = gw_ref[0, rows, :]
    return gw[:, 0:1] * y0_ref[0, rows, :].astype(F32) + gw[:, 1:2] * y1_ref[0, rows, :].astype(F32)


def _proj_kernel(*refs, tm, rs, tn, n_ctx, act_cols, pending):
    if pending:
        x_ref, y0_ref, y1_ref, gw_ref, pmod_ref, g_ref, mod_ref, w_ref, o_ref, xo_ref = refs
    else:
        x_ref, g_ref, mod_ref, w_ref, o_ref = refs
    i = pl.program_id(1)
    n = w_ref.shape[1]
    for s in range(tm // rs):
        rows = slice(s * rs, (s + 1) * rs)
        is_ctx = _is_ctx(i * tm + s * rs, rs, n_ctx)
        x = x_ref[0, rows, :]
        if pending:
            x = x + _mod_row(pmod_ref, is_ctx, 5) * _ffn_sum(y0_ref, y1_ref, gw_ref, rows)
            xo_ref[0, rows, :] = x
        h = _norm_mod(x, g_ref[...], mod_ref, is_ctx, 0).astype(BF16)
        for c in range(n // tn):
            cols = slice(c * tn, (c + 1) * tn)
            y = _dot(h, w_ref[:, cols])
            if c * tn < act_cols:
                y = jax.nn.gelu(y)
            o_ref[0, rows, cols] = y.astype(o_ref.dtype)


def _norm_proj(t, g, mod, w, n_ctx, act_cols=0, pending=None):
    b, p, d = t.shape
    n = w.shape[1]
    tm = _pick(p, (768, 512, 256)) if n <= 4 * d else _pick(p, (384, 256))
    rs = _pick(tm, (256, 128))
    tn = _pick(n, (1024, 512))
    assert act_cols % tn == 0
    kern = functools.partial(_proj_kernel, tm=tm, rs=rs, tn=tn, n_ctx=n_ctx, act_cols=act_cols,
                             pending=pending is not None)
    row_spec = pl.BlockSpec((1, tm, d), lambda bi, i: (bi, i, 0))
    mod_spec = pl.BlockSpec((1, 2 * N_MOD, d), lambda bi, i: (bi, 0, 0))
    in_specs = [row_spec]
    args = [t]
    if pending is not None:
        in_specs += [row_spec, row_spec, pl.BlockSpec((1, tm, TOP_K), lambda bi, i: (bi, i, 0)), mod_spec]
        args += [pending[0], pending[1], pending[2], pending[3]]
    in_specs += [pl.BlockSpec((1, d), lambda bi, i: (0, 0)), mod_spec,
                 pl.BlockSpec((d, n), lambda bi, i: (0, 0), pipeline_mode=pl.Buffered(1))]
    args += [g.reshape(1, d), mod, w]
    out_specs = [pl.BlockSpec((1, tm, n), lambda bi, i: (bi, i, 0))]
    out_shape = [jax.ShapeDtypeStruct((b, p, n), BF16)]
    if pending is not None:
        out_specs.append(row_spec)
        out_shape.append(jax.ShapeDtypeStruct((b, p, d), F32))
    res = pl.pallas_call(
        kern,
        grid=(b, p // tm),
        in_specs=in_specs,
        out_specs=out_specs,
        out_shape=out_shape,
        compiler_params=_params(("parallel", "parallel")),
        name="norm_proj",
    )(*args)
    return (res[0], res[1]) if pending is not None else (res[0], t)


def _out_kernel(a_ref, w_ref, x_ref, mod_ref, g_ref, rw_ref, xo_ref, h_ref, p_ref, *, tm, rs, n_ctx):
    for s in range(tm // rs):
        rows = slice(s * rs, (s + 1) * rs)
        is_ctx = _is_ctx(pl.program_id(1) * tm + s * rs, rs, n_ctx)
        y = _dot(a_ref[0, rows, :], w_ref[...])
        xn = x_ref[0, rows, :] + _mod_row(mod_ref, is_ctx, 2) * y
        xo_ref[0, rows, :] = xn
        h = _norm_mod(xn, g_ref[...], mod_ref, is_ctx, 3)
        hi = h.astype(BF16)
        h_ref[0, rows, :] = hi
        lo = (h - hi.astype(F32)).astype(BF16)
        p_ref[0, rows, :] = _dot(hi, rw_ref[...]) + _dot(lo, rw_ref[...])


def _out_proj(a, w, t, mod, g, rw, n_ctx):
    b, p, d = t.shape
    k = a.shape[-1]
    tm = _pick(p, (768, 512, 256))
    kern = functools.partial(_out_kernel, tm=tm, rs=_pick(tm, (256, 128)), n_ctx=n_ctx)
    return pl.pallas_call(
        kern,
        grid=(b, p // tm),
        in_specs=[pl.BlockSpec((1, tm, k), lambda bi, i: (bi, i, 0)),
                  pl.BlockSpec((k, d), lambda bi, i: (0, 0)),
                  pl.BlockSpec((1, tm, d), lambda bi, i: (bi, i, 0)),
                  pl.BlockSpec((1, 2 * N_MOD, d), lambda bi, i: (bi, 0, 0)),
                  pl.BlockSpec((1, d), lambda bi, i: (0, 0)),
                  pl.BlockSpec((d, ROUTER_LANES), lambda bi, i: (0, 0))],
        out_specs=[pl.BlockSpec((1, tm, d), lambda bi, i: (bi, i, 0)),
                   pl.BlockSpec((1, tm, d), lambda bi, i: (bi, i, 0)),
                   pl.BlockSpec((1, tm, ROUTER_LANES), lambda bi, i: (bi, i, 0))],
        out_shape=[jax.ShapeDtypeStruct((b, p, d), F32),
                   jax.ShapeDtypeStruct((b, p, d), BF16),
                   jax.ShapeDtypeStruct((b, p, ROUTER_LANES), F32)],
        compiler_params=_params(("parallel", "parallel")),
        name="out_proj",
    )(a, w, t, mod, g.reshape(1, d), rw)


def _lru_kernel(gt_ref, up_ref, cw_ref, cb_ref, wg_ref, gb_ref, lam_ref, o_ref,
                upad, u_s, hf_s, hb_s, af, bf, ab, bb, *, n_ctx, tt, nt):
    p = nt * tt
    tc = o_ref.shape[-1]
    pad = 8
    upad[0:pad, :] = jnp.zeros((pad, tc), F32)
    upad[pad + p:pad + p + pad, :] = jnp.zeros((pad, tc), F32)
    for t in range(nt):
        upad[pad + t * tt:pad + (t + 1) * tt, :] = up_ref[0, t * tt:(t + 1) * tt, :].astype(F32)

    neg_lam = -lam_ref[...]
    sp = jnp.maximum(neg_lam, 0.0) + jnp.log1p(jnp.exp(-jnp.abs(neg_lam)))
    rowi = lax.broadcasted_iota(jnp.int32, (tt, 1), 0)

    def conv_tile(t):
        r0 = t * tt
        first = r0 in (0, n_ctx)
        last = r0 + tt in (n_ctx, p)
        acc = jnp.broadcast_to(cb_ref[...], (tt, tc))
        for kk in range(CONV_W):
            d = kk - CONV_LEFT
            xs = upad[pad + r0 + d:pad + r0 + d + tt, :]
            if first and d < 0:
                xs = jnp.where(rowi >= -d, xs, 0.0)
            if last and d > 0:
                xs = jnp.where(rowi < tt - d, xs, 0.0)
            acc = acc + cw_ref[kk:kk + 1, :] * xs
        return acc

    def gates(u, d, a_ref, b_ref):
        ub = u.astype(BF16)
        for n in range(tc // LANES):
            sl = slice(n * LANES, (n + 1) * LANES)
            ri = _dot(ub[:, sl], wg_ref[d, n])
            r = jax.nn.sigmoid(ri[:, :LANES] + gb_ref[d, 0:1, sl])
            ig = jax.nn.sigmoid(ri[:, LANES:] + gb_ref[d, 1:2, sl])
            log_a = (-LRU_C) * r * sp[d:d + 1, sl]
            a = jnp.exp(log_a)
            one_minus_a2 = -jnp.tanh(log_a) * (a * a + 1.0)
            a_ref[:, sl] = a
            b_ref[:, sl] = jnp.sqrt(one_minus_a2) * (ig * u[:, sl])

    nz = n_ctx // tt
    fwd_order = list(range(nt))
    bwd_order = list(range(nz - 1, -1, -1)) + list(range(nt - 1, nz - 1, -1))
    for t in range(nt):
        u_s[t * tt:(t + 1) * tt, :] = conv_tile(t)

    sub = 8
    carry = (jnp.zeros((1, tc), F32), jnp.zeros((1, tc), F32))
    for tf, tb in zip(fwd_order, bwd_order):
        gates(u_s[tf * tt:(tf + 1) * tt, :], 0, af, bf)
        gates(u_s[tb * tt:(tb + 1) * tt, :], 1, ab, bb)

        def step(jj, c, tf=tf, tb=tb):
            hf, hb = c
            f0 = pl.multiple_of(jj * sub, sub)
            b0 = pl.multiple_of(tt - sub - jj * sub, sub)
            a_f, b_f = af[pl.ds(f0, sub), :], bf[pl.ds(f0, sub), :]
            a_b, b_b = ab[pl.ds(b0, sub), :], bb[pl.ds(b0, sub), :]
            out_f, out_b = [], [None] * sub
            for r in range(sub):
                hf = a_f[r:r + 1, :] * hf + b_f[r:r + 1, :]
                out_f.append(hf)
                rb = sub - 1 - r
                hb = a_b[rb:rb + 1, :] * hb + b_b[rb:rb + 1, :]
                out_b[rb] = hb
            hf_s[pl.ds(tf * tt + f0, sub), :] = jnp.concatenate(out_f, axis=0)
            hb_s[pl.ds(tb * tt + b0, sub), :] = jnp.concatenate(out_b, axis=0)
            return hf, hb

        carry = lax.fori_loop(0, tt // sub, step, carry)

    for t in range(nt):
        rs = slice(t * tt, (t + 1) * tt)
        o_ref[0, rs, :] = ((hf_s[rs, :] + hb_s[rs, :]) * gt_ref[0, rs, :].astype(F32)).astype(o_ref.dtype)


def _lru(proj, conv_w, conv_b, gate_w, gate_b, lam, n_ctx):
    b, p, d2 = proj.shape
    d = d2 // 2
    tc = 512
    tt = SCAN_TILE
    assert p % tt == 0 and n_ctx % tt == 0 and d % tc == 0 and d // LRU_BLOCKS == LANES
    nt = p // tt
    nct = d // tc
    wg = jnp.concatenate([gate_w[:, 0], gate_w[:, 1]], axis=-1).astype(BF16)
    kern = functools.partial(_lru_kernel, n_ctx=n_ctx, tt=tt, nt=nt)
    return pl.pallas_call(
        kern,
        grid=(b, nct),
        in_specs=[pl.BlockSpec((1, p, tc), lambda bi, ci: (bi, 0, ci)),
                  pl.BlockSpec((1, p, tc), lambda bi, ci: (bi, 0, nct + ci)),
                  pl.BlockSpec((CONV_W, tc), lambda bi, ci: (0, ci)),
                  pl.BlockSpec((1, tc), lambda bi, ci: (0, ci)),
                  pl.BlockSpec((2, tc // LANES, LANES, 2 * LANES), lambda bi, ci: (0, ci, 0, 0)),
                  pl.BlockSpec((2, 2, tc), lambda bi, ci: (0, 0, ci)),
                  pl.BlockSpec((2, tc), lambda bi, ci: (0, ci))],
        out_specs=pl.BlockSpec((1, p, tc), lambda bi, ci: (bi, 0, ci)),
        out_shape=jax.ShapeDtypeStruct((b, p, d), BF16),
        scratch_shapes=[pltpu.VMEM((p + 16, tc), F32), pltpu.VMEM((p, tc), F32),
                        pltpu.VMEM((p, tc), F32), pltpu.VMEM((p, tc), F32),
                        pltpu.VMEM((tt, tc), F32), pltpu.VMEM((tt, tc), F32),
                        pltpu.VMEM((tt, tc), F32), pltpu.VMEM((tt, tc), F32)],
        compiler_params=_params(("parallel", "parallel")),
        name="rglru",
    )(proj, proj, conv_w, conv_b.reshape(1, d), wg, gate_b, lam)


def _na_window(rb, rows):
    win = NA_QROWS + NA_ROWS - 1
    return min(max(NA_QROWS * rb - NA_ROWS // 2, 0), rows - win)


def _na_bias(rpb, rows):
    nb = rows // NA_QROWS
    win = NA_QROWS + NA_ROWS - 1
    kr = min(NA_ROWS, rows)
    n_rel_r, n_rel_c = 2 * NA_ROWS - 1, 2 * NA_COLS - 1
    col = np.arange(GRID_W)
    c0 = np.clip(col - NA_COLS // 2, 0, GRID_W - NA_COLS)
    valid_c = (col[None, :] >= c0[:, None]) & (col[None, :] < c0[:, None] + NA_COLS)
    rel_c = np.clip(col[None, :] - col[:, None] + NA_COLS - 1, 0, n_rel_c - 1)
    pick_c = (rel_c[..., None] == np.arange(n_rel_c)).astype(np.float32)
    pick_r, valid = [], []
    for rb in (0, 1, nb - 1):
        r_abs = NA_QROWS * rb + np.arange(NA_QROWS)
        r0 = np.clip(r_abs - kr // 2, 0, rows - kr)
        k_abs = _na_window(rb, rows) + np.arange(win)
        valid_r = (k_abs[None, :] >= r0[:, None]) & (k_abs[None, :] < r0[:, None] + kr)
        rel_r = np.clip(k_abs[None, :] - r_abs[:, None] + NA_ROWS - 1, 0, n_rel_r - 1)
        pick_r.append((rel_r[..., None] == np.arange(n_rel_r)).astype(np.float32))
        valid.append(valid_r[:, None, :, None] & valid_c[None, :, None, :])
    rows_sel = jnp.einsum('hab,tqwa->htqwb', rpb, jnp.asarray(np.stack(pick_r)), precision=lax.Precision.HIGHEST)
    bias = jnp.einsum('htqwb,ckb->htqcwk', rows_sel, jnp.asarray(pick_c), precision=lax.Precision.HIGHEST)
    bias = jnp.where(jnp.asarray(np.stack(valid))[None], bias, NEG_INF)
    return bias.reshape(rpb.shape[0], 3, NA_QROWS * GRID_W, win * GRID_W)


def _na_kernel(q_ref, k_ref, v_ref, bias_ref, o_ref, km_ref, vm_ref, *, n_ctx, rows):
    hd = LANES // 2
    scale = hd ** -0.5
    nq = NA_QROWS * GRID_W
    nb = rows // NA_QROWS
    nk = (NA_QROWS + NA_ROWS - 1) * GRID_W
    lane = lax.broadcasted_iota(jnp.int32, (1, LANES), 1)
    first = lane < hd
    k_all = k_ref[0]
    v_all = v_ref[0]
    km_ref[0] = jnp.where(first, k_all, jnp.zeros_like(k_all))
    km_ref[1] = jnp.where(first, jnp.zeros_like(k_all), k_all)
    vm_ref[0] = jnp.where(first, v_all, jnp.ones_like(v_all))
    vm_ref[1] = jnp.where(first, jnp.ones_like(v_all), v_all)

    def attend(qb, ks, bias_ty):
        qb = qb * scale
        heads = range(2)
        s_ctx = [_dot_nt(qb, km_ref[hh, 0:n_ctx, :]) for hh in heads]
        m = [jnp.max(s, axis=-1, keepdims=True) for s in s_ctx]
        if ks is not None:
            s_loc = [_dot_nt(qb, km_ref[hh, ks:ks + nk, :]) + bias_ref[hh, bias_ty] for hh in heads]
            m = [jnp.maximum(m[hh], jnp.max(s_loc[hh], axis=-1, keepdims=True)) for hh in heads]
        o = [_dot(jnp.exp(s_ctx[hh] - m[hh]).astype(BF16), vm_ref[hh, 0:n_ctx, :]) for hh in heads]
        if ks is not None:
            o = [o[hh] + _dot(jnp.exp(s_loc[hh] - m[hh]).astype(BF16), vm_ref[hh, ks:ks + nk, :]) for hh in heads]
        outs = [o[hh] * (1.0 / pltpu.roll(o[hh], hd, 1)) for hh in heads]
        return jnp.where(first, outs[0], outs[1])

    o_ref[0, 0:n_ctx, :] = attend(q_ref[0, 0:n_ctx, :], None, None).astype(o_ref.dtype)
    for rb in range(nb):
        qs = n_ctx + rb * nq
        ks = n_ctx + _na_window(rb, rows) * GRID_W
        ty = 0 if rb == 0 else (2 if rb == nb - 1 else 1)
        o_ref[0, qs:qs + nq, :] = attend(q_ref[0, qs:qs + nq, :], ks, ty).astype(o_ref.dtype)


def _na(qkv, rpb, n_ctx):
    b, p, d3 = qkv.shape
    d = d3 // 3
    rows = (p - n_ctx) // GRID_W
    assert d // NA_HEADS == LANES // 2 and rows % NA_QROWS == 0 and rows // NA_QROWS >= 3
    nhp = d // LANES
    bias = _na_bias(rpb.astype(F32), rows)
    nq, nk = bias.shape[2], bias.shape[3]
    kern = functools.partial(_na_kernel, n_ctx=n_ctx, rows=rows)
    return pl.pallas_call(
        kern,
        grid=(nhp, b),
        in_specs=[pl.BlockSpec((1, p, LANES), lambda hp, bi: (bi, 0, hp)),
                  pl.BlockSpec((1, p, LANES), lambda hp, bi: (bi, 0, nhp + hp)),
                  pl.BlockSpec((1, p, LANES), lambda hp, bi: (bi, 0, 2 * nhp + hp)),
                  pl.BlockSpec((2, 3, nq, nk), lambda hp, bi: (hp, 0, 0, 0))],
        out_specs=pl.BlockSpec((1, p, LANES), lambda hp, bi: (bi, 0, hp)),
        out_shape=jax.ShapeDtypeStruct((b, p, d), BF16),
        scratch_shapes=[pltpu.VMEM((2, p, LANES), BF16), pltpu.VMEM((2, p, LANES), BF16)],
        compiler_params=_params(("parallel", "parallel")),
        name="nbr_attn",
    )(qkv, qkv, qkv, bias)


def _ret_tables(n_ctx, s, dk, ch):
    quarter = dk // 4
    pos = jnp.arange(s)
    inv = ROPE_BASE ** (-jnp.arange(quarter, dtype=F32) / quarter)
    ang_r = (pos // GRID_W).astype(F32)[:, None] * inv
    ang_c = (pos % GRID_W).astype(F32)[:, None] * inv
    cos = jnp.concatenate([jnp.cos(ang_r)] * 2 + [jnp.cos(ang_c)] * 2, axis=-1)
    sin = jnp.concatenate([-jnp.sin(ang_r), jnp.sin(ang_r), -jnp.sin(ang_c), jnp.sin(ang_c)], axis=-1)
    cos = jnp.concatenate([jnp.ones((n_ctx, dk), F32), cos], axis=0)
    sin = jnp.concatenate([jnp.zeros((n_ctx, dk), F32), sin], axis=0)
    log_gamma = jnp.log1p(-(2.0 ** (-5.0 - jnp.arange(RET_HEADS, dtype=F32))))[:, None, None]
    pq = jnp.arange(ch, dtype=F32)
    col = jnp.broadcast_to(pq[:, None], (ch, dk))[None]
    tabs = jnp.stack([
        jnp.exp(jnp.abs(pq[:, None] - pq[None, :])[None] * log_gamma),
        jnp.exp((col + 1.0) * log_gamma),
        jnp.exp((ch - col) * log_gamma),
        jnp.exp((ch - 1.0 - col) * log_gamma),
        jnp.exp(col * log_gamma),
    ], axis=1)
    chunk_decay = jnp.exp(ch * log_gamma[:, 0, 0])
    return cos, sin, tabs, chunk_decay


def _ret_kernel(cd_ref, q_ref, k_ref, v_ref, g_ref, cos_ref, sin_ref, tab_ref, o_ref,
                sb_ref, st_ref, *, ch, nc):
    dk = q_ref.shape[-1]
    cd = cd_ref[pl.program_id(1)]
    k_scale = dk ** -0.5

    def rope(t, c0):
        parts = []
        for hf in range(dk // LANES):
            sl = slice(hf * LANES, (hf + 1) * LANES)
            th = t[:, sl]
            parts.append(th * cos_ref[pl.ds(c0, ch), sl] + pltpu.roll(th, LANES // 2, 1) * sin_ref[pl.ds(c0, ch), sl])
        return jnp.concatenate(parts, axis=-1)

    def load_k(c0):
        return rope(k_ref[0, pl.ds(c0, ch), :].astype(F32) * k_scale, c0)

    def kv_outer(kdec, c0):
        return _dot_tn(kdec.astype(BF16), v_ref[0, pl.ds(c0, ch), :])

    nz = 1
    sb_ref[0] = jnp.zeros(sb_ref.shape[1:], BF16)
    st_ref[...] = kv_outer(load_k(0) * tab_ref[0, 4], 0)

    def bwd(j, _):
        c = nc - 1 - j
        c0 = pl.multiple_of(c * ch, ch)
        sb_ref[c] = st_ref[...].astype(BF16)
        st_ref[...] = cd * st_ref[...] + kv_outer(load_k(c0) * tab_ref[0, 4], c0)
        return 0

    lax.fori_loop(0, nc - 1 - nz, bwd, 0)
    sb_ref[nz] = st_ref[...].astype(BF16)

    st_ref[...] = jnp.zeros(st_ref.shape, F32)

    def fwd(c, _):
        c0 = pl.multiple_of(c * ch, ch)
        q = rope(q_ref[0, pl.ds(c0, ch), :].astype(F32), c0)
        k = load_k(c0)
        v = v_ref[0, pl.ds(c0, ch), :]
        inner = _dot_nt(q.astype(BF16), k.astype(BF16)) * tab_ref[0, 0]
        o = (_dot(inner.astype(BF16), v)
             + _dot((q * tab_ref[0, 1]).astype(BF16), st_ref[...].astype(BF16))
             + _dot((q * tab_ref[0, 2]).astype(BF16), sb_ref[c]))
        st_ref[...] = cd * st_ref[...] + kv_outer(k * tab_ref[0, 3], c0)
        o = o * lax.rsqrt(jnp.mean(o * o, axis=-1, keepdims=True) + RMS_EPS)
        g = g_ref[0, pl.ds(c0, ch), :].astype(F32)
        o_ref[0, pl.ds(c0, ch), :] = (o * (g * jax.nn.sigmoid(g))).astype(o_ref.dtype)
        return 0

    lax.fori_loop(0, nc, fwd, 0)


def _ret(qkvg, n_ctx):
    b, p, d6 = qkvg.shape
    d = d6 // 6
    dk = d // RET_HEADS
    dv = 2 * dk
    ch = dk
    assert dk == 2 * LANES and n_ctx == ch and p % ch == 0
    nc = p // ch
    cos, sin, tabs, chunk_decay = _ret_tables(n_ctx, p - n_ctx, dk, ch)
    kern = functools.partial(_ret_kernel, ch=ch, nc=nc)
    nh = RET_HEADS
    return pl.pallas_call(
        kern,
        grid_spec=pltpu.PrefetchScalarGridSpec(
            num_scalar_prefetch=1,
            grid=(b, nh),
            in_specs=[pl.BlockSpec((1, p, dk), lambda bi, h, cd: (bi, 0, h)),
                      pl.BlockSpec((1, p, dk), lambda bi, h, cd: (bi, 0, nh + h)),
                      pl.BlockSpec((1, p, dv), lambda bi, h, cd: (bi, 0, nh + h)),
                      pl.BlockSpec((1, p, dv), lambda bi, h, cd: (bi, 0, 2 * nh + h)),
                      pl.BlockSpec((p, dk), lambda bi, h, cd: (0, 0)),
                      pl.BlockSpec((p, dk), lambda bi, h, cd: (0, 0)),
                      pl.BlockSpec((1, 5, ch, dk), lambda bi, h, cd: (h, 0, 0, 0))],
            out_specs=pl.BlockSpec((1, p, dv), lambda bi, h, cd: (bi, 0, h)),
            scratch_shapes=[pltpu.VMEM((nc, dk, dv), BF16), pltpu.VMEM((dk, dv), F32)]),
        out_shape=jax.ShapeDtypeStruct((b, p, nh * dv), BF16),
        compiler_params=_params(("parallel", "parallel")),
        name="retention",
    )(chunk_decay, qkvg, qkvg, qkvg, qkvg, cos, sin, tabs)


def _router_kernel(lg_ref, rb_ref, tri_ref, e_ref, w_ref, rk_ref, cnt_ref, run_ref):
    @pl.when(pl.program_id(0) == 0)
    def _():
        run_ref[...] = jnp.zeros(run_ref.shape, F32)

    lg = lg_ref[...]
    ex = jnp.exp(lg - jnp.max(lg, axis=0, keepdims=True))
    probs = ex / jnp.sum(ex, axis=0, keepdims=True)
    sel = probs + rb_ref[...]
    epg = EXPERTS_PER_GROUP
    best = grp = cur = curp = None
    for g in range(N_GROUPS):
        s = [sel[g * epg + i:g * epg + i + 1, :] for i in range(epg)]
        pr = [probs[g * epg + i:g * epg + i + 1, :] for i in range(epg)]
        top2 = None
        for i in range(epg):
            for j in range(i + 1, epg):
                top2 = s[i] + s[j] if top2 is None else jnp.maximum(top2, s[i] + s[j])
        if g == 0:
            best, grp, cur, curp = top2, jnp.zeros(top2.shape, jnp.int32), s, pr
        else:
            better = top2 > best
            best = jnp.where(better, top2, best)
            grp = jnp.where(better, g, grp)
            cur = [jnp.where(better, s[i], cur[i]) for i in range(epg)]
            curp = [jnp.where(better, pr[i], curp[i]) for i in range(epg)]
    b1, i1, p1 = cur[0], jnp.zeros(best.shape, jnp.int32), curp[0]
    for i in range(1, epg):
        gt = cur[i] > b1
        b1, i1, p1 = jnp.where(gt, cur[i], b1), jnp.where(gt, i, i1), jnp.where(gt, curp[i], p1)
    b2 = i2 = p2 = None
    for i in range(epg):
        v = jnp.where(i1 == i, -jnp.inf, cur[i])
        if b2 is None:
            b2, i2, p2 = v, jnp.zeros(best.shape, jnp.int32), curp[0]
        else:
            gt = v > b2
            b2, i2, p2 = jnp.where(gt, v, b2), jnp.where(gt, i, i2), jnp.where(gt, curp[i], p2)
    e1 = grp * epg + i1
    e2 = grp * epg + i2
    inv = 1.0 / (p1 + p2)
    e_ref[0:1, :] = e1
    e_ref[1:2, :] = e2
    w_ref[0:1, :] = p1 * inv
    w_ref[1:2, :] = p2 * inv

    eidx = lax.broadcasted_iota(jnp.int32, lg.shape, 0)
    eq1 = eidx == e1
    eq2 = eidx == e2
    member = jnp.where(eq1, 1.0, jnp.where(eq2, 1.0, 0.0))
    before = _dot(member.astype(BF16), tri_ref[...]) + run_ref[...]
    rk_ref[0:1, :] = jnp.sum(jnp.where(eq1, before, 0.0), axis=0, keepdims=True).astype(jnp.int32)
    rk_ref[1:2, :] = jnp.sum(jnp.where(eq2, before, 0.0), axis=0, keepdims=True).astype(jnp.int32)
    run_ref[...] = run_ref[...] + jnp.sum(member, axis=1, keepdims=True)
    cnt_ref[...] = run_ref[...]


def _router(logits_t, router_b):
    e, n = logits_t.shape
    tt = 512
    assert n % tt == 0
    tri = (jnp.arange(tt)[:, None] < jnp.arange(tt)[None, :]).astype(BF16)
    kn = jax.ShapeDtypeStruct((TOP_K, n), jnp.int32)
    return pl.pallas_call(
        _router_kernel,
        grid=(n // tt,),
        in_specs=[pl.BlockSpec((e, tt), lambda i: (0, i)),
                  pl.BlockSpec((e, 1), lambda i: (0, 0)),
                  pl.BlockSpec((tt, tt), lambda i: (0, 0))],
        out_specs=[pl.BlockSpec((TOP_K, tt), lambda i: (0, i)),
                   pl.BlockSpec((TOP_K, tt), lambda i: (0, i)),
                   pl.BlockSpec((TOP_K, tt), lambda i: (0, i)),
                   pl.BlockSpec((e, 1), lambda i: (0, 0))],
        out_shape=[kn, jax.ShapeDtypeStruct((TOP_K, n), F32), kn, jax.ShapeDtypeStruct((e, 1), F32)],
        scratch_shapes=[pltpu.VMEM((e, 1), F32)],
        compiler_params=_params(("arbitrary",)),
        name="router",
    )(logits_t, router_b.astype(F32).reshape(e, 1), tri)


def _moe_kernel(be_ref, nu_ref, xs_ref, wg_ref, wu_ref, wd_ref, ys_ref, wgb, wub, wdb, *, fc):
    i = pl.program_id(0)
    used = i < nu_ref[0]
    new_expert = jnp.logical_or(i == 0, be_ref[i] != be_ref[jnp.maximum(i - 1, 0)])

    @pl.when(jnp.logical_and(used, new_expert))
    def _():
        wgb[...] = wg_ref[0, 0].astype(BF16)
        wub[...] = wu_ref[0, 0].astype(BF16)
        wdb[...] = wd_ref[0, 0].astype(BF16)

    @pl.when(used)
    def _():
        x = xs_ref[...]
        f = wgb.shape[-1]
        acc = jnp.zeros(ys_ref.shape, F32)
        for c in range(f // fc):
            sl = slice(c * fc, (c + 1) * fc)
            g = _dot(x, wgb[:, sl])
            u = _dot(x, wub[:, sl])
            a = (g * jax.nn.sigmoid(g) * u).astype(BF16)
            acc = acc + _dot(a, wdb[sl, :])
        ys_ref[...] = acc.astype(ys_ref.dtype)

    @pl.when(jnp.logical_not(used))
    def _():
        ys_ref[...] = jnp.zeros(ys_ref.shape, ys_ref.dtype)


def _moe_experts(xs, block_expert, n_used, w_gate, w_up, w_down, layer):
    n_rows, d = xs.shape
    f = w_gate.shape[-1]
    bm = MOE_BLOCK
    kern = functools.partial(_moe_kernel, fc=_pick(f, (512, 256, 128)))
    return pl.pallas_call(
        kern,
        grid_spec=pltpu.PrefetchScalarGridSpec(
            num_scalar_prefetch=2,
            grid=(n_rows // bm,),
            in_specs=[pl.BlockSpec((bm, d), lambda i, be, nu: (i, 0)),
                      pl.BlockSpec((1, 1, d, f), lambda i, be, nu: (layer, be[i], 0, 0)),
                      pl.BlockSpec((1, 1, d, f), lambda i, be, nu: (layer, be[i], 0, 0)),
                      pl.BlockSpec((1, 1, f, d), lambda i, be, nu: (layer, be[i], 0, 0))],
            out_specs=pl.BlockSpec((bm, d), lambda i, be, nu: (i, 0)),
            scratch_shapes=[pltpu.VMEM((d, f), BF16), pltpu.VMEM((d, f), BF16), pltpu.VMEM((f, d), BF16)]),
        out_shape=jax.ShapeDtypeStruct((n_rows, d), BF16),
        compiler_params=_params(("arbitrary",)),
        name="moe_experts",
    )(block_expert, n_used, xs, w_gate, w_up, w_down)


def _moe(h_rows, tok_ids, logits_t, router_b, w_gate, w_up, w_down, layer):
    n = tok_ids.shape[0]
    bm = MOE_BLOCK
    expert, weight, rank, counts = _router(logits_t, router_b)
    counts = counts[:, 0].astype(jnp.int32)
    padded = (counts + bm - 1) // bm * bm
    pad_end = jnp.cumsum(padded)
    pad_start = pad_end - padded
    dest = rank + jnp.sum(jnp.where(expert[..., None] == jnp.arange(N_EXPERTS), pad_start, 0), axis=-1)
    n_blocks = -(-(n * TOP_K) // bm) + N_EXPERTS
    n_rows = n_blocks * bm
    tok = tok_ids.astype(jnp.int32)
    src = (jnp.arange(n_rows, dtype=jnp.int32) % h_rows.shape[0]).at[dest.reshape(-1)].set(
        jnp.tile(tok, TOP_K), unique_indices=True)
    block_start = jnp.arange(n_blocks, dtype=jnp.int32) * bm
    block_expert = jnp.minimum(
        jnp.sum((pad_end[None, :] <= block_start[:, None]).astype(jnp.int32), axis=1), N_EXPERTS - 1)
    n_used = (pad_end[-1] // bm).astype(jnp.int32).reshape(1)
    ys = _moe_experts(h_rows[src], block_expert, n_used, w_gate, w_up, w_down, layer)
    return [ys[dest[k]] for k in range(TOP_K)], weight


def _final_kernel(x_ref, y0_ref, y1_ref, gw_ref, mod_ref, g_ref, o_ref):
    gate = mod_ref[0, 2 * N_MOD - 1:2 * N_MOD, :]
    xn = x_ref[0] + gate * _ffn_sum(y0_ref, y1_ref, gw_ref, slice(None))
    o_ref[0] = xn * lax.rsqrt(jnp.mean(xn * xn, axis=-1, keepdims=True) + RMS_EPS) * g_ref[...]


def _final(t, ys, gw, mod, g, n_ctx):
    b, p, d = t.shape
    rows = p - n_ctx
    tm = _pick(n_ctx, (512, 256))
    assert rows % tm == 0
    off = n_ctx // tm
    row_spec = pl.BlockSpec((1, tm, d), lambda bi, i: (bi, i, 0))
    return pl.pallas_call(
        _final_kernel,
        grid=(b, rows // tm),
        in_specs=[pl.BlockSpec((1, tm, d), lambda bi, i: (bi, i + off, 0)),
                  row_spec, row_spec,
                  pl.BlockSpec((1, tm, TOP_K), lambda bi, i: (bi, i, 0)),
                  pl.BlockSpec((1, 2 * N_MOD, d), lambda bi, i: (bi, 0, 0)),
                  pl.BlockSpec((1, d), lambda bi, i: (0, 0))],
        out_specs=row_spec,
        out_shape=jax.ShapeDtypeStruct((b, rows, d), F32),
        compiler_params=_params(("parallel", "parallel")),
        name="ffn_final",
    )(t, ys[0], ys[1], gw, mod, g.reshape(1, d))


def kernel(x, c, ctx, c_ctx, ada_w, ada_b, norm_mix_g, norm_ffn_g, final_norm_g, lru_w_in, lru_conv_w, lru_conv_b, lru_gate_w, lru_gate_b, lru_lambda, lru_w_out, na_w_qkv, na_rpb, na_w_o, ret_w_qkvg, ret_w_o, router_w, router_b, moe_w_gate, moe_w_up, moe_w_down):
    b, s, d = x.shape
    n_ctx = ctx.shape[1]
    p = n_ctx + s
    depth = ada_w.shape[0]
    t = jnp.concatenate([ctx, x], axis=1)

    pad_rows = -(b + 1) % 16
    cc = jnp.concatenate([c, c_ctx[None, :], jnp.zeros((pad_rows, d), F32)], axis=0)
    mods = _ada(cc, ada_w, ada_b)
    mod_x = mods[:, :b].reshape(depth, b, N_MOD, d)
    mod_z = jnp.broadcast_to(mods[:, b].reshape(depth, 1, N_MOD, d), (depth, b, N_MOD, d))
    mods = jnp.concatenate([mod_z, mod_x], axis=2)

    rw32 = router_w.astype(F32)
    rw_hi = rw32.astype(BF16)
    rw_lo = (rw32 - rw_hi.astype(F32)).astype(BF16)
    rw = jnp.concatenate([rw_hi, rw_lo, jnp.zeros((d, ROUTER_LANES - 2 * N_EXPERTS), BF16)], axis=1)

    all_rows = jnp.arange(b * p, dtype=jnp.int32)
    pending = None
    for i in range(depth):
        kind, j = i % 3, i // 3
        last = i == depth - 1
        mod = _at(mods, i)
        if kind == 0:
            proj, t = _norm_proj(t, _at(norm_mix_g, i), mod, _at(lru_w_in, j).astype(BF16), n_ctx,
                                 act_cols=d, pending=pending)
            a = _lru(proj, _at(lru_conv_w, j), _at(lru_conv_b, j), _at(lru_gate_w, j), _at(lru_gate_b, j),
                     _at(lru_lambda, j), n_ctx)
            w_o = _at(lru_w_out, j)
        elif kind == 1:
            qkv, t = _norm_proj(t, _at(norm_mix_g, i), mod, _at(na_w_qkv, j).astype(BF16), n_ctx, pending=pending)
            a = _na(qkv, _at(na_rpb, j), n_ctx)
            w_o = _at(na_w_o, j)
        else:
            qkvg, t = _norm_proj(t, _at(norm_mix_g, i), mod, _at(ret_w_qkvg, j).astype(BF16), n_ctx, pending=pending)
            a = _ret(qkvg, n_ctx)
            w_o = _at(ret_w_o, j)
        t, h2, pl_ = _out_proj(a, w_o.astype(BF16), t, mod, _at(norm_ffn_g, i), rw, n_ctx)
        logits = (pl_[..., :N_EXPERTS] + pl_[..., N_EXPERTS:2 * N_EXPERTS])
        if last:
            tok_ids = all_rows.reshape(b, p)[:, n_ctx:].reshape(-1)
            logits = logits[:, n_ctx:]
        else:
            tok_ids = all_rows
        ys, wts = _moe(h2.reshape(b * p, d), tok_ids, logits.reshape(-1, N_EXPERTS).T, router_b,
                       moe_w_gate, moe_w_up, moe_w_down, i)
        ys = [y.reshape(b, -1, d) for y in ys]
        wts = wts.T.reshape(b, -1, TOP_K)
        pending = (ys[0], ys[1], wts, mod)
    return _final(t, ys, wts, mod, final_norm_g, n_ctx)
```

```python
import functools

import jax
import jax.numpy as jnp
import numpy as np
from jax import lax
from jax.experimental import pallas as pl
from jax.experimental.pallas import tpu as pltpu

F32 = jnp.float32
BF16 = jnp.bfloat16

GRID_W = 64
N_MOD = 6
RMS_EPS = 1e-6
LRU_BLOCKS = 8
CONV_W = 4
CONV_LEFT = CONV_W // 2
LRU_C = 8.0
NA_HEADS = 16
NA_ROWS = 8
NA_COLS = 16
NA_QROWS = 4
NEG_INF = -1e30
RET_HEADS = 4
ROPE_BASE = 10000.0
N_EXPERTS = 16
N_GROUPS = 4
EXPERTS_PER_GROUP = N_EXPERTS // N_GROUPS
TOP_K = 2
MOE_BLOCK = 512
LANES = 128
ROUTER_LANES = 128
SCAN_TILE = 256
VMEM_LIMIT = 56 * 1024 * 1024


def _pick(n, candidates):
    for c in candidates:
        if n % c == 0:
            return c
    raise ValueError(f"no tile in {candidates} divides {n}")


def _params(sem):
    return pltpu.CompilerParams(dimension_semantics=sem, vmem_limit_bytes=VMEM_LIMIT)


def _at(w, i):
    return lax.index_in_dim(w, i, axis=0, keepdims=False)


def _dot(a, b):
    return jnp.dot(a, b, preferred_element_type=F32)


def _dot_nt(a, b):
    return lax.dot_general(a, b, (((1,), (1,)), ((), ())), preferred_element_type=F32)


def _dot_tn(a, b):
    return lax.dot_general(a, b, (((0,), (0,)), ((), ())), preferred_element_type=F32)


def _is_ctx(pos0, tm, n_ctx):
    return (pos0 + lax.broadcasted_iota(jnp.int32, (tm, 1), 0)) < n_ctx


def _mod_row(mod_ref, is_ctx, k):
    return jnp.where(is_ctx, mod_ref[0, k:k + 1, :], mod_ref[0, N_MOD + k:N_MOD + k + 1, :])


def _norm_mod(x, g, mod_ref, is_ctx, k):
    y = x * lax.rsqrt(jnp.mean(x * x, axis=-1, keepdims=True) + RMS_EPS) * g
    return y * (1.0 + _mod_row(mod_ref, is_ctx, k + 1)) + _mod_row(mod_ref, is_ctx, k)


def _ada_kernel(cc_ref, w_ref, b_ref, o_ref):
    cc = cc_ref[...]
    s = (cc * jax.nn.sigmoid(cc)).astype(BF16)
    o_ref[0] = _dot(s, w_ref[0].astype(BF16)) + b_ref[0]


def _ada(cc, ada_w, ada_b):
    depth, d, n = ada_w.shape
    rows = cc.shape[0]
    tn = _pick(n, (1536, 1024, 512, 256, 128))
    return pl.pallas_call(
        _ada_kernel,
        grid=(depth, n // tn),
        in_specs=[pl.BlockSpec((rows, d), lambda i, j: (0, 0)),
                  pl.BlockSpec((1, d, tn), lambda i, j: (i, 0, j)),
                  pl.BlockSpec((1, 1, tn), lambda i, j: (i, 0, j))],
        out_specs=pl.BlockSpec((1, rows, tn), lambda i, j: (i, 0, j)),
        out_shape=jax.ShapeDtypeStruct((depth, rows, n), F32),
        compiler_params=_params(("parallel", "parallel")),
        name="ada_mod",
    )(cc, ada_w, ada_b.reshape(depth, 1, n))


def _ffn_sum(y0_ref, y1_ref, gw_ref, rows):
    gw = gw_ref[0, rows, :]
    return gw[:, 0:1] * y0_ref[0, rows, :].astype(F32) + gw[:, 1:2] * y1_ref[0, rows, :].astype(F32)


def _proj_kernel(*refs, tm, rs, tn, n_ctx, act_cols, pending):
    if pending:
        x_ref, y0_ref, y1_ref, gw_ref, pmod_ref, g_ref, mod_ref, w_ref, o_ref, xo_ref = refs
    else:
        x_ref, g_ref, mod_ref, w_ref, o_ref = refs
    i = pl.program_id(1)
    n = w_ref.shape[1]
    for s in range(tm // rs):
        rows = slice(s * rs, (s + 1) * rs)
        is_ctx = _is_ctx(i * tm + s * rs, rs, n_ctx)
        x = x_ref[0, rows, :]
        if pending:
            x = x + _mod_row(pmod_ref, is_ctx, 5) * _ffn_sum(y0_ref, y1_ref, gw_ref, rows)
            xo_ref[0, rows, :] = x
        h = _norm_mod(x, g_ref[...], mod_ref, is_ctx, 0).astype(BF16)
        for c in range(n // tn):
            cols = slice(c * tn, (c + 1) * tn)
            y = _dot(h, w_ref[:, cols])
            if c * tn < act_cols:
                y = jax.nn.gelu(y)
            o_ref[0, rows, cols] = y.astype(o_ref.dtype)


def _norm_proj(t, g, mod, w, n_ctx, act_cols=0, pending=None):
    b, p, d = t.shape
    n = w.shape[1]
    tm = _pick(p, (768, 512, 256)) if n <= 4 * d else _pick(p, (384, 256))
    rs = _pick(tm, (256, 128))
    tn = _pick(n, (1024, 512))
    assert act_cols % tn == 0
    kern = functools.partial(_proj_kernel, tm=tm, rs=rs, tn=tn, n_ctx=n_ctx, act_cols=act_cols,
                             pending=pending is not None)
    row_spec = pl.BlockSpec((1, tm, d), lambda bi, i: (bi, i, 0))
    mod_spec = pl.BlockSpec((1, 2 * N_MOD, d), lambda bi, i: (bi, 0, 0))
    in_specs = [row_spec]
    args = [t]
    if pending is not None:
        in_specs += [row_spec, row_spec, pl.BlockSpec((1, tm, TOP_K), lambda bi, i: (bi, i, 0)), mod_spec]
        args += [pending[0], pending[1], pending[2], pending[3]]
    in_specs += [pl.BlockSpec((1, d), lambda bi, i: (0, 0)), mod_spec,
                 pl.BlockSpec((d, n), lambda bi, i: (0, 0), pipeline_mode=pl.Buffered(1))]
    args += [g.reshape(1, d), mod, w]
    out_specs = [pl.BlockSpec((1, tm, n), lambda bi, i: (bi, i, 0))]
    out_shape = [jax.ShapeDtypeStruct((b, p, n), BF16)]
    if pending is not None:
        out_specs.append(row_spec)
        out_shape.append(jax.ShapeDtypeStruct((b, p, d), F32))
    res = pl.pallas_call(
        kern,
        grid=(b, p // tm),
        in_specs=in_specs,
        out_specs=out_specs,
        out_shape=out_shape,
        compiler_params=_params(("parallel", "parallel")),
        name="norm_proj",
    )(*args)
    return (res[0], res[1]) if pending is not None else (res[0], t)


def _out_kernel(a_ref, w_ref, x_ref, mod_ref, g_ref, rw_ref, xo_ref, h_ref, p_ref, *, tm, rs, n_ctx):
    for s in range(tm // rs):
        rows = slice(s * rs, (s + 1) * rs)
        is_ctx = _is_ctx(pl.program_id(1) * tm + s * rs, rs, n_ctx)
        y = _dot(a_ref[0, rows, :], w_ref[...])
        xn = x_ref[0, rows, :] + _mod_row(mod_ref, is_ctx, 2) * y
        xo_ref[0, rows, :] = xn
        h = _norm_mod(xn, g_ref[...], mod_ref, is_ctx, 3)
        hi = h.astype(BF16)
        h_ref[0, rows, :] = hi
        lo = (h - hi.astype(F32)).astype(BF16)
        p_ref[0, rows, :] = _dot(hi, rw_ref[...]) + _dot(lo, rw_ref[...])


def _out_proj(a, w, t, mod, g, rw, n_ctx):
    b, p, d = t.shape
    k = a.shape[-1]
    tm = _pick(p, (768, 512, 256))
    kern = functools.partial(_out_kernel, tm=tm, rs=_pick(tm, (256, 128)), n_ctx=n_ctx)
    return pl.pallas_call(
        kern,
        grid=(b, p // tm),
        in_specs=[pl.BlockSpec((1, tm, k), lambda bi, i: (bi, i, 0)),
                  pl.BlockSpec((k, d), lambda bi, i: (0, 0)),
                  pl.BlockSpec((1, tm, d), lambda bi, i: (bi, i, 0)),
                  pl.BlockSpec((1, 2 * N_MOD, d), lambda bi, i: (bi, 0, 0)),
                  pl.BlockSpec((1, d), lambda bi, i: (0, 0)),
                  pl.BlockSpec((d, ROUTER_LANES), lambda bi, i: (0, 0))],
        out_specs=[pl.BlockSpec((1, tm, d), lambda bi, i: (bi, i, 0)),
                   pl.BlockSpec((1, tm, d), lambda bi, i: (bi, i, 0)),
                   pl.BlockSpec((1, tm, ROUTER_LANES), lambda bi, i: (bi, i, 0))],
        out_shape=[jax.ShapeDtypeStruct((b, p, d), F32),
                   jax.ShapeDtypeStruct((b, p, d), BF16),
                   jax.ShapeDtypeStruct((b, p, ROUTER_LANES), F32)],
        compiler_params=_params(("parallel", "parallel")),
        name="out_proj",
    )(a, w, t, mod, g.reshape(1, d), rw)


def _lru_kernel(gt_ref, up_ref, cw_ref, cb_ref, wg_ref, gb_ref, lam_ref, o_ref,
                upad, u_s, hf_s, hb_s, af, bf, ab, bb, *, n_ctx, tt, nt):
    p = nt * tt
    tc = o_ref.shape[-1]
    pad = 8
    upad[0:pad, :] = jnp.zeros((pad, tc), F32)
    upad[pad + p:pad + p + pad, :] = jnp.zeros((pad, tc), F32)
    for t in range(nt):
        upad[pad + t * tt:pad + (t + 1) * tt, :] = up_ref[0, t * tt:(t + 1) * tt, :].astype(F32)

    neg_lam = -lam_ref[...]
    sp = jnp.maximum(neg_lam, 0.0) + jnp.log1p(jnp.exp(-jnp.abs(neg_lam)))
    rowi = lax.broadcasted_iota(jnp.int32, (tt, 1), 0)

    def conv_tile(t):
        r0 = t * tt
        first = r0 in (0, n_ctx)
        last = r0 + tt in (n_ctx, p)
        acc = jnp.broadcast_to(cb_ref[...], (tt, tc))
        for kk in range(CONV_W):
            d = kk - CONV_LEFT
            xs = upad[pad + r0 + d:pad + r0 + d + tt, :]
            if first and d < 0:
                xs = jnp.where(rowi >= -d, xs, 0.0)
            if last and d > 0:
                xs = jnp.where(rowi < tt - d, xs, 0.0)
            acc = acc + cw_ref[kk:kk + 1, :] * xs
        return acc

    def gates(u, d, a_ref, b_ref):
        ub = u.astype(BF16)
        for n in range(tc // LANES):
            sl = slice(n * LANES, (n + 1) * LANES)
            ri = _dot(ub[:, sl], wg_ref[d, n])
            r = jax.nn.sigmoid(ri[:, :LANES] + gb_ref[d, 0:1, sl])
            ig = jax.nn.sigmoid(ri[:, LANES:] + gb_ref[d, 1:2, sl])
            log_a = (-LRU_C) * r * sp[d:d + 1, sl]
            a = jnp.exp(log_a)
            one_minus_a2 = -jnp.tanh(log_a) * (a * a + 1.0)
            a_ref[:, sl] = a
            b_ref[:, sl] = jnp.sqrt(one_minus_a2) * (ig * u[:, sl])

    nz = n_ctx // tt
    fwd_order = list(range(nt))
    bwd_order = list(range(nz - 1, -1, -1)) + list(range(nt - 1, nz - 1, -1))
    for t in range(nt):
        u_s[t * tt:(t + 1) * tt, :] = conv_tile(t)

    sub = 8
    carry = (jnp.zeros((1, tc), F32), jnp.zeros((1, tc), F32))
    for tf, tb in zip(fwd_order, bwd_order):
        gates(u_s[tf * tt:(tf + 1) * tt, :], 0, af, bf)
        gates(u_s[tb * tt:(tb + 1) * tt, :], 1, ab, bb)

        def step(jj, c, tf=tf, tb=tb):
            hf, hb = c
            f0 = pl.multiple_of(jj * sub, sub)
            b0 = pl.multiple_of(tt - sub - jj * sub, sub)
            a_f, b_f = af[pl.ds(f0, sub), :], bf[pl.ds(f0, sub), :]
            a_b, b_b = ab[pl.ds(b0, sub), :], bb[pl.ds(b0, sub), :]
            out_f, out_b = [], [None] * sub
            for r in range(sub):
                hf = a_f[r:r + 1, :] * hf + b_f[r:r + 1, :]
                out_f.append(hf)
                rb = sub - 1 - r
                hb = a_b[rb:rb + 1, :] * hb + b_b[rb:rb + 1, :]
                out_b[rb] = hb
            hf_s[pl.ds(tf * tt + f0, sub), :] = jnp.concatenate(out_f, axis=0)
            hb_s[pl.ds(tb * tt + b0, sub), :] = jnp.concatenate(out_b, axis=0)
            return hf, hb

        carry = lax.fori_loop(0, tt // sub, step, carry)

    for t in range(nt):
        rs = slice(t * tt, (t + 1) * tt)
        o_ref[0, rs, :] = ((hf_s[rs, :] + hb_s[rs, :]) * gt_ref[0, rs, :].astype(F32)).astype(o_ref.dtype)


def _lru(proj, conv_w, conv_b, gate_w, gate_b, lam, n_ctx):
    b, p, d2 = proj.shape
    d = d2 // 2
    tc = 512
    tt = SCAN_TILE
    assert p % tt == 0 and n_ctx % tt == 0 and d % tc == 0 and d // LRU_BLOCKS == LANES
    nt = p // tt
    nct = d // tc
    wg = jnp.concatenate([gate_w[:, 0], gate_w[:, 1]], axis=-1).astype(BF16)
    kern = functools.partial(_lru_kernel, n_ctx=n_ctx, tt=tt, nt=nt)
    return pl.pallas_call(
        kern,
        grid=(b, nct),
        in_specs=[pl.BlockSpec((1, p, tc), lambda bi, ci: (bi, 0, ci)),
                  pl.BlockSpec((1, p, tc), lambda bi, ci: (bi, 0, nct + ci)),
                  pl.BlockSpec((CONV_W, tc), lambda bi, ci: (0, ci)),
                  pl.BlockSpec((1, tc), lambda bi, ci: (0, ci)),
                  pl.BlockSpec((2, tc // LANES, LANES, 2 * LANES), lambda bi, ci: (0, ci, 0, 0)),
                  pl.BlockSpec((2, 2, tc), lambda bi, ci: (0, 0, ci)),
                  pl.BlockSpec((2, tc), lambda bi, ci: (0, ci))],
        out_specs=pl.BlockSpec((1, p, tc), lambda bi, ci: (bi, 0, ci)),
        out_shape=jax.ShapeDtypeStruct((b, p, d), BF16),
        scratch_shapes=[pltpu.VMEM((p + 16, tc), F32), pltpu.VMEM((p, tc), F32),
                        pltpu.VMEM((p, tc), F32), pltpu.VMEM((p, tc), F32),
                        pltpu.VMEM((tt, tc), F32), pltpu.VMEM((tt, tc), F32),
                        pltpu.VMEM((tt, tc), F32), pltpu.VMEM((tt, tc), F32)],
        compiler_params=_params(("parallel", "parallel")),
        name="rglru",
    )(proj, proj, conv_w, conv_b.reshape(1, d), wg, gate_b, lam)


def _na_window(rb, rows):
    win = NA_QROWS + NA_ROWS - 1
    return min(max(NA_QROWS * rb - NA_ROWS // 2, 0), rows - win)


def _na_bias(rpb, rows):
    nb = rows // NA_QROWS
    win = NA_QROWS + NA_ROWS - 1
    kr = min(NA_ROWS, rows)
    n_rel_r, n_rel_c = 2 * NA_ROWS - 1, 2 * NA_COLS - 1
    col = np.arange(GRID_W)
    c0 = np.clip(col - NA_COLS // 2, 0, GRID_W - NA_COLS)
    valid_c = (col[None, :] >= c0[:, None]) & (col[None, :] < c0[:, None] + NA_COLS)
    rel_c = np.clip(col[None, :] - col[:, None] + NA_COLS - 1, 0, n_rel_c - 1)
    pick_c = (rel_c[..., None] == np.arange(n_rel_c)).astype(np.float32)
    pick_r, valid = [], []
    for rb in (0, 1, nb - 1):
        r_abs = NA_QROWS * rb + np.arange(NA_QROWS)
        r0 = np.clip(r_abs - kr // 2, 0, rows - kr)
        k_abs = _na_window(rb, rows) + np.arange(win)
        valid_r = (k_abs[None, :] >= r0[:, None]) & (k_abs[None, :] < r0[:, None] + kr)
        rel_r = np.clip(k_abs[None, :] - r_abs[:, None] + NA_ROWS - 1, 0, n_rel_r - 1)
        pick_r.append((rel_r[..., None] == np.arange(n_rel_r)).astype(np.float32))
        valid.append(valid_r[:, None, :, None] & valid_c[None, :, None, :])
    rows_sel = jnp.einsum('hab,tqwa->htqwb', rpb, jnp.asarray(np.stack(pick_r)), precision=lax.Precision.HIGHEST)
    bias = jnp.einsum('htqwb,ckb->htqcwk', rows_sel, jnp.asarray(pick_c), precision=lax.Precision.HIGHEST)
    bias = jnp.where(jnp.asarray(np.stack(valid))[None], bias, NEG_INF)
    return bias.reshape(rpb.shape[0], 3, NA_QROWS * GRID_W, win * GRID_W)


def _na_kernel(q_ref, k_ref, v_ref, bias_ref, o_ref, vx_ref, *, n_ctx, rows):
    hd = LANES // 2
    scale = hd ** -0.5
    nq = NA_QROWS * GRID_W
    nb = rows // NA_QROWS
    nk = (NA_QROWS + NA_ROWS - 1) * GRID_W
    first = lax.broadcasted_iota(jnp.int32, (1, LANES), 1) < hd
    vx_ref[:, 0:LANES] = v_ref[0]
    vx_ref[:, LANES:2 * LANES] = jnp.ones((vx_ref.shape[0], LANES), BF16)

    def attend(qb, ks, bias_ty):
        n = qb.shape[0]
        qb = qb * scale
        zero = jnp.zeros_like(qb)
        q2 = jnp.concatenate([jnp.where(first, qb, zero), jnp.where(first, zero, qb)], axis=0)
        s_ctx = _dot_nt(q2, k_ref[0, 0:n_ctx, :])
        m = jnp.max(s_ctx, axis=-1, keepdims=True)
        if ks is not None:
            s_loc = _dot_nt(q2, k_ref[0, ks:ks + nk, :]) + bias_ref[0, bias_ty]
            m = jnp.maximum(m, jnp.max(s_loc, axis=-1, keepdims=True))
        o = _dot(jnp.exp(s_ctx - m).astype(BF16), vx_ref[0:n_ctx, :])
        if ks is not None:
            o = o + _dot(jnp.exp(s_loc - m).astype(BF16), vx_ref[ks:ks + nk, :])
        o = o[:, 0:LANES] * (1.0 / o[:, LANES:2 * LANES])
        return jnp.where(first, o[0:n], o[n:2 * n])

    o_ref[0, 0:n_ctx, :] = attend(q_ref[0, 0:n_ctx, :], None, None).astype(o_ref.dtype)
    for rb in range(nb):
        qs = n_ctx + rb * nq
        ks = n_ctx + _na_window(rb, rows) * GRID_W
        ty = 0 if rb == 0 else (2 if rb == nb - 1 else 1)
        o_ref[0, qs:qs + nq, :] = attend(q_ref[0, qs:qs + nq, :], ks, ty).astype(o_ref.dtype)


def _na(qkv, rpb, n_ctx):
    b, p, d3 = qkv.shape
    d = d3 // 3
    rows = (p - n_ctx) // GRID_W
    assert d // NA_HEADS == LANES // 2 and rows % NA_QROWS == 0 and rows // NA_QROWS >= 3
    nhp = d // LANES
    bias = _na_bias(rpb.astype(F32), rows)
    nq, nk = bias.shape[2], bias.shape[3]
    bias = bias.reshape(nhp, 2, 3, nq, nk).transpose(0, 2, 1, 3, 4).reshape(nhp, 3, 2 * nq, nk)
    kern = functools.partial(_na_kernel, n_ctx=n_ctx, rows=rows)
    return pl.pallas_call(
        kern,
        grid=(nhp, b),
        in_specs=[pl.BlockSpec((1, p, LANES), lambda hp, bi: (bi, 0, hp)),
                  pl.BlockSpec((1, p, LANES), lambda hp, bi: (bi, 0, nhp + hp)),
                  pl.BlockSpec((1, p, LANES), lambda hp, bi: (bi, 0, 2 * nhp + hp)),
                  pl.BlockSpec((1, 3, 2 * nq, nk), lambda hp, bi: (hp, 0, 0, 0))],
        out_specs=pl.BlockSpec((1, p, LANES), lambda hp, bi: (bi, 0, hp)),
        out_shape=jax.ShapeDtypeStruct((b, p, d), BF16),
        scratch_shapes=[pltpu.VMEM((p, 2 * LANES), BF16)],
        compiler_params=_params(("parallel", "parallel")),
        name="nbr_attn",
    )(qkv, qkv, qkv, bias)


def _ret_tables(n_ctx, s, dk, ch):
    quarter = dk // 4
    pos = jnp.arange(s)
    inv = ROPE_BASE ** (-jnp.arange(quarter, dtype=F32) / quarter)
    ang_r = (pos // GRID_W).astype(F32)[:, None] * inv
    ang_c = (pos % GRID_W).astype(F32)[:, None] * inv
    cos = jnp.concatenate([jnp.cos(ang_r)] * 2 + [jnp.cos(ang_c)] * 2, axis=-1)
    sin = jnp.concatenate([-jnp.sin(ang_r), jnp.sin(ang_r), -jnp.sin(ang_c), jnp.sin(ang_c)], axis=-1)
    cos = jnp.concatenate([jnp.ones((n_ctx, dk), F32), cos], axis=0)
    sin = jnp.concatenate([jnp.zeros((n_ctx, dk), F32), sin], axis=0)
    log_gamma = jnp.log1p(-(2.0 ** (-5.0 - jnp.arange(RET_HEADS, dtype=F32))))[:, None, None]
    pq = jnp.arange(ch, dtype=F32)
    col = jnp.broadcast_to(pq[:, None], (ch, dk))[None]
    tabs = jnp.stack([
        jnp.exp(jnp.abs(pq[:, None] - pq[None, :])[None] * log_gamma),
        jnp.exp((col + 1.0) * log_gamma),
        jnp.exp((ch - col) * log_gamma),
        jnp.exp((ch - 1.0 - col) * log_gamma),
        jnp.exp(col * log_gamma),
    ], axis=1)
    chunk_decay = jnp.exp(ch * log_gamma[:, 0, 0])
    return cos, sin, tabs, chunk_decay


def _ret_kernel(cd_ref, q_ref, k_ref, v_ref, g_ref, cos_ref, sin_ref, tab_ref, o_ref,
                sb_ref, st_ref, *, ch, nc):
    dk = q_ref.shape[-1]
    cd = cd_ref[pl.program_id(1)]
    k_scale = dk ** -0.5

    def rope(t, c0):
        parts = []
        for hf in range(dk // LANES):
            sl = slice(hf * LANES, (hf + 1) * LANES)
            th = t[:, sl]
            parts.append(th * cos_ref[pl.ds(c0, ch), sl] + pltpu.roll(th, LANES // 2, 1) * sin_ref[pl.ds(c0, ch), sl])
        return jnp.concatenate(parts, axis=-1)

    def load_k(c0):
        return rope(k_ref[0, pl.ds(c0, ch), :].astype(F32) * k_scale, c0)

    def kv_outer(kdec, c0):
        return _dot_tn(kdec.astype(BF16), v_ref[0, pl.ds(c0, ch), :])

    nz = 1
    sb_ref[0] = jnp.zeros(sb_ref.shape[1:], BF16)
    st_ref[...] = kv_outer(load_k(0) * tab_ref[0, 4], 0)

    def bwd(j, _):
        c = nc - 1 - j
        c0 = pl.multiple_of(c * ch, ch)
        sb_ref[c] = st_ref[...].astype(BF16)
        st_ref[...] = cd * st_ref[...] + kv_outer(load_k(c0) * tab_ref[0, 4], c0)
        return 0

    lax.fori_loop(0, nc - 1 - nz, bwd, 0, unroll=True)
    sb_ref[nz] = st_ref[...].astype(BF16)

    st_ref[...] = jnp.zeros(st_ref.shape, F32)

    def fwd(c, _):
        c0 = pl.multiple_of(c * ch, ch)
        q = rope(q_ref[0, pl.ds(c0, ch), :].astype(F32), c0)
        k = load_k(c0)
        v = v_ref[0, pl.ds(c0, ch), :]
        inner = _dot_nt(q.astype(BF16), k.astype(BF16)) * tab_ref[0, 0]
        o = (_dot(inner.astype(BF16), v)
             + _dot((q * tab_ref[0, 1]).astype(BF16), st_ref[...].astype(BF16))
             + _dot((q * tab_ref[0, 2]).astype(BF16), sb_ref[c]))
        st_ref[...] = cd * st_ref[...] + kv_outer(k * tab_ref[0, 3], c0)
        o = o * lax.rsqrt(jnp.mean(o * o, axis=-1, keepdims=True) + RMS_EPS)
        g = g_ref[0, pl.ds(c0, ch), :].astype(F32)
        o_ref[0, pl.ds(c0, ch), :] = (o * (g * jax.nn.sigmoid(g))).astype(o_ref.dtype)
        return 0

    lax.fori_loop(0, nc, fwd, 0, unroll=3 if nc % 3 == 0 else 1)


def _ret(qkvg, n_ctx):
    b, p, d6 = qkvg.shape
    d = d6 // 6
    dk = d // RET_HEADS
    dv = 2 * dk
    ch = dk
    assert dk == 2 * LANES and n_ctx == ch and p % ch == 0
    nc = p // ch
    cos, sin, tabs, chunk_decay = _ret_tables(n_ctx, p - n_ctx, dk, ch)
    kern = functools.partial(_ret_kernel, ch=ch, nc=nc)
    nh = RET_HEADS
    return pl.pallas_call(
        kern,
        grid_spec=pltpu.PrefetchScalarGridSpec(
            num_scalar_prefetch=1,
            grid=(b, nh),
            in_specs=[pl.BlockSpec((1, p, dk), lambda bi, h, cd: (bi, 0, h)),
                      pl.BlockSpec((1, p, dk), lambda bi, h, cd: (bi, 0, nh + h)),
                      pl.BlockSpec((1, p, dv), lambda bi, h, cd: (bi, 0, nh + h)),
                      pl.BlockSpec((1, p, dv), lambda bi, h, cd: (bi, 0, 2 * nh + h)),
                      pl.BlockSpec((p, dk), lambda bi, h, cd: (0, 0)),
                      pl.BlockSpec((p, dk), lambda bi, h, cd: (0, 0)),
                      pl.BlockSpec((1, 5, ch, dk), lambda bi, h, cd: (h, 0, 0, 0))],
            out_specs=pl.BlockSpec((1, p, dv), lambda bi, h, cd: (bi, 0, h)),
            scratch_shapes=[pltpu.VMEM((nc, dk, dv), BF16), pltpu.VMEM((dk, dv), F32)]),
        out_shape=jax.ShapeDtypeStruct((b, p, nh * dv), BF16),
        compiler_params=_params(("parallel", "parallel")),
        name="retention",
    )(chunk_decay, qkvg, qkvg, qkvg, qkvg, cos, sin, tabs)


def _router_kernel(lg_ref, rb_ref, tri_ref, e_ref, w_ref, rk_ref, cnt_ref, run_ref):
    @pl.when(pl.program_id(0) == 0)
    def _():
        run_ref[...] = jnp.zeros(run_ref.shape, F32)

    lg = lg_ref[...]
    ex = jnp.exp(lg - jnp.max(lg, axis=0, keepdims=True))
    probs = ex / jnp.sum(ex, axis=0, keepdims=True)
    sel = probs + rb_ref[...]
    epg = EXPERTS_PER_GROUP
    best = grp = cur = curp = None
    for g in range(N_GROUPS):
        s = [sel[g * epg + i:g * epg + i + 1, :] for i in range(epg)]
        pr = [probs[g * epg + i:g * epg + i + 1, :] for i in range(epg)]
        top2 = None
        for i in range(epg):
            for j in range(i + 1, epg):
                top2 = s[i] + s[j] if top2 is None else jnp.maximum(top2, s[i] + s[j])
        if g == 0:
            best, grp, cur, curp = top2, jnp.zeros(top2.shape, jnp.int32), s, pr
        else:
            better = top2 > best
            best = jnp.where(better, top2, best)
            grp = jnp.where(better, g, grp)
            cur = [jnp.where(better, s[i], cur[i]) for i in range(epg)]
            curp = [jnp.where(better, pr[i], curp[i]) for i in range(epg)]
    b1, i1, p1 = cur[0], jnp.zeros(best.shape, jnp.int32), curp[0]
    for i in range(1, epg):
        gt = cur[i] > b1
        b1, i1, p1 = jnp.where(gt, cur[i], b1), jnp.where(gt, i, i1), jnp.where(gt, curp[i], p1)
    b2 = i2 = p2 = None
    for i in range(epg):
        v = jnp.where(i1 == i, -jnp.inf, cur[i])
        if b2 is None:
            b2, i2, p2 = v, jnp.zeros(best.shape, jnp.int32), curp[0]
        else:
            gt = v > b2
            b2, i2, p2 = jnp.where(gt, v, b2), jnp.where(gt, i, i2), jnp.where(gt, curp[i], p2)
    e1 = grp * epg + i1
    e2 = grp * epg + i2
    inv = 1.0 / (p1 + p2)
    e_ref[0:1, :] = e1
    e_ref[1:2, :] = e2
    w_ref[0:1, :] = p1 * inv
    w_ref[1:2, :] = p2 * inv

    eidx = lax.broadcasted_iota(jnp.int32, lg.shape, 0)
    eq1 = eidx == e1
    eq2 = eidx == e2
    member = jnp.where(eq1, 1.0, jnp.where(eq2, 1.0, 0.0))
    before = _dot(member.astype(BF16), tri_ref[...]) + run_ref[...]
    rk_ref[0:1, :] = jnp.sum(jnp.where(eq1, before, 0.0), axis=0, keepdims=True).astype(jnp.int32)
    rk_ref[1:2, :] = jnp.sum(jnp.where(eq2, before, 0.0), axis=0, keepdims=True).astype(jnp.int32)
    run_ref[...] = run_ref[...] + jnp.sum(member, axis=1, keepdims=True)
    cnt_ref[...] = run_ref[...]


def _router(logits_t, router_b):
    e, n = logits_t.shape
    tt = 512
    assert n % tt == 0
    tri = (jnp.arange(tt)[:, None] < jnp.arange(tt)[None, :]).astype(BF16)
    kn = jax.ShapeDtypeStruct((TOP_K, n), jnp.int32)
    return pl.pallas_call(
        _router_kernel,
        grid=(n // tt,),
        in_specs=[pl.BlockSpec((e, tt), lambda i: (0, i)),
                  pl.BlockSpec((e, 1), lambda i: (0, 0)),
                  pl.BlockSpec((tt, tt), lambda i: (0, 0))],
        out_specs=[pl.BlockSpec((TOP_K, tt), lambda i: (0, i)),
                   pl.BlockSpec((TOP_K, tt), lambda i: (0, i)),
                   pl.BlockSpec((TOP_K, tt), lambda i: (0, i)),
                   pl.BlockSpec((e, 1), lambda i: (0, 0))],
        out_shape=[kn, jax.ShapeDtypeStruct((TOP_K, n), F32), kn, jax.ShapeDtypeStruct((e, 1), F32)],
        scratch_shapes=[pltpu.VMEM((e, 1), F32)],
        compiler_params=_params(("arbitrary",)),
        name="router",
    )(logits_t, router_b.astype(F32).reshape(e, 1), tri)


def _moe_kernel(be_ref, nu_ref, xs_ref, wg_ref, wu_ref, wd_ref, ys_ref, wgb, wub, wdb, *, fc):
    i = pl.program_id(0)
    used = i < nu_ref[0]
    new_expert = jnp.logical_or(i == 0, be_ref[i] != be_ref[jnp.maximum(i - 1, 0)])

    @pl.when(jnp.logical_and(used, new_expert))
    def _():
        wgb[...] = wg_ref[0, 0].astype(BF16)
        wub[...] = wu_ref[0, 0].astype(BF16)
        wdb[...] = wd_ref[0, 0].astype(BF16)

    @pl.when(used)
    def _():
        x = xs_ref[...]
        f = wgb.shape[-1]
        acc = jnp.zeros(ys_ref.shape, F32)
        for c in range(f // fc):
            sl = slice(c * fc, (c + 1) * fc)
            g = _dot(x, wgb[:, sl])
            u = _dot(x, wub[:, sl])
            a = (g * jax.nn.sigmoid(g) * u).astype(BF16)
            acc = acc + _dot(a, wdb[sl, :])
        ys_ref[...] = acc.astype(ys_ref.dtype)

    @pl.when(jnp.logical_not(used))
    def _():
        ys_ref[...] = jnp.zeros(ys_ref.shape, ys_ref.dtype)


def _moe_experts(xs, block_expert, n_used, w_gate, w_up, w_down, layer):
    n_rows, d = xs.shape
    f = w_gate.shape[-1]
    bm = MOE_BLOCK
    kern = functools.partial(_moe_kernel, fc=_pick(f, (512, 256, 128)))
    return pl.pallas_call(
        kern,
        grid_spec=pltpu.PrefetchScalarGridSpec(
            num_scalar_prefetch=2,
            grid=(n_rows // bm,),
            in_specs=[pl.BlockSpec((bm, d), lambda i, be, nu: (i, 0)),
                      pl.BlockSpec((1, 1, d, f), lambda i, be, nu: (layer, be[i], 0, 0)),
                      pl.BlockSpec((1, 1, d, f), lambda i, be, nu: (layer, be[i], 0, 0)),
                      pl.BlockSpec((1, 1, f, d), lambda i, be, nu: (layer, be[i], 0, 0))],
            out_specs=pl.BlockSpec((bm, d), lambda i, be, nu: (i, 0)),
            scratch_shapes=[pltpu.VMEM((d, f), BF16), pltpu.VMEM((d, f), BF16), pltpu.VMEM((f, d), BF16)]),
        out_shape=jax.ShapeDtypeStruct((n_rows, d), BF16),
        compiler_params=_params(("arbitrary",)),
        name="moe_experts",
    )(block_expert, n_used, xs, w_gate, w_up, w_down)


def _moe(h_rows, tok_ids, logits_t, router_b, w_gate, w_up, w_down, layer):
    n = tok_ids.shape[0]
    bm = MOE_BLOCK
    expert, weight, rank, counts = _router(logits_t, router_b)
    counts = counts[:, 0].astype(jnp.int32)
    padded = (counts + bm - 1) // bm * bm
    pad_end = jnp.cumsum(padded)
    pad_start = pad_end - padded
    dest = rank + jnp.sum(jnp.where(expert[..., None] == jnp.arange(N_EXPERTS), pad_start, 0), axis=-1)
    n_blocks = -(-(n * TOP_K) // bm) + N_EXPERTS
    n_rows = n_blocks * bm
    tok = tok_ids.astype(jnp.int32)
    src = (jnp.arange(n_rows, dtype=jnp.int32) % h_rows.shape[0]).at[dest.reshape(-1)].set(
        jnp.tile(tok, TOP_K), unique_indices=True)
    block_start = jnp.arange(n_blocks, dtype=jnp.int32) * bm
    block_expert = jnp.minimum(
        jnp.sum((pad_end[None, :] <= block_start[:, None]).astype(jnp.int32), axis=1), N_EXPERTS - 1)
    n_used = (pad_end[-1] // bm).astype(jnp.int32).reshape(1)
    ys = _moe_experts(h_rows[src], block_expert, n_used, w_gate, w_up, w_down, layer)
    return [ys[dest[k]] for k in range(TOP_K)], weight


def _final_kernel(x_ref, y0_ref, y1_ref, gw_ref, mod_ref, g_ref, o_ref):
    gate = mod_ref[0, 2 * N_MOD - 1:2 * N_MOD, :]
    xn = x_ref[0] + gate * _ffn_sum(y0_ref, y1_ref, gw_ref, slice(None))
    o_ref[0] = xn * lax.rsqrt(jnp.mean(xn * xn, axis=-1, keepdims=True) + RMS_EPS) * g_ref[...]


def _final(t, ys, gw, mod, g, n_ctx):
    b, p, d = t.shape
    rows = p - n_ctx
    tm = _pick(n_ctx, (512, 256))
    assert rows % tm == 0
    off = n_ctx // tm
    row_spec = pl.BlockSpec((1, tm, d), lambda bi, i: (bi, i, 0))
    return pl.pallas_call(
        _final_kernel,
        grid=(b, rows // tm),
        in_specs=[pl.BlockSpec((1, tm, d), lambda bi, i: (bi, i + off, 0)),
                  row_spec, row_spec,
                  pl.BlockSpec((1, tm, TOP_K), lambda bi, i: (bi, i, 0)),
                  pl.BlockSpec((1, 2 * N_MOD, d), lambda bi, i: (bi, 0, 0)),
                  pl.BlockSpec((1, d), lambda bi, i: (0, 0))],
        out_specs=row_spec,
        out_shape=jax.ShapeDtypeStruct((b, rows, d), F32),
        compiler_params=_params(("parallel", "parallel")),
        name="ffn_final",
    )(t, ys[0], ys[1], gw, mod, g.reshape(1, d))


def kernel(x, c, ctx, c_ctx, ada_w, ada_b, norm_mix_g, norm_ffn_g, final_norm_g, lru_w_in, lru_conv_w, lru_conv_b, lru_gate_w, lru_gate_b, lru_lambda, lru_w_out, na_w_qkv, na_rpb, na_w_o, ret_w_qkvg, ret_w_o, router_w, router_b, moe_w_gate, moe_w_up, moe_w_down):
    b, s, d = x.shape
    n_ctx = ctx.shape[1]
    p = n_ctx + s
    depth = ada_w.shape[0]
    t = jnp.concatenate([ctx, x], axis=1)

    pad_rows = -(b + 1) % 16
    cc = jnp.concatenate([c, c_ctx[None, :], jnp.zeros((pad_rows, d), F32)], axis=0)
    mods = _ada(cc, ada_w, ada_b)
    mod_x = mods[:, :b].reshape(depth, b, N_MOD, d)
    mod_z = jnp.broadcast_to(mods[:, b].reshape(depth, 1, N_MOD, d), (depth, b, N_MOD, d))
    mods = jnp.concatenate([mod_z, mod_x], axis=2)

    rw32 = router_w.astype(F32)
    rw_hi = rw32.astype(BF16)
    rw_lo = (rw32 - rw_hi.astype(F32)).astype(BF16)
    rw = jnp.concatenate([rw_hi, rw_lo, jnp.zeros((d, ROUTER_LANES - 2 * N_EXPERTS), BF16)], axis=1)

    all_rows = jnp.arange(b * p, dtype=jnp.int32)
    pending = None
    for i in range(depth):
        kind, j = i % 3, i // 3
        last = i == depth - 1
        mod = _at(mods, i)
        if kind == 0:
            proj, t = _norm_proj(t, _at(norm_mix_g, i), mod, _at(lru_w_in, j).astype(BF16), n_ctx,
                                 act_cols=d, pending=pending)
            a = _lru(proj, _at(lru_conv_w, j), _at(lru_conv_b, j), _at(lru_gate_w, j), _at(lru_gate_b, j),
                     _at(lru_lambda, j), n_ctx)
            w_o = _at(lru_w_out, j)
        elif kind == 1:
            qkv, t = _norm_proj(t, _at(norm_mix_g, i), mod, _at(na_w_qkv, j).astype(BF16), n_ctx, pending=pending)
            a = _na(qkv, _at(na_rpb, j), n_ctx)
            w_o = _at(na_w_o, j)
        else:
            qkvg, t = _norm_proj(t, _at(norm_mix_g, i), mod, _at(ret_w_qkvg, j).astype(BF16), n_ctx, pending=pending)
            a = _ret(qkvg, n_ctx)
            w_o = _at(ret_w_o, j)
        t, h2, pl_ = _out_proj(a, w_o.astype(BF16), t, mod, _at(norm_ffn_g, i), rw, n_ctx)
        logits = (pl_[..., :N_EXPERTS] + pl_[..., N_EXPERTS:2 * N_EXPERTS])
        if last:
            tok_ids = all_rows.reshape(b, p)[:, n_ctx:].reshape(-1)
            logits = logits[:, n_ctx:]
        else:
            tok_ids = all_rows
        ys, wts = _moe(h2.reshape(b * p, d), tok_ids, logits.reshape(-1, N_EXPERTS).T, router_b,
                       moe_w_gate, moe_w_up, moe_w_down, i)
        ys = [y.reshape(b, -1, d) for y in ys]
        wts = wts.T.reshape(b, -1, TOP_K)
        pending = (ys[0], ys[1], wts, mod)
    return _final(t, ys, wts, mod, final_norm_g, n_ctx)
```

```python
import functools

import jax
import jax.numpy as jnp
import numpy as np
from jax import lax
from jax.experimental import pallas as pl
from jax.experimental.pallas import tpu as pltpu

F32 = jnp.float32
BF16 = jnp.bfloat16

GRID_W = 64
N_MOD = 6
RMS_EPS = 1e-6
LRU_BLOCKS = 8
CONV_W = 4
CONV_LEFT = CONV_W // 2
LRU_C = 8.0
NA_HEADS = 16
NA_ROWS = 8
NA_COLS = 16
NA_QROWS = 4
NEG_INF = -1e30
RET_HEADS = 4
ROPE_BASE = 10000.0
N_EXPERTS = 16
N_GROUPS = 4
EXPERTS_PER_GROUP = N_EXPERTS // N_GROUPS
TOP_K = 2
MOE_BLOCK = 512
LANES = 128
ROUTER_LANES = 128
SCAN_TILE = 256
VMEM_LIMIT = 56 * 1024 * 1024
N_STREAMS = 2


def _pick(n, candidates):
    for c in candidates:
        if n % c == 0:
            return c
    raise ValueError(f"no tile in {candidates} divides {n}")


def _params(sem):
    return pltpu.CompilerParams(dimension_semantics=sem, vmem_limit_bytes=VMEM_LIMIT)


def _at(w, i):
    return lax.index_in_dim(w, i, axis=0, keepdims=False)


def _dot(a, b):
    return jnp.dot(a, b, preferred_element_type=F32)


def _dot_nt(a, b):
    return lax.dot_general(a, b, (((1,), (1,)), ((), ())), preferred_element_type=F32)


def _dot_tn(a, b):
    return lax.dot_general(a, b, (((0,), (0,)), ((), ())), preferred_element_type=F32)


def _is_ctx(pos0, tm, n_ctx):
    return (pos0 + lax.broadcasted_iota(jnp.int32, (tm, 1), 0)) < n_ctx


def _mod_row(mod_ref, is_ctx, k):
    return jnp.where(is_ctx, mod_ref[0, k:k + 1, :], mod_ref[0, N_MOD + k:N_MOD + k + 1, :])


def _norm_mod(x, g, mod_ref, is_ctx, k):
    y = x * lax.rsqrt(jnp.mean(x * x, axis=-1, keepdims=True) + RMS_EPS) * g
    return y * (1.0 + _mod_row(mod_ref, is_ctx, k + 1)) + _mod_row(mod_ref, is_ctx, k)


def _ada_kernel(cc_ref, w_ref, b_ref, o_ref):
    cc = cc_ref[...]
    s = (cc * jax.nn.sigmoid(cc)).astype(BF16)
    o_ref[0] = _dot(s, w_ref[0].astype(BF16)) + b_ref[0]


def _ada(cc, ada_w, ada_b):
    depth, d, n = ada_w.shape
    rows = cc.shape[0]
    tn = _pick(n, (1536, 1024, 512, 256, 128))
    return pl.pallas_call(
        _ada_kernel,
        grid=(depth, n // tn),
        in_specs=[pl.BlockSpec((rows, d), lambda i, j: (0, 0)),
                  pl.BlockSpec((1, d, tn), lambda i, j: (i, 0, j)),
                  pl.BlockSpec((1, 1, tn), lambda i, j: (i, 0, j))],
        out_specs=pl.BlockSpec((1, rows, tn), lambda i, j: (i, 0, j)),
        out_shape=jax.ShapeDtypeStruct((depth, rows, n), F32),
        compiler_params=_params(("parallel", "parallel")),
        name="ada_mod",
    )(cc, ada_w, ada_b.reshape(depth, 1, n))


def _ffn_sum(y0_ref, y1_ref, gw_ref, rows):
    gw = gw_ref[0, rows, :]
    return gw[:, 0:1] * y0_ref[0, rows, :].astype(F32) + gw[:, 1:2] * y1_ref[0, rows, :].astype(F32)


def _proj_kernel(*refs, tm, rs, tn, n_ctx, act_cols, pending):
    if pending:
        x_ref, y0_ref, y1_ref, gw_ref, pmod_ref, g_ref, mod_ref, w_ref, o_ref, xo_ref = refs
    else:
        x_ref, g_ref, mod_ref, w_ref, o_ref = refs
    i = pl.program_id(1)
    n = w_ref.shape[1]
    for s in range(tm // rs):
        rows = slice(s * rs, (s + 1) * rs)
        is_ctx = _is_ctx(i * tm + s * rs, rs, n_ctx)
        x = x_ref[0, rows, :]
        if pending:
            x = x + _mod_row(pmod_ref, is_ctx, 5) * _ffn_sum(y0_ref, y1_ref, gw_ref, rows)
            xo_ref[0, rows, :] = x
        h = _norm_mod(x, g_ref[...], mod_ref, is_ctx, 0).astype(BF16)
        for c in range(n // tn):
            cols = slice(c * tn, (c + 1) * tn)
            y = _dot(h, w_ref[:, cols])
            if c * tn < act_cols:
                y = jax.nn.gelu(y)
            o_ref[0, rows, cols] = y.astype(o_ref.dtype)


def _norm_proj(t, g, mod, w, n_ctx, act_cols=0, pending=None):
    b, p, d = t.shape
    n = w.shape[1]
    tm = _pick(p, (768, 512, 256)) if n <= 4 * d else _pick(p, (384, 256))
    rs = _pick(tm, (256, 128))
    tn = _pick(n, (1024, 512))
    assert act_cols % tn == 0
    kern = functools.partial(_proj_kernel, tm=tm, rs=rs, tn=tn, n_ctx=n_ctx, act_cols=act_cols,
                             pending=pending is not None)
    row_spec = pl.BlockSpec((1, tm, d), lambda bi, i: (bi, i, 0))
    mod_spec = pl.BlockSpec((1, 2 * N_MOD, d), lambda bi, i: (bi, 0, 0))
    in_specs = [row_spec]
    args = [t]
    if pending is not None:
        in_specs += [row_spec, row_spec, pl.BlockSpec((1, tm, TOP_K), lambda bi, i: (bi, i, 0)), mod_spec]
        args += [pending[0], pending[1], pending[2], pending[3]]
    in_specs += [pl.BlockSpec((1, d), lambda bi, i: (0, 0)), mod_spec,
                 pl.BlockSpec((d, n), lambda bi, i: (0, 0), pipeline_mode=pl.Buffered(1))]
    args += [g.reshape(1, d), mod, w]
    out_specs = [pl.BlockSpec((1, tm, n), lambda bi, i: (bi, i, 0))]
    out_shape = [jax.ShapeDtypeStruct((b, p, n), BF16)]
    if pending is not None:
        out_specs.append(row_spec)
        out_shape.append(jax.ShapeDtypeStruct((b, p, d), F32))
    res = pl.pallas_call(
        kern,
        grid=(b, p // tm),
        in_specs=in_specs,
        out_specs=out_specs,
        out_shape=out_shape,
        compiler_params=_params(("parallel", "parallel")),
        name="norm_proj",
    )(*args)
    return (res[0], res[1]) if pending is not None else (res[0], t)


def _out_kernel(a_ref, w_ref, x_ref, mod_ref, g_ref, rw_ref, xo_ref, h_ref, p_ref, *, tm, rs, n_ctx):
    for s in range(tm // rs):
        rows = slice(s * rs, (s + 1) * rs)
        is_ctx = _is_ctx(pl.program_id(1) * tm + s * rs, rs, n_ctx)
        y = _dot(a_ref[0, rows, :], w_ref[...])
        xn = x_ref[0, rows, :] + _mod_row(mod_ref, is_ctx, 2) * y
        xo_ref[0, rows, :] = xn
        h = _norm_mod(xn, g_ref[...], mod_ref, is_ctx, 3)
        hi = h.astype(BF16)
        h_ref[0, rows, :] = hi
        lo = (h - hi.astype(F32)).astype(BF16)
        p_ref[0, rows, :] = _dot(hi, rw_ref[...]) + _dot(lo, rw_ref[...])


def _out_proj(a, w, t, mod, g, rw, n_ctx):
    b, p, d = t.shape
    k = a.shape[-1]
    tm = _pick(p, (768, 512, 256))
    kern = functools.partial(_out_kernel, tm=tm, rs=_pick(tm, (256, 128)), n_ctx=n_ctx)
    return pl.pallas_call(
        kern,
        grid=(b, p // tm),
        in_specs=[pl.BlockSpec((1, tm, k), lambda bi, i: (bi, i, 0)),
                  pl.BlockSpec((k, d), lambda bi, i: (0, 0)),
                  pl.BlockSpec((1, tm, d), lambda bi, i: (bi, i, 0)),
                  pl.BlockSpec((1, 2 * N_MOD, d), lambda bi, i: (bi, 0, 0)),
                  pl.BlockSpec((1, d), lambda bi, i: (0, 0)),
                  pl.BlockSpec((d, ROUTER_LANES), lambda bi, i: (0, 0))],
        out_specs=[pl.BlockSpec((1, tm, d), lambda bi, i: (bi, i, 0)),
                   pl.BlockSpec((1, tm, d), lambda bi, i: (bi, i, 0)),
                   pl.BlockSpec((1, tm, ROUTER_LANES), lambda bi, i: (bi, i, 0))],
        out_shape=[jax.ShapeDtypeStruct((b, p, d), F32),
                   jax.ShapeDtypeStruct((b, p, d), BF16),
                   jax.ShapeDtypeStruct((b, p, ROUTER_LANES), F32)],
        compiler_params=_params(("parallel", "parallel")),
        name="out_proj",
    )(a, w, t, mod, g.reshape(1, d), rw)


def _lru_kernel(gt_ref, up_ref, cw_ref, cb_ref, wg_ref, gb_ref, lam_ref, o_ref,
                upad, u_s, hf_s, hb_s, af, bf, ab, bb, *, n_ctx, tt, nt):
    p = nt * tt
    tc = o_ref.shape[-1]
    pad = 8
    upad[0:pad, :] = jnp.zeros((pad, tc), F32)
    upad[pad + p:pad + p + pad, :] = jnp.zeros((pad, tc), F32)
    for t in range(nt):
        upad[pad + t * tt:pad + (t + 1) * tt, :] = up_ref[0, t * tt:(t + 1) * tt, :].astype(F32)

    neg_lam = -lam_ref[...]
    sp = jnp.maximum(neg_lam, 0.0) + jnp.log1p(jnp.exp(-jnp.abs(neg_lam)))
    rowi = lax.broadcasted_iota(jnp.int32, (tt, 1), 0)

    def conv_tile(t):
        r0 = t * tt
        first = r0 in (0, n_ctx)
        last = r0 + tt in (n_ctx, p)
        acc = jnp.broadcast_to(cb_ref[...], (tt, tc))
        for kk in range(CONV_W):
            d = kk - CONV_LEFT
            xs = upad[pad + r0 + d:pad + r0 + d + tt, :]
            if first and d < 0:
                xs = jnp.where(rowi >= -d, xs, 0.0)
            if last and d > 0:
                xs = jnp.where(rowi < tt - d, xs, 0.0)
            acc = acc + cw_ref[kk:kk + 1, :] * xs
        return acc

    def gates(u, d, a_ref, b_ref):
        ub = u.astype(BF16)
        for n in range(tc // LANES):
            sl = slice(n * LANES, (n + 1) * LANES)
            ri = _dot(ub[:, sl], wg_ref[d, n])
            r = jax.nn.sigmoid(ri[:, :LANES] + gb_ref[d, 0:1, sl])
            ig = jax.nn.sigmoid(ri[:, LANES:] + gb_ref[d, 1:2, sl])
            log_a = (-LRU_C) * r * sp[d:d + 1, sl]
            a = jnp.exp(log_a)
            one_minus_a2 = -jnp.tanh(log_a) * (a * a + 1.0)
            a_ref[:, sl] = a
            b_ref[:, sl] = jnp.sqrt(one_minus_a2) * (ig * u[:, sl])

    nz = n_ctx // tt
    fwd_order = list(range(nt))
    bwd_order = list(range(nz - 1, -1, -1)) + list(range(nt - 1, nz - 1, -1))
    for t in range(nt):
        u_s[t * tt:(t + 1) * tt, :] = conv_tile(t)

    sub = 8
    carry = (jnp.zeros((1, tc), F32), jnp.zeros((1, tc), F32))
    for tf, tb in zip(fwd_order, bwd_order):
        gates(u_s[tf * tt:(tf + 1) * tt, :], 0, af, bf)
        gates(u_s[tb * tt:(tb + 1) * tt, :], 1, ab, bb)

        def step(jj, c, tf=tf, tb=tb):
            hf, hb = c
            f0 = pl.multiple_of(jj * sub, sub)
            b0 = pl.multiple_of(tt - sub - jj * sub, sub)
            a_f, b_f = af[pl.ds(f0, sub), :], bf[pl.ds(f0, sub), :]
            a_b, b_b = ab[pl.ds(b0, sub), :], bb[pl.ds(b0, sub), :]
            out_f, out_b = [], [None] * sub
            for r in range(sub):
                hf = a_f[r:r + 1, :] * hf + b_f[r:r + 1, :]
                out_f.append(hf)
                rb = sub - 1 - r
                hb = a_b[rb:rb + 1, :] * hb + b_b[rb:rb + 1, :]
                out_b[rb] = hb
            hf_s[pl.ds(tf * tt + f0, sub), :] = jnp.concatenate(out_f, axis=0)
            hb_s[pl.ds(tb * tt + b0, sub), :] = jnp.concatenate(out_b, axis=0)
            return hf, hb

        carry = lax.fori_loop(0, tt // sub, step, carry)

    for t in range(nt):
        rs = slice(t * tt, (t + 1) * tt)
        o_ref[0, rs, :] = ((hf_s[rs, :] + hb_s[rs, :]) * gt_ref[0, rs, :].astype(F32)).astype(o_ref.dtype)


def _lru(proj, conv_w, conv_b, gate_w, gate_b, lam, n_ctx):
    b, p, d2 = proj.shape
    d = d2 // 2
    tc = 512
    tt = SCAN_TILE
    assert p % tt == 0 and n_ctx % tt == 0 and d % tc == 0 and d // LRU_BLOCKS == LANES
    nt = p // tt
    nct = d // tc
    wg = jnp.concatenate([gate_w[:, 0], gate_w[:, 1]], axis=-1).astype(BF16)
    kern = functools.partial(_lru_kernel, n_ctx=n_ctx, tt=tt, nt=nt)
    return pl.pallas_call(
        kern,
        grid=(b, nct),
        in_specs=[pl.BlockSpec((1, p, tc), lambda bi, ci: (bi, 0, ci)),
                  pl.BlockSpec((1, p, tc), lambda bi, ci: (bi, 0, nct + ci)),
                  pl.BlockSpec((CONV_W, tc), lambda bi, ci: (0, ci)),
                  pl.BlockSpec((1, tc), lambda bi, ci: (0, ci)),
                  pl.BlockSpec((2, tc // LANES, LANES, 2 * LANES), lambda bi, ci: (0, ci, 0, 0)),
                  pl.BlockSpec((2, 2, tc), lambda bi, ci: (0, 0, ci)),
                  pl.BlockSpec((2, tc), lambda bi, ci: (0, ci))],
        out_specs=pl.BlockSpec((1, p, tc), lambda bi, ci: (bi, 0, ci)),
        out_shape=jax.ShapeDtypeStruct((b, p, d), BF16),
        scratch_shapes=[pltpu.VMEM((p + 16, tc), F32), pltpu.VMEM((p, tc), F32),
                        pltpu.VMEM((p, tc), F32), pltpu.VMEM((p, tc), F32),
                        pltpu.VMEM((tt, tc), F32), pltpu.VMEM((tt, tc), F32),
                        pltpu.VMEM((tt, tc), F32), pltpu.VMEM((tt, tc), F32)],
        compiler_params=_params(("parallel", "parallel")),
        name="rglru",
    )(proj, proj, conv_w, conv_b.reshape(1, d), wg, gate_b, lam)


def _na_window(rb, rows):
    win = NA_QROWS + NA_ROWS - 1
    return min(max(NA_QROWS * rb - NA_ROWS // 2, 0), rows - win)


def _na_bias(rpb, rows):
    nb = rows // NA_QROWS
    win = NA_QROWS + NA_ROWS - 1
    kr = min(NA_ROWS, rows)
    n_rel_r, n_rel_c = 2 * NA_ROWS - 1, 2 * NA_COLS - 1
    col = np.arange(GRID_W)
    c0 = np.clip(col - NA_COLS // 2, 0, GRID_W - NA_COLS)
    valid_c = (col[None, :] >= c0[:, None]) & (col[None, :] < c0[:, None] + NA_COLS)
    rel_c = np.clip(col[None, :] - col[:, None] + NA_COLS - 1, 0, n_rel_c - 1)
    pick_c = (rel_c[..., None] == np.arange(n_rel_c)).astype(np.float32)
    pick_r, valid = [], []
    for rb in (0, 1, nb - 1):
        r_abs = NA_QROWS * rb + np.arange(NA_QROWS)
        r0 = np.clip(r_abs - kr // 2, 0, rows - kr)
        k_abs = _na_window(rb, rows) + np.arange(win)
        valid_r = (k_abs[None, :] >= r0[:, None]) & (k_abs[None, :] < r0[:, None] + kr)
        rel_r = np.clip(k_abs[None, :] - r_abs[:, None] + NA_ROWS - 1, 0, n_rel_r - 1)
        pick_r.append((rel_r[..., None] == np.arange(n_rel_r)).astype(np.float32))
        valid.append(valid_r[:, None, :, None] & valid_c[None, :, None, :])
    rows_sel = jnp.einsum('hab,tqwa->htqwb', rpb, jnp.asarray(np.stack(pick_r)), precision=lax.Precision.HIGHEST)
    bias = jnp.einsum('htqwb,ckb->htqcwk', rows_sel, jnp.asarray(pick_c), precision=lax.Precision.HIGHEST)
    bias = jnp.where(jnp.asarray(np.stack(valid))[None], bias, NEG_INF)
    return bias.reshape(rpb.shape[0], 3, NA_QROWS * GRID_W, win * GRID_W)


def _na_kernel(q_ref, k_ref, v_ref, bias_ref, o_ref, vx_ref, *, n_ctx, rows):
    hd = LANES // 2
    scale = hd ** -0.5
    nq = NA_QROWS * GRID_W
    nb = rows // NA_QROWS
    nk = (NA_QROWS + NA_ROWS - 1) * GRID_W
    first = lax.broadcasted_iota(jnp.int32, (1, LANES), 1) < hd
    vx_ref[:, 0:LANES] = v_ref[0]
    vx_ref[:, LANES:2 * LANES] = jnp.ones((vx_ref.shape[0], LANES), BF16)

    def attend(qb, ks, bias_ty):
        n = qb.shape[0]
        qb = qb * scale
        zero = jnp.zeros_like(qb)
        q2 = jnp.concatenate([jnp.where(first, qb, zero), jnp.where(first, zero, qb)], axis=0)
        s_ctx = _dot_nt(q2, k_ref[0, 0:n_ctx, :])
        m = jnp.max(s_ctx, axis=-1, keepdims=True)
        if ks is not None:
            s_loc = _dot_nt(q2, k_ref[0, ks:ks + nk, :]) + bias_ref[0, bias_ty]
            m = jnp.maximum(m, jnp.max(s_loc, axis=-1, keepdims=True))
        o = _dot(jnp.exp(s_ctx - m).astype(BF16), vx_ref[0:n_ctx, :])
        if ks is not None:
            o = o + _dot(jnp.exp(s_loc - m).astype(BF16), vx_ref[ks:ks + nk, :])
        o = o[:, 0:LANES] * (1.0 / o[:, LANES:2 * LANES])
        return jnp.where(first, o[0:n], o[n:2 * n])

    o_ref[0, 0:n_ctx, :] = attend(q_ref[0, 0:n_ctx, :], None, None).astype(o_ref.dtype)
    for rb in range(nb):
        qs = n_ctx + rb * nq
        ks = n_ctx + _na_window(rb, rows) * GRID_W
        ty = 0 if rb == 0 else (2 if rb == nb - 1 else 1)
        o_ref[0, qs:qs + nq, :] = attend(q_ref[0, qs:qs + nq, :], ks, ty).astype(o_ref.dtype)


def _na(qkv, rpb, n_ctx):
    b, p, d3 = qkv.shape
    d = d3 // 3
    rows = (p - n_ctx) // GRID_W
    assert d // NA_HEADS == LANES // 2 and rows % NA_QROWS == 0 and rows // NA_QROWS >= 3
    nhp = d // LANES
    bias = _na_bias(rpb.astype(F32), rows)
    nq, nk = bias.shape[2], bias.shape[3]
    bias = bias.reshape(nhp, 2, 3, nq, nk).transpose(0, 2, 1, 3, 4).reshape(nhp, 3, 2 * nq, nk)
    kern = functools.partial(_na_kernel, n_ctx=n_ctx, rows=rows)
    return pl.pallas_call(
        kern,
        grid=(nhp, b),
        in_specs=[pl.BlockSpec((1, p, LANES), lambda hp, bi: (bi, 0, hp)),
                  pl.BlockSpec((1, p, LANES), lambda hp, bi: (bi, 0, nhp + hp)),
                  pl.BlockSpec((1, p, LANES), lambda hp, bi: (bi, 0, 2 * nhp + hp)),
                  pl.BlockSpec((1, 3, 2 * nq, nk), lambda hp, bi: (hp, 0, 0, 0))],
        out_specs=pl.BlockSpec((1, p, LANES), lambda hp, bi: (bi, 0, hp)),
        out_shape=jax.ShapeDtypeStruct((b, p, d), BF16),
        scratch_shapes=[pltpu.VMEM((p, 2 * LANES), BF16)],
        compiler_params=_params(("parallel", "parallel")),
        name="nbr_attn",
    )(qkv, qkv, qkv, bias)


def _ret_tables(n_ctx, s, dk, ch):
    quarter = dk // 4
    pos = jnp.arange(s)
    inv = ROPE_BASE ** (-jnp.arange(quarter, dtype=F32) / quarter)
    ang_r = (pos // GRID_W).astype(F32)[:, None] * inv
    ang_c = (pos % GRID_W).astype(F32)[:, None] * inv
    cos = jnp.concatenate([jnp.cos(ang_r)] * 2 + [jnp.cos(ang_c)] * 2, axis=-1)
    sin = jnp.concatenate([-jnp.sin(ang_r), jnp.sin(ang_r), -jnp.sin(ang_c), jnp.sin(ang_c)], axis=-1)
    cos = jnp.concatenate([jnp.ones((n_ctx, dk), F32), cos], axis=0)
    sin = jnp.concatenate([jnp.zeros((n_ctx, dk), F32), sin], axis=0)
    log_gamma = jnp.log1p(-(2.0 ** (-5.0 - jnp.arange(RET_HEADS, dtype=F32))))[:, None, None]
    pq = jnp.arange(ch, dtype=F32)
    col = jnp.broadcast_to(pq[:, None], (ch, dk))[None]
    tabs = jnp.stack([
        jnp.exp(jnp.abs(pq[:, None] - pq[None, :])[None] * log_gamma),
        jnp.exp((col + 1.0) * log_gamma),
        jnp.exp((ch - col) * log_gamma),
        jnp.exp((ch - 1.0 - col) * log_gamma),
        jnp.exp(col * log_gamma),
    ], axis=1)
    chunk_decay = jnp.exp(ch * log_gamma[:, 0, 0])
    return cos, sin, tabs, chunk_decay


def _ret_kernel(cd_ref, q_ref, k_ref, v_ref, g_ref, cos_ref, sin_ref, tab_ref, o_ref,
                sb_ref, st_ref, *, ch, nc):
    dk = q_ref.shape[-1]
    cd = cd_ref[pl.program_id(1)]
    k_scale = dk ** -0.5

    def rope(t, c0):
        parts = []
        for hf in range(dk // LANES):
            sl = slice(hf * LANES, (hf + 1) * LANES)
            th = t[:, sl]
            parts.append(th * cos_ref[pl.ds(c0, ch), sl] + pltpu.roll(th, LANES // 2, 1) * sin_ref[pl.ds(c0, ch), sl])
        return jnp.concatenate(parts, axis=-1)

    def load_k(c0):
        return rope(k_ref[0, pl.ds(c0, ch), :].astype(F32) * k_scale, c0)

    def kv_outer(kdec, c0):
        return _dot_tn(kdec.astype(BF16), v_ref[0, pl.ds(c0, ch), :])

    nz = 1
    sb_ref[0] = jnp.zeros(sb_ref.shape[1:], BF16)
    st_ref[...] = kv_outer(load_k(0) * tab_ref[0, 4], 0)

    def bwd(j, _):
        c = nc - 1 - j
        c0 = pl.multiple_of(c * ch, ch)
        sb_ref[c] = st_ref[...].astype(BF16)
        st_ref[...] = cd * st_ref[...] + kv_outer(load_k(c0) * tab_ref[0, 4], c0)
        return 0

    lax.fori_loop(0, nc - 1 - nz, bwd, 0, unroll=True)
    sb_ref[nz] = st_ref[...].astype(BF16)

    st_ref[...] = jnp.zeros(st_ref.shape, F32)

    def fwd(c, _):
        c0 = pl.multiple_of(c * ch, ch)
        q = rope(q_ref[0, pl.ds(c0, ch), :].astype(F32), c0)
        k = load_k(c0)
        v = v_ref[0, pl.ds(c0, ch), :]
        inner = _dot_nt(q.astype(BF16), k.astype(BF16)) * tab_ref[0, 0]
        o = (_dot(inner.astype(BF16), v)
             + _dot((q * tab_ref[0, 1]).astype(BF16), st_ref[...].astype(BF16))
             + _dot((q * tab_ref[0, 2]).astype(BF16), sb_ref[c]))
        st_ref[...] = cd * st_ref[...] + kv_outer(k * tab_ref[0, 3], c0)
        o = o * lax.rsqrt(jnp.mean(o * o, axis=-1, keepdims=True) + RMS_EPS)
        g = g_ref[0, pl.ds(c0, ch), :].astype(F32)
        o_ref[0, pl.ds(c0, ch), :] = (o * (g * jax.nn.sigmoid(g))).astype(o_ref.dtype)
        return 0

    lax.fori_loop(0, nc, fwd, 0, unroll=3 if nc % 3 == 0 else 1)


def _ret(qkvg, n_ctx):
    b, p, d6 = qkvg.shape
    d = d6 // 6
    dk = d // RET_HEADS
    dv = 2 * dk
    ch = dk
    assert dk == 2 * LANES and n_ctx == ch and p % ch == 0
    nc = p // ch
    cos, sin, tabs, chunk_decay = _ret_tables(n_ctx, p - n_ctx, dk, ch)
    kern = functools.partial(_ret_kernel, ch=ch, nc=nc)
    nh = RET_HEADS
    return pl.pallas_call(
        kern,
        grid_spec=pltpu.PrefetchScalarGridSpec(
            num_scalar_prefetch=1,
            grid=(b, nh),
            in_specs=[pl.BlockSpec((1, p, dk), lambda bi, h, cd: (bi, 0, h)),
                      pl.BlockSpec((1, p, dk), lambda bi, h, cd: (bi, 0, nh + h)),
                      pl.BlockSpec((1, p, dv), lambda bi, h, cd: (bi, 0, nh + h)),
                      pl.BlockSpec((1, p, dv), lambda bi, h, cd: (bi, 0, 2 * nh + h)),
                      pl.BlockSpec((p, dk), lambda bi, h, cd: (0, 0)),
                      pl.BlockSpec((p, dk), lambda bi, h, cd: (0, 0)),
                      pl.BlockSpec((1, 5, ch, dk), lambda bi, h, cd: (h, 0, 0, 0))],
            out_specs=pl.BlockSpec((1, p, dv), lambda bi, h, cd: (bi, 0, h)),
            scratch_shapes=[pltpu.VMEM((nc, dk, dv), BF16), pltpu.VMEM((dk, dv), F32)]),
        out_shape=jax.ShapeDtypeStruct((b, p, nh * dv), BF16),
        compiler_params=_params(("parallel", "parallel")),
        name="retention",
    )(chunk_decay, qkvg, qkvg, qkvg, qkvg, cos, sin, tabs)


def _router_kernel(lg_ref, rb_ref, tri_ref, e_ref, w_ref, rk_ref, cnt_ref, run_ref):
    @pl.when(pl.program_id(0) == 0)
    def _():
        run_ref[...] = jnp.zeros(run_ref.shape, F32)

    lg = lg_ref[...]
    ex = jnp.exp(lg - jnp.max(lg, axis=0, keepdims=True))
    probs = ex / jnp.sum(ex, axis=0, keepdims=True)
    sel = probs + rb_ref[...]
    epg = EXPERTS_PER_GROUP
    best = grp = cur = curp = None
    for g in range(N_GROUPS):
        s = [sel[g * epg + i:g * epg + i + 1, :] for i in range(epg)]
        pr = [probs[g * epg + i:g * epg + i + 1, :] for i in range(epg)]
        top2 = None
        for i in range(epg):
            for j in range(i + 1, epg):
                top2 = s[i] + s[j] if top2 is None else jnp.maximum(top2, s[i] + s[j])
        if g == 0:
            best, grp, cur, curp = top2, jnp.zeros(top2.shape, jnp.int32), s, pr
        else:
            better = top2 > best
            best = jnp.where(better, top2, best)
            grp = jnp.where(better, g, grp)
            cur = [jnp.where(better, s[i], cur[i]) for i in range(epg)]
            curp = [jnp.where(better, pr[i], curp[i]) for i in range(epg)]
    b1, i1, p1 = cur[0], jnp.zeros(best.shape, jnp.int32), curp[0]
    for i in range(1, epg):
        gt = cur[i] > b1
        b1, i1, p1 = jnp.where(gt, cur[i], b1), jnp.where(gt, i, i1), jnp.where(gt, curp[i], p1)
    b2 = i2 = p2 = None
    for i in range(epg):
        v = jnp.where(i1 == i, -jnp.inf, cur[i])
        if b2 is None:
            b2, i2, p2 = v, jnp.zeros(best.shape, jnp.int32), curp[0]
        else:
            gt = v > b2
            b2, i2, p2 = jnp.where(gt, v, b2), jnp.where(gt, i, i2), jnp.where(gt, curp[i], p2)
    e1 = grp * epg + i1
    e2 = grp * epg + i2
    inv = 1.0 / (p1 + p2)
    e_ref[0:1, :] = e1
    e_ref[1:2, :] = e2
    w_ref[0:1, :] = p1 * inv
    w_ref[1:2, :] = p2 * inv

    eidx = lax.broadcasted_iota(jnp.int32, lg.shape, 0)
    eq1 = eidx == e1
    eq2 = eidx == e2
    member = jnp.where(eq1, 1.0, jnp.where(eq2, 1.0, 0.0))
    before = _dot(member.astype(BF16), tri_ref[...]) + run_ref[...]
    rk_ref[0:1, :] = jnp.sum(jnp.where(eq1, before, 0.0), axis=0, keepdims=True).astype(jnp.int32)
    rk_ref[1:2, :] = jnp.sum(jnp.where(eq2, before, 0.0), axis=0, keepdims=True).astype(jnp.int32)
    run_ref[...] = run_ref[...] + jnp.sum(member, axis=1, keepdims=True)
    cnt_ref[...] = run_ref[...]


def _router(logits_t, router_b):
    e, n = logits_t.shape
    tt = 512
    assert n % tt == 0
    tri = (jnp.arange(tt)[:, None] < jnp.arange(tt)[None, :]).astype(BF16)
    kn = jax.ShapeDtypeStruct((TOP_K, n), jnp.int32)
    return pl.pallas_call(
        _router_kernel,
        grid=(n // tt,),
        in_specs=[pl.BlockSpec((e, tt), lambda i: (0, i)),
                  pl.BlockSpec((e, 1), lambda i: (0, 0)),
                  pl.BlockSpec((tt, tt), lambda i: (0, 0))],
        out_specs=[pl.BlockSpec((TOP_K, tt), lambda i: (0, i)),
                   pl.BlockSpec((TOP_K, tt), lambda i: (0, i)),
                   pl.BlockSpec((TOP_K, tt), lambda i: (0, i)),
                   pl.BlockSpec((e, 1), lambda i: (0, 0))],
        out_shape=[kn, jax.ShapeDtypeStruct((TOP_K, n), F32), kn, jax.ShapeDtypeStruct((e, 1), F32)],
        scratch_shapes=[pltpu.VMEM((e, 1), F32)],
        compiler_params=_params(("arbitrary",)),
        name="router",
    )(logits_t, router_b.astype(F32).reshape(e, 1), tri)


def _moe_kernel(be_ref, nu_ref, xs_ref, wg_ref, wu_ref, wd_ref, ys_ref, wgb, wub, wdb, *, fc):
    i = pl.program_id(0)
    used = i < nu_ref[0]
    new_expert = jnp.logical_or(i == 0, be_ref[i] != be_ref[jnp.maximum(i - 1, 0)])

    @pl.when(jnp.logical_and(used, new_expert))
    def _():
        wgb[...] = wg_ref[0, 0].astype(BF16)
        wub[...] = wu_ref[0, 0].astype(BF16)
        wdb[...] = wd_ref[0, 0].astype(BF16)

    @pl.when(used)
    def _():
        x = xs_ref[...]
        f = wgb.shape[-1]
        acc = jnp.zeros(ys_ref.shape, F32)
        for c in range(f // fc):
            sl = slice(c * fc, (c + 1) * fc)
            g = _dot(x, wgb[:, sl])
            u = _dot(x, wub[:, sl])
            a = (g * jax.nn.sigmoid(g) * u).astype(BF16)
            acc = acc + _dot(a, wdb[sl, :])
        ys_ref[...] = acc.astype(ys_ref.dtype)

    @pl.when(jnp.logical_not(used))
    def _():
        ys_ref[...] = jnp.zeros(ys_ref.shape, ys_ref.dtype)


def _moe_experts(xs, block_expert, n_used, w_gate, w_up, w_down, layer):
    n_rows, d = xs.shape
    f = w_gate.shape[-1]
    bm = MOE_BLOCK
    kern = functools.partial(_moe_kernel, fc=_pick(f, (512, 256, 128)))
    return pl.pallas_call(
        kern,
        grid_spec=pltpu.PrefetchScalarGridSpec(
            num_scalar_prefetch=2,
            grid=(n_rows // bm,),
            in_specs=[pl.BlockSpec((bm, d), lambda i, be, nu: (i, 0)),
                      pl.BlockSpec((1, 1, d, f), lambda i, be, nu: (layer, be[i], 0, 0)),
                      pl.BlockSpec((1, 1, d, f), lambda i, be, nu: (layer, be[i], 0, 0)),
                      pl.BlockSpec((1, 1, f, d), lambda i, be, nu: (layer, be[i], 0, 0))],
            out_specs=pl.BlockSpec((bm, d), lambda i, be, nu: (i, 0)),
            scratch_shapes=[pltpu.VMEM((d, f), BF16), pltpu.VMEM((d, f), BF16), pltpu.VMEM((f, d), BF16)]),
        out_shape=jax.ShapeDtypeStruct((n_rows, d), BF16),
        compiler_params=_params(("arbitrary",)),
        name="moe_experts",
    )(block_expert, n_used, xs, w_gate, w_up, w_down)


def _moe(h_rows, tok_ids, logits_t, router_b, w_gate, w_up, w_down, layer):
    n = tok_ids.shape[0]
    bm = MOE_BLOCK
    expert, weight, rank, counts = _router(logits_t, router_b)
    counts = counts[:, 0].astype(jnp.int32)
    padded = (counts + bm - 1) // bm * bm
    pad_end = jnp.cumsum(padded)
    pad_start = pad_end - padded
    dest = rank + jnp.sum(jnp.where(expert[..., None] == jnp.arange(N_EXPERTS), pad_start, 0), axis=-1)
    n_blocks = -(-(n * TOP_K) // bm) + N_EXPERTS
    n_rows = n_blocks * bm
    tok = tok_ids.astype(jnp.int32)
    src = (jnp.arange(n_rows, dtype=jnp.int32) % h_rows.shape[0]).at[dest.reshape(-1)].set(
        jnp.tile(tok, TOP_K), unique_indices=True)
    block_start = jnp.arange(n_blocks, dtype=jnp.int32) * bm
    block_expert = jnp.minimum(
        jnp.sum((pad_end[None, :] <= block_start[:, None]).astype(jnp.int32), axis=1), N_EXPERTS - 1)
    n_used = (pad_end[-1] // bm).astype(jnp.int32).reshape(1)
    ys = _moe_experts(h_rows[src], block_expert, n_used, w_gate, w_up, w_down, layer)
    return [ys[dest[k]] for k in range(TOP_K)], weight


def _final_kernel(x_ref, y0_ref, y1_ref, gw_ref, mod_ref, g_ref, o_ref):
    gate = mod_ref[0, 2 * N_MOD - 1:2 * N_MOD, :]
    xn = x_ref[0] + gate * _ffn_sum(y0_ref, y1_ref, gw_ref, slice(None))
    o_ref[0] = xn * lax.rsqrt(jnp.mean(xn * xn, axis=-1, keepdims=True) + RMS_EPS) * g_ref[...]


def _final(t, ys, gw, mod, g, n_ctx):
    b, p, d = t.shape
    rows = p - n_ctx
    tm = _pick(n_ctx, (512, 256))
    assert rows % tm == 0
    off = n_ctx // tm
    row_spec = pl.BlockSpec((1, tm, d), lambda bi, i: (bi, i, 0))
    return pl.pallas_call(
        _final_kernel,
        grid=(b, rows // tm),
        in_specs=[pl.BlockSpec((1, tm, d), lambda bi, i: (bi, i + off, 0)),
                  row_spec, row_spec,
                  pl.BlockSpec((1, tm, TOP_K), lambda bi, i: (bi, i, 0)),
                  pl.BlockSpec((1, 2 * N_MOD, d), lambda bi, i: (bi, 0, 0)),
                  pl.BlockSpec((1, d), lambda bi, i: (0, 0))],
        out_specs=row_spec,
        out_shape=jax.ShapeDtypeStruct((b, rows, d), F32),
        compiler_params=_params(("parallel", "parallel")),
        name="ffn_final",
    )(t, ys[0], ys[1], gw, mod, g.reshape(1, d))


def kernel(x, c, ctx, c_ctx, ada_w, ada_b, norm_mix_g, norm_ffn_g, final_norm_g, lru_w_in, lru_conv_w, lru_conv_b, lru_gate_w, lru_gate_b, lru_lambda, lru_w_out, na_w_qkv, na_rpb, na_w_o, ret_w_qkvg, ret_w_o, router_w, router_b, moe_w_gate, moe_w_up, moe_w_down):
    b, s, d = x.shape
    n_ctx = ctx.shape[1]
    p = n_ctx + s
    depth = ada_w.shape[0]
    t = jnp.concatenate([ctx, x], axis=1)

    pad_rows = -(b + 1) % 16
    cc = jnp.concatenate([c, c_ctx[None, :], jnp.zeros((pad_rows, d), F32)], axis=0)
    mods = _ada(cc, ada_w, ada_b)
    mod_x = mods[:, :b].reshape(depth, b, N_MOD, d)
    mod_z = jnp.broadcast_to(mods[:, b].reshape(depth, 1, N_MOD, d), (depth, b, N_MOD, d))
    mods = jnp.concatenate([mod_z, mod_x], axis=2)

    rw32 = router_w.astype(F32)
    rw_hi = rw32.astype(BF16)
    rw_lo = (rw32 - rw_hi.astype(F32)).astype(BF16)
    rw = jnp.concatenate([rw_hi, rw_lo, jnp.zeros((d, ROUTER_LANES - 2 * N_EXPERTS), BF16)], axis=1)

    n_streams = N_STREAMS if b % N_STREAMS == 0 else 1
    bs = b // n_streams
    all_rows = jnp.arange(bs * p, dtype=jnp.int32)
    ts = [t[k * bs:(k + 1) * bs] for k in range(n_streams)]
    pendings = [None] * n_streams
    outs = []
    for i in range(depth):
        kind, j = i % 3, i // 3
        last = i == depth - 1
        mods_i = [_at(mods, i)[k * bs:(k + 1) * bs] for k in range(n_streams)]
        routed = []
        for k in range(n_streams):
            t, mod, pending = ts[k], mods_i[k], pendings[k]
            if kind == 0:
                proj, t = _norm_proj(t, _at(norm_mix_g, i), mod, _at(lru_w_in, j).astype(BF16), n_ctx,
                                     act_cols=d, pending=pending)
                a = _lru(proj, _at(lru_conv_w, j), _at(lru_conv_b, j), _at(lru_gate_w, j), _at(lru_gate_b, j),
                         _at(lru_lambda, j), n_ctx)
                w_o = _at(lru_w_out, j)
            elif kind == 1:
                qkv, t = _norm_proj(t, _at(norm_mix_g, i), mod, _at(na_w_qkv, j).astype(BF16), n_ctx,
                                    pending=pending)
                a = _na(qkv, _at(na_rpb, j), n_ctx)
                w_o = _at(na_w_o, j)
            else:
                qkvg, t = _norm_proj(t, _at(norm_mix_g, i), mod, _at(ret_w_qkvg, j).astype(BF16), n_ctx,
                                     pending=pending)
                a = _ret(qkvg, n_ctx)
                w_o = _at(ret_w_o, j)
            t, h2, pl_ = _out_proj(a, w_o.astype(BF16), t, mod, _at(norm_ffn_g, i), rw, n_ctx)
            ts[k] = t
            routed.append((h2, pl_))
        for k in range(n_streams):
            h2, pl_ = routed[k]
            logits = (pl_[..., :N_EXPERTS] + pl_[..., N_EXPERTS:2 * N_EXPERTS])
            if last:
                tok_ids = all_rows.reshape(bs, p)[:, n_ctx:].reshape(-1)
                logits = logits[:, n_ctx:]
            else:
                tok_ids = all_rows
            ys, wts = _moe(h2.reshape(bs * p, d), tok_ids, logits.reshape(-1, N_EXPERTS).T, router_b,
                           moe_w_gate, moe_w_up, moe_w_down, i)
            ys = [y.reshape(bs, -1, d) for y in ys]
            wts = wts.T.reshape(bs, -1, TOP_K)
            pendings[k] = (ys[0], ys[1], wts, mods_i[k])
            if last:
                outs.append(_final(ts[k], ys, wts, mods_i[k], final_norm_g, n_ctx))
    return jnp.concatenate(outs, axis=0) if n_streams > 1 else outs[0]
```

```python
import functools

import jax
import jax.numpy as jnp
import numpy as np
from jax import lax
from jax.experimental import pallas as pl
from jax.experimental.pallas import tpu as pltpu

F32 = jnp.float32
BF16 = jnp.bfloat16

GRID_W = 64
N_MOD = 6
RMS_EPS = 1e-6
MIN_NORMAL = float(np.finfo(np.float32).tiny)
LRU_BLOCKS = 8
CONV_W = 4
CONV_LEFT = CONV_W // 2
LRU_C = 8.0
NA_HEADS = 16
NA_ROWS = 8
NA_COLS = 16
NA_QROWS = 4
NEG_INF = -1e30
RET_HEADS = 4
ROPE_BASE = 10000.0
N_EXPERTS = 16
N_GROUPS = 4
EXPERTS_PER_GROUP = N_EXPERTS // N_GROUPS
TOP_K = 2
MOE_BLOCK = 512
LANES = 128
ROUTER_LANES = 128
SCAN_TILE = 256
VMEM_LIMIT = 56 * 1024 * 1024


def _pick(n, candidates):
    for c in candidates:
        if n % c == 0:
            return c
    raise ValueError(f"no tile in {candidates} divides {n}")


def _params(sem):
    return pltpu.CompilerParams(dimension_semantics=sem, vmem_limit_bytes=VMEM_LIMIT)


def _at(w, i):
    return lax.index_in_dim(w, i, axis=0, keepdims=False)


def _dot(a, b):
    return jnp.dot(a, b, preferred_element_type=F32)


def _dot_nt(a, b):
    return lax.dot_general(a, b, (((1,), (1,)), ((), ())), preferred_element_type=F32)


def _dot_tn(a, b):
    return lax.dot_general(a, b, (((0,), (0,)), ((), ())), preferred_element_type=F32)


def _is_ctx(pos0, tm, n_ctx):
    return (pos0 + lax.broadcasted_iota(jnp.int32, (tm, 1), 0)) < n_ctx


def _mod_row(mod_ref, is_ctx, k):
    return jnp.where(is_ctx, mod_ref[0, k:k + 1, :], mod_ref[0, N_MOD + k:N_MOD + k + 1, :])


def _norm_mod(x, g, mod_ref, is_ctx, k):
    y = x * lax.rsqrt(jnp.mean(x * x, axis=-1, keepdims=True) + RMS_EPS) * g
    return y * (1.0 + _mod_row(mod_ref, is_ctx, k + 1)) + _mod_row(mod_ref, is_ctx, k)


def _ada_kernel(cc_ref, w_ref, b_ref, o_ref):
    cc = cc_ref[...]
    s = (cc * jax.nn.sigmoid(cc)).astype(BF16)
    o_ref[0] = _dot(s, w_ref[0].astype(BF16)) + b_ref[0]


def _ada(cc, ada_w, ada_b):
    depth, d, n = ada_w.shape
    rows = cc.shape[0]
    tn = _pick(n, (1536, 1024, 512, 256, 128))
    return pl.pallas_call(
        _ada_kernel,
        grid=(depth, n // tn),
        in_specs=[pl.BlockSpec((rows, d), lambda i, j: (0, 0)),
                  pl.BlockSpec((1, d, tn), lambda i, j: (i, 0, j)),
                  pl.BlockSpec((1, 1, tn), lambda i, j: (i, 0, j))],
        out_specs=pl.BlockSpec((1, rows, tn), lambda i, j: (i, 0, j)),
        out_shape=jax.ShapeDtypeStruct((depth, rows, n), F32),
        compiler_params=_params(("parallel", "parallel")),
        name="ada_mod",
    )(cc, ada_w, ada_b.reshape(depth, 1, n))


def _ffn_sum(y0_ref, y1_ref, gw_ref, rows):
    gw = gw_ref[0, rows, :]
    return gw[:, 0:1] * y0_ref[0, rows, :].astype(F32) + gw[:, 1:2] * y1_ref[0, rows, :].astype(F32)


def _proj_kernel(*refs, tm, rs, tn, n_ctx, act_cols, pending):
    if pending:
        x_ref, y0_ref, y1_ref, gw_ref, pmod_ref, g_ref, mod_ref, w_ref, o_ref, xo_ref = refs
    else:
        x_ref, g_ref, mod_ref, w_ref, o_ref = refs
    i = pl.program_id(1)
    n = w_ref.shape[1]
    for s in range(tm // rs):
        rows = slice(s * rs, (s + 1) * rs)
        is_ctx = _is_ctx(i * tm + s * rs, rs, n_ctx)
        x = x_ref[0, rows, :]
        if pending:
            x = x + _mod_row(pmod_ref, is_ctx, 5) * _ffn_sum(y0_ref, y1_ref, gw_ref, rows)
            xo_ref[0, rows, :] = x
        h = _norm_mod(x, g_ref[...], mod_ref, is_ctx, 0).astype(BF16)
        for c in range(n // tn):
            cols = slice(c * tn, (c + 1) * tn)
            y = _dot(h, w_ref[:, cols])
            if c * tn < act_cols:
                y = jax.nn.gelu(y)
            o_ref[0, rows, cols] = y.astype(o_ref.dtype)


def _norm_proj(t, g, mod, w, n_ctx, act_cols=0, pending=None):
    b, p, d = t.shape
    n = w.shape[1]
    tm = _pick(p, (768, 512, 256))
    rs = _pick(tm, (256, 128))
    tn = _pick(n, (1024, 512))
    assert act_cols % tn == 0
    kern = functools.partial(_proj_kernel, tm=tm, rs=rs, tn=tn, n_ctx=n_ctx, act_cols=act_cols,
                             pending=pending is not None)
    row_spec = pl.BlockSpec((1, tm, d), lambda bi, i: (bi, i, 0))
    mod_spec = pl.BlockSpec((1, 2 * N_MOD, d), lambda bi, i: (bi, 0, 0))
    in_specs = [row_spec]
    args = [t]
    if pending is not None:
        in_specs += [row_spec, row_spec, pl.BlockSpec((1, tm, TOP_K), lambda bi, i: (bi, i, 0)), mod_spec]
        args += [pending[0], pending[1], pending[2], pending[3]]
    in_specs += [pl.BlockSpec((1, d), lambda bi, i: (0, 0)), mod_spec,
                 pl.BlockSpec((d, n), lambda bi, i: (0, 0), pipeline_mode=pl.Buffered(1))]
    args += [g.reshape(1, d), mod, w]
    out_specs = [pl.BlockSpec((1, tm, n), lambda bi, i: (bi, i, 0))]
    out_shape = [jax.ShapeDtypeStruct((b, p, n), BF16)]
    if pending is not None:
        out_specs.append(row_spec)
        out_shape.append(jax.ShapeDtypeStruct((b, p, d), F32))
    res = pl.pallas_call(
        kern,
        grid=(b, p // tm),
        in_specs=in_specs,
        out_specs=out_specs,
        out_shape=out_shape,
        compiler_params=_params(("parallel", "parallel")),
        name="norm_proj",
    )(*args)
    return (res[0], res[1]) if pending is not None else (res[0], t)


def _out_kernel(a_ref, w_ref, x_ref, mod_ref, g_ref, rw_ref, xo_ref, h_ref, p_ref, *, tm, rs, n_ctx):
    for s in range(tm // rs):
        rows = slice(s * rs, (s + 1) * rs)
        is_ctx = _is_ctx(pl.program_id(1) * tm + s * rs, rs, n_ctx)
        y = _dot(a_ref[0, rows, :], w_ref[...])
        xn = x_ref[0, rows, :] + _mod_row(mod_ref, is_ctx, 2) * y
        xo_ref[0, rows, :] = xn
        h = _norm_mod(xn, g_ref[...], mod_ref, is_ctx, 3)
        hi = h.astype(BF16)
        h_ref[0, rows, :] = hi
        lo = (h - hi.astype(F32)).astype(BF16)
        p_ref[0, rows, :] = _dot(hi, rw_ref[...]) + _dot(lo, rw_ref[...])


def _out_proj(a, w, t, mod, g, rw, n_ctx):
    b, p, d = t.shape
    k = a.shape[-1]
    tm = _pick(p, (768, 512, 256))
    kern = functools.partial(_out_kernel, tm=tm, rs=_pick(tm, (256, 128)), n_ctx=n_ctx)
    return pl.pallas_call(
        kern,
        grid=(b, p // tm),
        in_specs=[pl.BlockSpec((1, tm, k), lambda bi, i: (bi, i, 0)),
                  pl.BlockSpec((k, d), lambda bi, i: (0, 0)),
                  pl.BlockSpec((1, tm, d), lambda bi, i: (bi, i, 0)),
                  pl.BlockSpec((1, 2 * N_MOD, d), lambda bi, i: (bi, 0, 0)),
                  pl.BlockSpec((1, d), lambda bi, i: (0, 0)),
                  pl.BlockSpec((d, ROUTER_LANES), lambda bi, i: (0, 0))],
        out_specs=[pl.BlockSpec((1, tm, d), lambda bi, i: (bi, i, 0)),
                   pl.BlockSpec((1, tm, d), lambda bi, i: (bi, i, 0)),
                   pl.BlockSpec((1, tm, ROUTER_LANES), lambda bi, i: (bi, i, 0))],
        out_shape=[jax.ShapeDtypeStruct((b, p, d), F32),
                   jax.ShapeDtypeStruct((b, p, d), BF16),
                   jax.ShapeDtypeStruct((b, p, ROUTER_LANES), F32)],
        compiler_params=_params(("parallel", "parallel")),
        name="out_proj",
    )(a, w, t, mod, g.reshape(1, d), rw)


def _lru_kernel(gt_ref, up_ref, cw_ref, cb_ref, wg_ref, gb_ref, lam_ref, o_ref,
                upad, u_s, hf_s, hb_s, af, bf, ab, bb, *, n_ctx, tt, nt):
    p = nt * tt
    tc = o_ref.shape[-1]
    pad = 8
    upad[0:pad, :] = jnp.zeros((pad, tc), F32)
    upad[pad + p:pad + p + pad, :] = jnp.zeros((pad, tc), F32)
    for t in range(nt):
        upad[pad + t * tt:pad + (t + 1) * tt, :] = up_ref[0, t * tt:(t + 1) * tt, :].astype(F32)

    neg_lam = -lam_ref[...]
    sp = jnp.maximum(neg_lam, 0.0) + jnp.log1p(jnp.exp(-jnp.abs(neg_lam)))
    rowi = lax.broadcasted_iota(jnp.int32, (tt, 1), 0)

    def conv_tile(t):
        r0 = t * tt
        first = r0 in (0, n_ctx)
        last = r0 + tt in (n_ctx, p)
        acc = jnp.broadcast_to(cb_ref[...], (tt, tc))
        for kk in range(CONV_W):
            d = kk - CONV_LEFT
            xs = upad[pad + r0 + d:pad + r0 + d + tt, :]
            if first and d < 0:
                xs = jnp.where(rowi >= -d, xs, 0.0)
            if last and d > 0:
                xs = jnp.where(rowi < tt - d, xs, 0.0)
            acc = acc + cw_ref[kk:kk + 1, :] * xs
        return acc

    half_c_sp = (0.5 * LRU_C) * sp

    def gates(u, d, a_ref, b_ref):
        ub = u.astype(BF16)
        for n in range(tc // LANES):
            sl = slice(n * LANES, (n + 1) * LANES)
            ri = _dot(ub[:, sl], wg_ref[d, n])
            t_r = jnp.tanh(ri[:, :LANES] + gb_ref[d, 0:1, sl])
            t_i = jnp.tanh(ri[:, LANES:] + gb_ref[d, 1:2, sl])
            c = half_c_sp[d:d + 1, sl]
            neg_log_a = c * t_r + c
            a = jnp.exp(-neg_log_a)
            one_minus_a2 = jnp.tanh(neg_log_a) * (a * a + 1.0)
            root = one_minus_a2 * lax.rsqrt(jnp.maximum(one_minus_a2, MIN_NORMAL))
            a_ref[:, sl] = a
            b_ref[:, sl] = root * ((t_i + 1.0) * (0.5 * u[:, sl]))

    nz = n_ctx // tt
    fwd_order = list(range(nt))
    bwd_order = list(range(nz - 1, -1, -1)) + list(range(nt - 1, nz - 1, -1))
    for t in range(nt):
        u_s[t * tt:(t + 1) * tt, :] = conv_tile(t)

    sub = 8
    carry = (jnp.zeros((1, tc), F32), jnp.zeros((1, tc), F32))
    for tf, tb in zip(fwd_order, bwd_order):
        gates(u_s[tf * tt:(tf + 1) * tt, :], 0, af, bf)
        gates(u_s[tb * tt:(tb + 1) * tt, :], 1, ab, bb)

        def step(jj, c, tf=tf, tb=tb):
            hf, hb = c
            f0 = pl.multiple_of(jj * sub, sub)
            b0 = pl.multiple_of(tt - sub - jj * sub, sub)
            a_f, b_f = af[pl.ds(f0, sub), :], bf[pl.ds(f0, sub), :]
            a_b, b_b = ab[pl.ds(b0, sub), :], bb[pl.ds(b0, sub), :]
            out_f, out_b = [], [None] * sub
            for r in range(sub):
                hf = a_f[r:r + 1, :] * hf + b_f[r:r + 1, :]
                out_f.append(hf)
                rb = sub - 1 - r
                hb = a_b[rb:rb + 1, :] * hb + b_b[rb:rb + 1, :]
                out_b[rb] = hb
            hf_s[pl.ds(tf * tt + f0, sub), :] = jnp.concatenate(out_f, axis=0)
            hb_s[pl.ds(tb * tt + b0, sub), :] = jnp.concatenate(out_b, axis=0)
            return hf, hb

        carry = lax.fori_loop(0, tt // sub, step, carry)

    for t in range(nt):
        rs = slice(t * tt, (t + 1) * tt)
        o_ref[0, rs, :] = ((hf_s[rs, :] + hb_s[rs, :]) * gt_ref[0, rs, :].astype(F32)).astype(o_ref.dtype)


def _lru(proj, conv_w, conv_b, gate_w, gate_b, lam, n_ctx):
    b, p, d2 = proj.shape
    d = d2 // 2
    tc = 512
    tt = SCAN_TILE
    assert p % tt == 0 and n_ctx % tt == 0 and d % tc == 0 and d // LRU_BLOCKS == LANES
    nt = p // tt
    nct = d // tc
    wg = (0.5 * jnp.concatenate([gate_w[:, 0], gate_w[:, 1]], axis=-1)).astype(BF16)
    gate_b = 0.5 * gate_b
    kern = functools.partial(_lru_kernel, n_ctx=n_ctx, tt=tt, nt=nt)
    return pl.pallas_call(
        kern,
        grid=(b, nct),
        in_specs=[pl.BlockSpec((1, p, tc), lambda bi, ci: (bi, 0, ci)),
                  pl.BlockSpec((1, p, tc), lambda bi, ci: (bi, 0, nct + ci)),
                  pl.BlockSpec((CONV_W, tc), lambda bi, ci: (0, ci)),
                  pl.BlockSpec((1, tc), lambda bi, ci: (0, ci)),
                  pl.BlockSpec((2, tc // LANES, LANES, 2 * LANES), lambda bi, ci: (0, ci, 0, 0)),
                  pl.BlockSpec((2, 2, tc), lambda bi, ci: (0, 0, ci)),
                  pl.BlockSpec((2, tc), lambda bi, ci: (0, ci))],
        out_specs=pl.BlockSpec((1, p, tc), lambda bi, ci: (bi, 0, ci)),
        out_shape=jax.ShapeDtypeStruct((b, p, d), BF16),
        scratch_shapes=[pltpu.VMEM((p + 16, tc), F32), pltpu.VMEM((p, tc), F32),
                        pltpu.VMEM((p, tc), F32), pltpu.VMEM((p, tc), F32),
                        pltpu.VMEM((tt, tc), F32), pltpu.VMEM((tt, tc), F32),
                        pltpu.VMEM((tt, tc), F32), pltpu.VMEM((tt, tc), F32)],
        compiler_params=_params(("parallel", "parallel")),
        name="rglru",
    )(proj, proj, conv_w, conv_b.reshape(1, d), wg, gate_b, lam)


def _na_window(rb, rows):
    win = NA_QROWS + NA_ROWS - 1
    return min(max(NA_QROWS * rb - NA_ROWS // 2, 0), rows - win)


def _na_bias(rpb, rows):
    nb = rows // NA_QROWS
    win = NA_QROWS + NA_ROWS - 1
    kr = min(NA_ROWS, rows)
    n_rel_r, n_rel_c = 2 * NA_ROWS - 1, 2 * NA_COLS - 1
    col = np.arange(GRID_W)
    c0 = np.clip(col - NA_COLS // 2, 0, GRID_W - NA_COLS)
    valid_c = (col[None, :] >= c0[:, None]) & (col[None, :] < c0[:, None] + NA_COLS)
    rel_c = np.clip(col[None, :] - col[:, None] + NA_COLS - 1, 0, n_rel_c - 1)
    pick_c = (rel_c[..., None] == np.arange(n_rel_c)).astype(np.float32)
    pick_r, valid = [], []
    for rb in (0, 1, nb - 1):
        r_abs = NA_QROWS * rb + np.arange(NA_QROWS)
        r0 = np.clip(r_abs - kr // 2, 0, rows - kr)
        k_abs = _na_window(rb, rows) + np.arange(win)
        valid_r = (k_abs[None, :] >= r0[:, None]) & (k_abs[None, :] < r0[:, None] + kr)
        rel_r = np.clip(k_abs[None, :] - r_abs[:, None] + NA_ROWS - 1, 0, n_rel_r - 1)
        pick_r.append((rel_r[..., None] == np.arange(n_rel_r)).astype(np.float32))
        valid.append(valid_r[:, None, :, None] & valid_c[None, :, None, :])
    rows_sel = jnp.einsum('hab,tqwa->htqwb', rpb, jnp.asarray(np.stack(pick_r)), precision=lax.Precision.HIGHEST)
    bias = jnp.einsum('htqwb,ckb->htqcwk', rows_sel, jnp.asarray(pick_c), precision=lax.Precision.HIGHEST)
    bias = jnp.where(jnp.asarray(np.stack(valid))[None], bias, NEG_INF)
    return bias.reshape(rpb.shape[0], 3, NA_QROWS * GRID_W, win * GRID_W)


def _na_kernel(q_ref, k_ref, v_ref, bias_ref, o_ref, vx_ref, *, n_ctx, rows):
    hd = LANES // 2
    scale = hd ** -0.5
    nq = NA_QROWS * GRID_W
    nb = rows // NA_QROWS
    nk = (NA_QROWS + NA_ROWS - 1) * GRID_W
    first = lax.broadcasted_iota(jnp.int32, (1, LANES), 1) < hd
    vx_ref[:, 0:LANES] = v_ref[0]
    vx_ref[:, LANES:2 * LANES] = jnp.ones((vx_ref.shape[0], LANES), BF16)

    def attend(qb, ks, bias_ty):
        n = qb.shape[0]
        qb = qb * scale
        zero = jnp.zeros_like(qb)
        q2 = jnp.concatenate([jnp.where(first, qb, zero), jnp.where(first, zero, qb)], axis=0)
        s_ctx = _dot_nt(q2, k_ref[0, 0:n_ctx, :])
        m = jnp.max(s_ctx, axis=-1, keepdims=True)
        if ks is not None:
            s_loc = _dot_nt(q2, k_ref[0, ks:ks + nk, :]) + bias_ref[0, bias_ty]
            m = jnp.maximum(m, jnp.max(s_loc, axis=-1, keepdims=True))
        o = _dot(jnp.exp(s_ctx - m).astype(BF16), vx_ref[0:n_ctx, :])
        if ks is not None:
            o = o + _dot(jnp.exp(s_loc - m).astype(BF16), vx_ref[ks:ks + nk, :])
        o = o[:, 0:LANES] * (1.0 / o[:, LANES:2 * LANES])
        return jnp.where(first, o[0:n], o[n:2 * n])

    o_ref[0, 0:n_ctx, :] = attend(q_ref[0, 0:n_ctx, :], None, None).astype(o_ref.dtype)
    for rb in range(nb):
        qs = n_ctx + rb * nq
        ks = n_ctx + _na_window(rb, rows) * GRID_W
        ty = 0 if rb == 0 else (2 if rb == nb - 1 else 1)
        o_ref[0, qs:qs + nq, :] = attend(q_ref[0, qs:qs + nq, :], ks, ty).astype(o_ref.dtype)


def _na(qkv, rpb, n_ctx):
    b, p, d3 = qkv.shape
    d = d3 // 3
    rows = (p - n_ctx) // GRID_W
    assert d // NA_HEADS == LANES // 2 and rows % NA_QROWS == 0 and rows // NA_QROWS >= 3
    nhp = d // LANES
    bias = _na_bias(rpb.astype(F32), rows)
    nq, nk = bias.shape[2], bias.shape[3]
    bias = bias.reshape(nhp, 2, 3, nq, nk).transpose(0, 2, 1, 3, 4).reshape(nhp, 3, 2 * nq, nk)
    kern = functools.partial(_na_kernel, n_ctx=n_ctx, rows=rows)
    return pl.pallas_call(
        kern,
        grid=(nhp, b),
        in_specs=[pl.BlockSpec((1, p, LANES), lambda hp, bi: (bi, 0, hp)),
                  pl.BlockSpec((1, p, LANES), lambda hp, bi: (bi, 0, nhp + hp)),
                  pl.BlockSpec((1, p, LANES), lambda hp, bi: (bi, 0, 2 * nhp + hp)),
                  pl.BlockSpec((1, 3, 2 * nq, nk), lambda hp, bi: (hp, 0, 0, 0))],
        out_specs=pl.BlockSpec((1, p, LANES), lambda hp, bi: (bi, 0, hp)),
        out_shape=jax.ShapeDtypeStruct((b, p, d), BF16),
        scratch_shapes=[pltpu.VMEM((p, 2 * LANES), BF16)],
        compiler_params=_params(("parallel", "parallel")),
        name="nbr_attn",
    )(qkv, qkv, qkv, bias)


def _ret_tables(n_ctx, s, dk, ch):
    quarter = dk // 4
    pos = jnp.arange(s)
    inv = ROPE_BASE ** (-jnp.arange(quarter, dtype=F32) / quarter)
    ang_r = (pos // GRID_W).astype(F32)[:, None] * inv
    ang_c = (pos % GRID_W).astype(F32)[:, None] * inv
    cos = jnp.concatenate([jnp.cos(ang_r)] * 2 + [jnp.cos(ang_c)] * 2, axis=-1)
    sin = jnp.concatenate([-jnp.sin(ang_r), jnp.sin(ang_r), -jnp.sin(ang_c), jnp.sin(ang_c)], axis=-1)
    cos = jnp.concatenate([jnp.ones((n_ctx, dk), F32), cos], axis=0)
    sin = jnp.concatenate([jnp.zeros((n_ctx, dk), F32), sin], axis=0)
    log_gamma = jnp.log1p(-(2.0 ** (-5.0 - jnp.arange(RET_HEADS, dtype=F32))))[:, None, None]
    pq = jnp.arange(ch, dtype=F32)
    col = jnp.broadcast_to(pq[:, None], (ch, dk))[None]
    tabs = jnp.stack([
        jnp.exp(jnp.abs(pq[:, None] - pq[None, :])[None] * log_gamma),
        jnp.exp((col + 1.0) * log_gamma),
        jnp.exp((ch - col) * log_gamma),
        jnp.exp((ch - 1.0 - col) * log_gamma),
        jnp.exp(col * log_gamma),
    ], axis=1)
    chunk_decay = jnp.exp(ch * log_gamma[:, 0, 0])
    return cos, sin, tabs, chunk_decay


def _ret_kernel(cd_ref, q_ref, k_ref, v_ref, g_ref, cos_ref, sin_ref, tab_ref, o_ref,
                sb_ref, st_ref, *, ch, nc):
    dk = q_ref.shape[-1]
    cd = cd_ref[pl.program_id(1)]
    k_scale = dk ** -0.5

    def rope(t, c0):
        parts = []
        for hf in range(dk // LANES):
            sl = slice(hf * LANES, (hf + 1) * LANES)
            th = t[:, sl]
            parts.append(th * cos_ref[pl.ds(c0, ch), sl] + pltpu.roll(th, LANES // 2, 1) * sin_ref[pl.ds(c0, ch), sl])
        return jnp.concatenate(parts, axis=-1)

    def load_k(c0):
        return rope(k_ref[0, pl.ds(c0, ch), :].astype(F32) * k_scale, c0)

    def kv_outer(kdec, c0):
        return _dot_tn(kdec.astype(BF16), v_ref[0, pl.ds(c0, ch), :])

    nz = 1
    sb_ref[0] = jnp.zeros(sb_ref.shape[1:], BF16)
    st_ref[...] = kv_outer(load_k(0) * tab_ref[0, 4], 0)

    def bwd(j, _):
        c = nc - 1 - j
        c0 = pl.multiple_of(c * ch, ch)
        sb_ref[c] = st_ref[...].astype(BF16)
        st_ref[...] = cd * st_ref[...] + kv_outer(load_k(c0) * tab_ref[0, 4], c0)
        return 0

    lax.fori_loop(0, nc - 1 - nz, bwd, 0, unroll=True)
    sb_ref[nz] = st_ref[...].astype(BF16)

    st_ref[...] = jnp.zeros(st_ref.shape, F32)

    def fwd(c, _):
        c0 = pl.multiple_of(c * ch, ch)
        q = rope(q_ref[0, pl.ds(c0, ch), :].astype(F32), c0)
        k = load_k(c0)
        v = v_ref[0, pl.ds(c0, ch), :]
        inner = _dot_nt(q.astype(BF16), k.astype(BF16)) * tab_ref[0, 0]
        o = (_dot(inner.astype(BF16), v)
             + _dot((q * tab_ref[0, 1]).astype(BF16), st_ref[...].astype(BF16))
             + _dot((q * tab_ref[0, 2]).astype(BF16), sb_ref[c]))
        st_ref[...] = cd * st_ref[...] + kv_outer(k * tab_ref[0, 3], c0)
        o = o * lax.rsqrt(jnp.mean(o * o, axis=-1, keepdims=True) + RMS_EPS)
        g = g_ref[0, pl.ds(c0, ch), :].astype(F32)
        o_ref[0, pl.ds(c0, ch), :] = (o * (g * jax.nn.sigmoid(g))).astype(o_ref.dtype)
        return 0

    lax.fori_loop(0, nc, fwd, 0, unroll=3 if nc % 3 == 0 else 1)


def _ret(qkvg, n_ctx):
    b, p, d6 = qkvg.shape
    d = d6 // 6
    dk = d // RET_HEADS
    dv = 2 * dk
    ch = dk
    assert dk == 2 * LANES and n_ctx == ch and p % ch == 0
    nc = p // ch
    cos, sin, tabs, chunk_decay = _ret_tables(n_ctx, p - n_ctx, dk, ch)
    kern = functools.partial(_ret_kernel, ch=ch, nc=nc)
    nh = RET_HEADS
    return pl.pallas_call(
        kern,
        grid_spec=pltpu.PrefetchScalarGridSpec(
            num_scalar_prefetch=1,
            grid=(b, nh),
            in_specs=[pl.BlockSpec((1, p, dk), lambda bi, h, cd: (bi, 0, h)),
                      pl.BlockSpec((1, p, dk), lambda bi, h, cd: (bi, 0, nh + h)),
                      pl.BlockSpec((1, p, dv), lambda bi, h, cd: (bi, 0, nh + h)),
                      pl.BlockSpec((1, p, dv), lambda bi, h, cd: (bi, 0, 2 * nh + h)),
                      pl.BlockSpec((p, dk), lambda bi, h, cd: (0, 0)),
                      pl.BlockSpec((p, dk), lambda bi, h, cd: (0, 0)),
                      pl.BlockSpec((1, 5, ch, dk), lambda bi, h, cd: (h, 0, 0, 0))],
            out_specs=pl.BlockSpec((1, p, dv), lambda bi, h, cd: (bi, 0, h)),
            scratch_shapes=[pltpu.VMEM((nc, dk, dv), BF16), pltpu.VMEM((dk, dv), F32)]),
        out_shape=jax.ShapeDtypeStruct((b, p, nh * dv), BF16),
        compiler_params=_params(("parallel", "parallel")),
        name="retention",
    )(chunk_decay, qkvg, qkvg, qkvg, qkvg, cos, sin, tabs)


def _router_kernel(lg_ref, rb_ref, tri_ref, e_ref, w_ref, rk_ref, cnt_ref, run_ref):
    @pl.when(pl.program_id(0) == 0)
    def _():
        run_ref[...] = jnp.zeros(run_ref.shape, F32)

    lg = lg_ref[...]
    ex = jnp.exp(lg - jnp.max(lg, axis=0, keepdims=True))
    probs = ex / jnp.sum(ex, axis=0, keepdims=True)
    sel = probs + rb_ref[...]
    epg = EXPERTS_PER_GROUP
    best = grp = cur = curp = None
    for g in range(N_GROUPS):
        s = [sel[g * epg + i:g * epg + i + 1, :] for i in range(epg)]
        pr = [probs[g * epg + i:g * epg + i + 1, :] for i in range(epg)]
        top2 = None
        for i in range(epg):
            for j in range(i + 1, epg):
                top2 = s[i] + s[j] if top2 is None else jnp.maximum(top2, s[i] + s[j])
        if g == 0:
            best, grp, cur, curp = top2, jnp.zeros(top2.shape, jnp.int32), s, pr
        else:
            better = top2 > best
            best = jnp.where(better, top2, best)
            grp = jnp.where(better, g, grp)
            cur = [jnp.where(better, s[i], cur[i]) for i in range(epg)]
            curp = [jnp.where(better, pr[i], curp[i]) for i in range(epg)]
    b1, i1, p1 = cur[0], jnp.zeros(best.shape, jnp.int32), curp[0]
    for i in range(1, epg):
        gt = cur[i] > b1
        b1, i1, p1 = jnp.where(gt, cur[i], b1), jnp.where(gt, i, i1), jnp.where(gt, curp[i], p1)
    b2 = i2 = p2 = None
    for i in range(epg):
        v = jnp.where(i1 == i, -jnp.inf, cur[i])
        if b2 is None:
            b2, i2, p2 = v, jnp.zeros(best.shape, jnp.int32), curp[0]
        else:
            gt = v > b2
            b2, i2, p2 = jnp.where(gt, v, b2), jnp.where(gt, i, i2), jnp.where(gt, curp[i], p2)
    e1 = grp * epg + i1
    e2 = grp * epg + i2
    inv = 1.0 / (p1 + p2)
    e_ref[0:1, :] = e1
    e_ref[1:2, :] = e2
    w_ref[0:1, :] = p1 * inv
    w_ref[1:2, :] = p2 * inv

    eidx = lax.broadcasted_iota(jnp.int32, lg.shape, 0)
    eq1 = eidx == e1
    eq2 = eidx == e2
    member = jnp.where(eq1, 1.0, jnp.where(eq2, 1.0, 0.0))
    before = _dot(member.astype(BF16), tri_ref[...]) + run_ref[...]
    rk_ref[0:1, :] = jnp.sum(jnp.where(eq1, before, 0.0), axis=0, keepdims=True).astype(jnp.int32)
    rk_ref[1:2, :] = jnp.sum(jnp.where(eq2, before, 0.0), axis=0, keepdims=True).astype(jnp.int32)
    run_ref[...] = run_ref[...] + jnp.sum(member, axis=1, keepdims=True)
    cnt_ref[...] = run_ref[...]


def _router(logits_t, router_b):
    e, n = logits_t.shape
    tt = 512
    assert n % tt == 0
    tri = (jnp.arange(tt)[:, None] < jnp.arange(tt)[None, :]).astype(BF16)
    kn = jax.ShapeDtypeStruct((TOP_K, n), jnp.int32)
    return pl.pallas_call(
        _router_kernel,
        grid=(n // tt,),
        in_specs=[pl.BlockSpec((e, tt), lambda i: (0, i)),
                  pl.BlockSpec((e, 1), lambda i: (0, 0)),
                  pl.BlockSpec((tt, tt), lambda i: (0, 0))],
        out_specs=[pl.BlockSpec((TOP_K, tt), lambda i: (0, i)),
                   pl.BlockSpec((TOP_K, tt), lambda i: (0, i)),
                   pl.BlockSpec((TOP_K, tt), lambda i: (0, i)),
                   pl.BlockSpec((e, 1), lambda i: (0, 0))],
        out_shape=[kn, jax.ShapeDtypeStruct((TOP_K, n), F32), kn, jax.ShapeDtypeStruct((e, 1), F32)],
        scratch_shapes=[pltpu.VMEM((e, 1), F32)],
        compiler_params=_params(("arbitrary",)),
        name="router",
    )(logits_t, router_b.astype(F32).reshape(e, 1), tri)


def _moe_kernel(be_ref, nu_ref, xs_ref, wg_ref, wu_ref, wd_ref, ys_ref, wgb, wub, wdb, *, fc):
    i = pl.program_id(0)
    used = i < nu_ref[0]
    new_expert = jnp.logical_or(i == 0, be_ref[i] != be_ref[jnp.maximum(i - 1, 0)])

    @pl.when(jnp.logical_and(used, new_expert))
    def _():
        wgb[...] = wg_ref[0, 0].astype(BF16)
        wub[...] = wu_ref[0, 0].astype(BF16)
        wdb[...] = wd_ref[0, 0].astype(BF16)

    @pl.when(used)
    def _():
        x = xs_ref[...]
        f = wgb.shape[-1]
        acc = jnp.zeros(ys_ref.shape, F32)
        for c in range(f // fc):
            sl = slice(c * fc, (c + 1) * fc)
            g = _dot(x, wgb[:, sl])
            u = _dot(x, wub[:, sl])
            a = (g * jax.nn.sigmoid(g) * u).astype(BF16)
            acc = acc + _dot(a, wdb[sl, :])
        ys_ref[...] = acc.astype(ys_ref.dtype)

    @pl.when(jnp.logical_not(used))
    def _():
        ys_ref[...] = jnp.zeros(ys_ref.shape, ys_ref.dtype)


def _moe_experts(xs, block_expert, n_used, w_gate, w_up, w_down, layer):
    n_rows, d = xs.shape
    f = w_gate.shape[-1]
    bm = MOE_BLOCK
    kern = functools.partial(_moe_kernel, fc=_pick(f, (512, 256, 128)))
    return pl.pallas_call(
        kern,
        grid_spec=pltpu.PrefetchScalarGridSpec(
            num_scalar_prefetch=2,
            grid=(n_rows // bm,),
            in_specs=[pl.BlockSpec((bm, d), lambda i, be, nu: (i, 0)),
                      pl.BlockSpec((1, 1, d, f), lambda i, be, nu: (layer, be[i], 0, 0)),
                      pl.BlockSpec((1, 1, d, f), lambda i, be, nu: (layer, be[i], 0, 0)),
                      pl.BlockSpec((1, 1, f, d), lambda i, be, nu: (layer, be[i], 0, 0))],
            out_specs=pl.BlockSpec((bm, d), lambda i, be, nu: (i, 0)),
            scratch_shapes=[pltpu.VMEM((d, f), BF16), pltpu.VMEM((d, f), BF16), pltpu.VMEM((f, d), BF16)]),
        out_shape=jax.ShapeDtypeStruct((n_rows, d), BF16),
        compiler_params=_params(("arbitrary",)),
        name="moe_experts",
    )(block_expert, n_used, xs, w_gate, w_up, w_down)


def _moe(h_rows, tok_ids, logits_t, router_b, w_gate, w_up, w_down, layer):
    n = tok_ids.shape[0]
    bm = MOE_BLOCK
    expert, weight, rank, counts = _router(logits_t, router_b)
    counts = counts[:, 0].astype(jnp.int32)
    padded = (counts + bm - 1) // bm * bm
    pad_end = jnp.cumsum(padded)
    pad_start = pad_end - padded
    dest = rank + jnp.sum(jnp.where(expert[..., None] == jnp.arange(N_EXPERTS), pad_start, 0), axis=-1)
    n_blocks = -(-(n * TOP_K) // bm) + N_EXPERTS
    n_rows = n_blocks * bm
    tok = tok_ids.astype(jnp.int32)
    src = (jnp.arange(n_rows, dtype=jnp.int32) % h_rows.shape[0]).at[dest.reshape(-1)].set(
        jnp.tile(tok, TOP_K), unique_indices=True)
    block_start = jnp.arange(n_blocks, dtype=jnp.int32) * bm
    block_expert = jnp.minimum(
        jnp.sum((pad_end[None, :] <= block_start[:, None]).astype(jnp.int32), axis=1), N_EXPERTS - 1)
    n_used = (pad_end[-1] // bm).astype(jnp.int32).reshape(1)
    ys = _moe_experts(h_rows[src], block_expert, n_used, w_gate, w_up, w_down, layer)
    return [ys[dest[k]] for k in range(TOP_K)], weight


def _final_kernel(x_ref, y0_ref, y1_ref, gw_ref, mod_ref, g_ref, o_ref):
    gate = mod_ref[0, 2 * N_MOD - 1:2 * N_MOD, :]
    xn = x_ref[0] + gate * _ffn_sum(y0_ref, y1_ref, gw_ref, slice(None))
    o_ref[0] = xn * lax.rsqrt(jnp.mean(xn * xn, axis=-1, keepdims=True) + RMS_EPS) * g_ref[...]


def _final(t, ys, gw, mod, g, n_ctx):
    b, p, d = t.shape
    rows = p - n_ctx
    tm = _pick(n_ctx, (512, 256))
    assert rows % tm == 0
    off = n_ctx // tm
    row_spec = pl.BlockSpec((1, tm, d), lambda bi, i: (bi, i, 0))
    return pl.pallas_call(
        _final_kernel,
        grid=(b, rows // tm),
        in_specs=[pl.BlockSpec((1, tm, d), lambda bi, i: (bi, i + off, 0)),
                  row_spec, row_spec,
                  pl.BlockSpec((1, tm, TOP_K), lambda bi, i: (bi, i, 0)),
                  pl.BlockSpec((1, 2 * N_MOD, d), lambda bi, i: (bi, 0, 0)),
                  pl.BlockSpec((1, d), lambda bi, i: (0, 0))],
        out_specs=row_spec,
        out_shape=jax.ShapeDtypeStruct((b, rows, d), F32),
        compiler_params=_params(("parallel", "parallel")),
        name="ffn_final",
    )(t, ys[0], ys[1], gw, mod, g.reshape(1, d))


def kernel(x, c, ctx, c_ctx, ada_w, ada_b, norm_mix_g, norm_ffn_g, final_norm_g, lru_w_in, lru_conv_w, lru_conv_b, lru_gate_w, lru_gate_b, lru_lambda, lru_w_out, na_w_qkv, na_rpb, na_w_o, ret_w_qkvg, ret_w_o, router_w, router_b, moe_w_gate, moe_w_up, moe_w_down):
    b, s, d = x.shape
    n_ctx = ctx.shape[1]
    p = n_ctx + s
    depth = ada_w.shape[0]
    t = jnp.concatenate([ctx, x], axis=1)

    pad_rows = -(b + 1) % 16
    cc = jnp.concatenate([c, c_ctx[None, :], jnp.zeros((pad_rows, d), F32)], axis=0)
    mods = _ada(cc, ada_w, ada_b)
    mod_x = mods[:, :b].reshape(depth, b, N_MOD, d)
    mod_z = jnp.broadcast_to(mods[:, b].reshape(depth, 1, N_MOD, d), (depth, b, N_MOD, d))
    mods = jnp.concatenate([mod_z, mod_x], axis=2)

    rw32 = router_w.astype(F32)
    rw_hi = rw32.astype(BF16)
    rw_lo = (rw32 - rw_hi.astype(F32)).astype(BF16)
    rw = jnp.concatenate([rw_hi, rw_lo, jnp.zeros((d, ROUTER_LANES - 2 * N_EXPERTS), BF16)], axis=1)

    all_rows = jnp.arange(b * p, dtype=jnp.int32)
    pending = None
    for i in range(depth):
        kind, j = i % 3, i // 3
        last = i == depth - 1
        mod = _at(mods, i)
        if kind == 0:
            proj, t = _norm_proj(t, _at(norm_mix_g, i), mod, _at(lru_w_in, j).astype(BF16), n_ctx,
                                 act_cols=d, pending=pending)
            a = _lru(proj, _at(lru_conv_w, j), _at(lru_conv_b, j), _at(lru_gate_w, j), _at(lru_gate_b, j),
                     _at(lru_lambda, j), n_ctx)
            w_o = _at(lru_w_out, j)
        elif kind == 1:
            qkv, t = _norm_proj(t, _at(norm_mix_g, i), mod, _at(na_w_qkv, j).astype(BF16), n_ctx, pending=pending)
            a = _na(qkv, _at(na_rpb, j), n_ctx)
            w_o = _at(na_w_o, j)
        else:
            qkvg, t = _norm_proj(t, _at(norm_mix_g, i), mod, _at(ret_w_qkvg, j).astype(BF16), n_ctx, pending=pending)
            a = _ret(qkvg, n_ctx)
            w_o = _at(ret_w_o, j)
        t, h2, pl_ = _out_proj(a, w_o.astype(BF16), t, mod, _at(norm_ffn_g, i), rw, n_ctx)
        logits = (pl_[..., :N_EXPERTS] + pl_[..., N_EXPERTS:2 * N_EXPERTS])
        if last:
            tok_ids = all_rows.reshape(b, p)[:, n_ctx:].reshape(-1)
            logits = logits[:, n_ctx:]
        else:
            tok_ids = all_rows
        ys, wts = _moe(h2.reshape(b * p, d), tok_ids, logits.reshape(-1, N_EXPERTS).T, router_b,
                       moe_w_gate, moe_w_up, moe_w_down, i)
        ys = [y.reshape(b, -1, d) for y in ys]
        wts = wts.T.reshape(b, -1, TOP_K)
        pending = (ys[0], ys[1], wts, mod)
    return _final(t, ys, wts, mod, final_norm_g, n_ctx)
```

```python
import functools

import jax
import jax.numpy as jnp
import numpy as np
from jax import lax
from jax.experimental import pallas as pl
from jax.experimental.pallas import tpu as pltpu

F32 = jnp.float32
BF16 = jnp.bfloat16

GRID_W = 64
N_MOD = 6
RMS_EPS = 1e-6
MIN_NORMAL = float(np.finfo(np.float32).tiny)
LRU_BLOCKS = 8
CONV_W = 4
CONV_LEFT = CONV_W // 2
LRU_C = 8.0
NA_HEADS = 16
NA_ROWS = 8
NA_COLS = 16
NA_QROWS = 4
NEG_INF = -1e30
RET_HEADS = 4
ROPE_BASE = 10000.0
N_EXPERTS = 16
N_GROUPS = 4
EXPERTS_PER_GROUP = N_EXPERTS // N_GROUPS
TOP_K = 2
MOE_BLOCK = 512
LANES = 128
ROUTER_LANES = 128
SCAN_TILE = 256
VMEM_LIMIT = 56 * 1024 * 1024


def _pick(n, candidates):
    for c in candidates:
        if n % c == 0:
            return c
    raise ValueError(f"no tile in {candidates} divides {n}")


def _params(sem):
    return pltpu.CompilerParams(dimension_semantics=sem, vmem_limit_bytes=VMEM_LIMIT)


def _at(w, i):
    return lax.index_in_dim(w, i, axis=0, keepdims=False)


def _dot(a, b):
    return jnp.dot(a, b, preferred_element_type=F32)


def _dot_nt(a, b):
    return lax.dot_general(a, b, (((1,), (1,)), ((), ())), preferred_element_type=F32)


def _dot_tn(a, b):
    return lax.dot_general(a, b, (((0,), (0,)), ((), ())), preferred_element_type=F32)


def _is_ctx(pos0, tm, n_ctx):
    return (pos0 + lax.broadcasted_iota(jnp.int32, (tm, 1), 0)) < n_ctx


def _mod_row(mod_ref, is_ctx, k):
    return jnp.where(is_ctx, mod_ref[0, k:k + 1, :], mod_ref[0, N_MOD + k:N_MOD + k + 1, :])


def _norm_mod(x, g, mod_ref, is_ctx, k):
    y = x * lax.rsqrt(jnp.mean(x * x, axis=-1, keepdims=True) + RMS_EPS) * g
    return y * (1.0 + _mod_row(mod_ref, is_ctx, k + 1)) + _mod_row(mod_ref, is_ctx, k)


def _ada_kernel(cc_ref, w_ref, b_ref, o_ref):
    cc = cc_ref[...]
    s = (cc * jax.nn.sigmoid(cc)).astype(BF16)
    o_ref[0] = _dot(s, w_ref[0].astype(BF16)) + b_ref[0]


def _ada(cc, ada_w, ada_b):
    depth, d, n = ada_w.shape
    rows = cc.shape[0]
    tn = _pick(n, (1536, 1024, 512, 256, 128))
    return pl.pallas_call(
        _ada_kernel,
        grid=(depth, n // tn),
        in_specs=[pl.BlockSpec((rows, d), lambda i, j: (0, 0)),
                  pl.BlockSpec((1, d, tn), lambda i, j: (i, 0, j)),
                  pl.BlockSpec((1, 1, tn), lambda i, j: (i, 0, j))],
        out_specs=pl.BlockSpec((1, rows, tn), lambda i, j: (i, 0, j)),
        out_shape=jax.ShapeDtypeStruct((depth, rows, n), F32),
        compiler_params=_params(("parallel", "parallel")),
        name="ada_mod",
    )(cc, ada_w, ada_b.reshape(depth, 1, n))


def _ffn_sum(y0_ref, y1_ref, gw_ref, rows):
    gw = gw_ref[0, rows, :]
    return gw[:, 0:1] * y0_ref[0, rows, :].astype(F32) + gw[:, 1:2] * y1_ref[0, rows, :].astype(F32)


def _proj_kernel(*refs, tm, rs, tn, n_ctx, act_cols, pending):
    if pending:
        x_ref, y0_ref, y1_ref, gw_ref, pmod_ref, g_ref, mod_ref, w_ref, o_ref, xo_ref = refs
    else:
        x_ref, g_ref, mod_ref, w_ref, o_ref = refs
    i = pl.program_id(1)
    n = w_ref.shape[1]
    for s in range(tm // rs):
        rows = slice(s * rs, (s + 1) * rs)
        is_ctx = _is_ctx(i * tm + s * rs, rs, n_ctx)
        x = x_ref[0, rows, :]
        if pending:
            x = x + _mod_row(pmod_ref, is_ctx, 5) * _ffn_sum(y0_ref, y1_ref, gw_ref, rows)
            xo_ref[0, rows, :] = x
        h = _norm_mod(x, g_ref[...], mod_ref, is_ctx, 0).astype(BF16)
        for c in range(n // tn):
            cols = slice(c * tn, (c + 1) * tn)
            y = _dot(h, w_ref[:, cols])
            if c * tn < act_cols:
                y = jax.nn.gelu(y)
            o_ref[0, rows, cols] = y.astype(o_ref.dtype)


def _norm_proj(t, g, mod, w, n_ctx, act_cols=0, pending=None):
    b, p, d = t.shape
    n = w.shape[1]
    tm = _pick(p, (768, 512, 256))
    rs = _pick(tm, (256, 128))
    tn = _pick(n, (1024, 512))
    assert act_cols % tn == 0
    kern = functools.partial(_proj_kernel, tm=tm, rs=rs, tn=tn, n_ctx=n_ctx, act_cols=act_cols,
                             pending=pending is not None)
    row_spec = pl.BlockSpec((1, tm, d), lambda bi, i: (bi, i, 0))
    mod_spec = pl.BlockSpec((1, 2 * N_MOD, d), lambda bi, i: (bi, 0, 0))
    in_specs = [row_spec]
    args = [t]
    if pending is not None:
        in_specs += [row_spec, row_spec, pl.BlockSpec((1, tm, TOP_K), lambda bi, i: (bi, i, 0)), mod_spec]
        args += [pending[0], pending[1], pending[2], pending[3]]
    in_specs += [pl.BlockSpec((1, d), lambda bi, i: (0, 0)), mod_spec,
                 pl.BlockSpec((d, n), lambda bi, i: (0, 0), pipeline_mode=pl.Buffered(1))]
    args += [g.reshape(1, d), mod, w]
    out_specs = [pl.BlockSpec((1, tm, n), lambda bi, i: (bi, i, 0))]
    out_shape = [jax.ShapeDtypeStruct((b, p, n), BF16)]
    if pending is not None:
        out_specs.append(row_spec)
        out_shape.append(jax.ShapeDtypeStruct((b, p, d), F32))
    res = pl.pallas_call(
        kern,
        grid=(b, p // tm),
        in_specs=in_specs,
        out_specs=out_specs,
        out_shape=out_shape,
        compiler_params=_params(("parallel", "parallel")),
        name="norm_proj",
    )(*args)
    return (res[0], res[1]) if pending is not None else (res[0], t)


def _out_kernel(a_ref, w_ref, x_ref, mod_ref, g_ref, rw_ref, xo_ref, h_ref, p_ref, *, tm, rs, n_ctx):
    for s in range(tm // rs):
        rows = slice(s * rs, (s + 1) * rs)
        is_ctx = _is_ctx(pl.program_id(1) * tm + s * rs, rs, n_ctx)
        y = _dot(a_ref[0, rows, :], w_ref[...])
        xn = x_ref[0, rows, :] + _mod_row(mod_ref, is_ctx, 2) * y
        xo_ref[0, rows, :] = xn
        h = _norm_mod(xn, g_ref[...], mod_ref, is_ctx, 3)
        hi = h.astype(BF16)
        h_ref[0, rows, :] = hi
        lo = (h - hi.astype(F32)).astype(BF16)
        p_ref[0, rows, :] = _dot(hi, rw_ref[...]) + _dot(lo, rw_ref[...])


def _out_proj(a, w, t, mod, g, rw, n_ctx):
    b, p, d = t.shape
    k = a.shape[-1]
    tm = _pick(p, (768, 512, 256))
    kern = functools.partial(_out_kernel, tm=tm, rs=_pick(tm, (256, 128)), n_ctx=n_ctx)
    return pl.pallas_call(
        kern,
        grid=(b, p // tm),
        in_specs=[pl.BlockSpec((1, tm, k), lambda bi, i: (bi, i, 0)),
                  pl.BlockSpec((k, d), lambda bi, i: (0, 0)),
                  pl.BlockSpec((1, tm, d), lambda bi, i: (bi, i, 0)),
                  pl.BlockSpec((1, 2 * N_MOD, d), lambda bi, i: (bi, 0, 0)),
                  pl.BlockSpec((1, d), lambda bi, i: (0, 0)),
                  pl.BlockSpec((d, ROUTER_LANES), lambda bi, i: (0, 0))],
        out_specs=[pl.BlockSpec((1, tm, d), lambda bi, i: (bi, i, 0)),
                   pl.BlockSpec((1, tm, d), lambda bi, i: (bi, i, 0)),
                   pl.BlockSpec((1, tm, ROUTER_LANES), lambda bi, i: (bi, i, 0))],
        out_shape=[jax.ShapeDtypeStruct((b, p, d), F32),
                   jax.ShapeDtypeStruct((b, p, d), BF16),
                   jax.ShapeDtypeStruct((b, p, ROUTER_LANES), F32)],
        compiler_params=_params(("parallel", "parallel")),
        name="out_proj",
    )(a, w, t, mod, g.reshape(1, d), rw)


def _lru_kernel(gt_ref, up_ref, cw_ref, cb_ref, wg_ref, gb_ref, lam_ref, o_ref,
                upad, u_s, hf_s, hb_s, af, bf, ab, bb, *, n_ctx, tt, nt):
    p = nt * tt
    tc = o_ref.shape[-1]
    pad = 8
    upad[0:pad, :] = jnp.zeros((pad, tc), F32)
    upad[pad + p:pad + p + pad, :] = jnp.zeros((pad, tc), F32)
    for t in range(nt):
        upad[pad + t * tt:pad + (t + 1) * tt, :] = up_ref[0, t * tt:(t + 1) * tt, :].astype(F32)

    neg_lam = -lam_ref[...]
    sp = jnp.maximum(neg_lam, 0.0) + jnp.log1p(jnp.exp(-jnp.abs(neg_lam)))
    rowi = lax.broadcasted_iota(jnp.int32, (tt, 1), 0)

    def conv_tile(t):
        r0 = t * tt
        first = r0 in (0, n_ctx)
        last = r0 + tt in (n_ctx, p)
        acc = jnp.broadcast_to(cb_ref[...], (tt, tc))
        for kk in range(CONV_W):
            d = kk - CONV_LEFT
            xs = upad[pad + r0 + d:pad + r0 + d + tt, :]
            if first and d < 0:
                xs = jnp.where(rowi >= -d, xs, 0.0)
            if last and d > 0:
                xs = jnp.where(rowi < tt - d, xs, 0.0)
            acc = acc + cw_ref[kk:kk + 1, :] * xs
        return acc

    half_c_sp = (0.5 * LRU_C) * sp

    def gates(u, d, a_ref, b_ref):
        ub = u.astype(BF16)
        for n in range(tc // LANES):
            sl = slice(n * LANES, (n + 1) * LANES)
            ri = _dot(ub[:, sl], wg_ref[d, n])
            t_r = jnp.tanh(ri[:, :LANES] + gb_ref[d, 0:1, sl])
            t_i = jnp.tanh(ri[:, LANES:] + gb_ref[d, 1:2, sl])
            c = half_c_sp[d:d + 1, sl]
            neg_log_a = c * t_r + c
            a = jnp.exp(-neg_log_a)
            one_minus_a2 = jnp.tanh(neg_log_a) * (a * a + 1.0)
            root = one_minus_a2 * lax.rsqrt(jnp.maximum(one_minus_a2, MIN_NORMAL))
            a_ref[:, sl] = a
            b_ref[:, sl] = root * ((t_i + 1.0) * (0.5 * u[:, sl]))

    nz = n_ctx // tt
    fwd_order = list(range(nt))
    bwd_order = list(range(nz - 1, -1, -1)) + list(range(nt - 1, nz - 1, -1))
    for t in range(nt):
        u_s[t * tt:(t + 1) * tt, :] = conv_tile(t)

    sub = 8
    row8 = lax.broadcasted_iota(jnp.int32, (sub, 1), 0)

    def group_scan(a, b, h_in, reverse):
        for s in (1, 2, 4):
            keep = row8 < sub - s if reverse else row8 >= s
            shift = sub - s if reverse else s
            a_prev = jnp.where(keep, pltpu.roll(a, shift, 0), 1.0)
            b_prev = jnp.where(keep, pltpu.roll(b, shift, 0), 0.0)
            b = a * b_prev + b
            a = a * a_prev
        h = a * h_in + b
        last = h[0:1, :] if reverse else h[sub - 1:sub, :]
        return h, jnp.broadcast_to(last, h.shape)

    carry = (jnp.zeros((sub, tc), F32), jnp.zeros((sub, tc), F32))
    for tf, tb in zip(fwd_order, bwd_order):
        gates(u_s[tf * tt:(tf + 1) * tt, :], 0, af, bf)
        gates(u_s[tb * tt:(tb + 1) * tt, :], 1, ab, bb)

        def step(jj, c, tf=tf, tb=tb):
            f0 = pl.multiple_of(jj * sub, sub)
            b0 = pl.multiple_of(tt - sub - jj * sub, sub)
            h_f, hf = group_scan(af[pl.ds(f0, sub), :], bf[pl.ds(f0, sub), :], c[0], False)
            h_b, hb = group_scan(ab[pl.ds(b0, sub), :], bb[pl.ds(b0, sub), :], c[1], True)
            hf_s[pl.ds(tf * tt + f0, sub), :] = h_f
            hb_s[pl.ds(tb * tt + b0, sub), :] = h_b
            return hf, hb

        carry = lax.fori_loop(0, tt // sub, step, carry)

    for t in range(nt):
        rs = slice(t * tt, (t + 1) * tt)
        o_ref[0, rs, :] = ((hf_s[rs, :] + hb_s[rs, :]) * gt_ref[0, rs, :].astype(F32)).astype(o_ref.dtype)


def _lru(proj, conv_w, conv_b, gate_w, gate_b, lam, n_ctx):
    b, p, d2 = proj.shape
    d = d2 // 2
    tc = 512
    tt = SCAN_TILE
    assert p % tt == 0 and n_ctx % tt == 0 and d % tc == 0 and d // LRU_BLOCKS == LANES
    nt = p // tt
    nct = d // tc
    wg = (0.5 * jnp.concatenate([gate_w[:, 0], gate_w[:, 1]], axis=-1)).astype(BF16)
    gate_b = 0.5 * gate_b
    kern = functools.partial(_lru_kernel, n_ctx=n_ctx, tt=tt, nt=nt)
    return pl.pallas_call(
        kern,
        grid=(b, nct),
        in_specs=[pl.BlockSpec((1, p, tc), lambda bi, ci: (bi, 0, ci)),
                  pl.BlockSpec((1, p, tc), lambda bi, ci: (bi, 0, nct + ci)),
                  pl.BlockSpec((CONV_W, tc), lambda bi, ci: (0, ci)),
                  pl.BlockSpec((1, tc), lambda bi, ci: (0, ci)),
                  pl.BlockSpec((2, tc // LANES, LANES, 2 * LANES), lambda bi, ci: (0, ci, 0, 0)),
                  pl.BlockSpec((2, 2, tc), lambda bi, ci: (0, 0, ci)),
                  pl.BlockSpec((2, tc), lambda bi, ci: (0, ci))],
        out_specs=pl.BlockSpec((1, p, tc), lambda bi, ci: (bi, 0, ci)),
        out_shape=jax.ShapeDtypeStruct((b, p, d), BF16),
        scratch_shapes=[pltpu.VMEM((p + 16, tc), F32), pltpu.VMEM((p, tc), F32),
                        pltpu.VMEM((p, tc), F32), pltpu.VMEM((p, tc), F32),
                        pltpu.VMEM((tt, tc), F32), pltpu.VMEM((tt, tc), F32),
                        pltpu.VMEM((tt, tc), F32), pltpu.VMEM((tt, tc), F32)],
        compiler_params=_params(("parallel", "parallel")),
        name="rglru",
    )(proj, proj, conv_w, conv_b.reshape(1, d), wg, gate_b, lam)


def _na_window(rb, rows):
    win = NA_QROWS + NA_ROWS - 1
    return min(max(NA_QROWS * rb - NA_ROWS // 2, 0), rows - win)


def _na_bias(rpb, rows):
    nb = rows // NA_QROWS
    win = NA_QROWS + NA_ROWS - 1
    kr = min(NA_ROWS, rows)
    n_rel_r, n_rel_c = 2 * NA_ROWS - 1, 2 * NA_COLS - 1
    col = np.arange(GRID_W)
    c0 = np.clip(col - NA_COLS // 2, 0, GRID_W - NA_COLS)
    valid_c = (col[None, :] >= c0[:, None]) & (col[None, :] < c0[:, None] + NA_COLS)
    rel_c = np.clip(col[None, :] - col[:, None] + NA_COLS - 1, 0, n_rel_c - 1)
    pick_c = (rel_c[..., None] == np.arange(n_rel_c)).astype(np.float32)
    pick_r, valid = [], []
    for rb in (0, 1, nb - 1):
        r_abs = NA_QROWS * rb + np.arange(NA_QROWS)
        r0 = np.clip(r_abs - kr // 2, 0, rows - kr)
        k_abs = _na_window(rb, rows) + np.arange(win)
        valid_r = (k_abs[None, :] >= r0[:, None]) & (k_abs[None, :] < r0[:, None] + kr)
        rel_r = np.clip(k_abs[None, :] - r_abs[:, None] + NA_ROWS - 1, 0, n_rel_r - 1)
        pick_r.append((rel_r[..., None] == np.arange(n_rel_r)).astype(np.float32))
        valid.append(valid_r[:, None, :, None] & valid_c[None, :, None, :])
    rpb2 = rpb.reshape(rpb.shape[0] // 2, 2, n_rel_r, n_rel_c)
    rows_sel = jnp.einsum('peab,tqwa->pteqwb', rpb2, jnp.asarray(np.stack(pick_r)), precision=lax.Precision.HIGHEST)
    bias = jnp.einsum('pteqwb,ckb->pteqcwk', rows_sel, jnp.asarray(pick_c), precision=lax.Precision.HIGHEST)
    bias = jnp.where(jnp.asarray(np.stack(valid))[None, :, None], bias, NEG_INF)
    return bias.reshape(rpb.shape[0] // 2, 3, 2 * NA_QROWS * GRID_W, win * GRID_W)


def _na_kernel(q_ref, k_ref, v_ref, bias_ref, o_ref, vx_ref, *, n_ctx, rows):
    hd = LANES // 2
    scale = hd ** -0.5
    nq = NA_QROWS * GRID_W
    nb = rows // NA_QROWS
    nk = (NA_QROWS + NA_ROWS - 1) * GRID_W
    first = lax.broadcasted_iota(jnp.int32, (1, LANES), 1) < hd
    vx_ref[:, 0:LANES] = v_ref[0]
    vx_ref[:, LANES:2 * LANES] = jnp.ones((vx_ref.shape[0], LANES), BF16)

    def attend(qb, ks, bias_ty):
        n = qb.shape[0]
        qb = qb * scale
        zero = jnp.zeros_like(qb)
        q2 = jnp.concatenate([jnp.where(first, qb, zero), jnp.where(first, zero, qb)], axis=0)
        s_ctx = _dot_nt(q2, k_ref[0, 0:n_ctx, :])
        m = jnp.max(s_ctx, axis=-1, keepdims=True)
        if ks is not None:
            s_loc = _dot_nt(q2, k_ref[0, ks:ks + nk, :]) + bias_ref[0, bias_ty]
            m = jnp.maximum(m, jnp.max(s_loc, axis=-1, keepdims=True))
        o = _dot(jnp.exp(s_ctx - m).astype(BF16), vx_ref[0:n_ctx, :])
        if ks is not None:
            o = o + _dot(jnp.exp(s_loc - m).astype(BF16), vx_ref[ks:ks + nk, :])
        o = o[:, 0:LANES] * (1.0 / o[:, LANES:2 * LANES])
        return jnp.where(first, o[0:n], o[n:2 * n])

    o_ref[0, 0:n_ctx, :] = attend(q_ref[0, 0:n_ctx, :], None, None).astype(o_ref.dtype)
    for rb in range(nb):
        qs = n_ctx + rb * nq
        ks = n_ctx + _na_window(rb, rows) * GRID_W
        ty = 0 if rb == 0 else (2 if rb == nb - 1 else 1)
        o_ref[0, qs:qs + nq, :] = attend(q_ref[0, qs:qs + nq, :], ks, ty).astype(o_ref.dtype)


def _na(qkv, rpb, n_ctx):
    b, p, d3 = qkv.shape
    d = d3 // 3
    rows = (p - n_ctx) // GRID_W
    assert d // NA_HEADS == LANES // 2 and rows % NA_QROWS == 0 and rows // NA_QROWS >= 3
    nhp = d // LANES
    bias = _na_bias(rpb.astype(F32), rows)
    nq, nk = bias.shape[2] // 2, bias.shape[3]
    kern = functools.partial(_na_kernel, n_ctx=n_ctx, rows=rows)
    return pl.pallas_call(
        kern,
        grid=(nhp, b),
        in_specs=[pl.BlockSpec((1, p, LANES), lambda hp, bi: (bi, 0, hp)),
                  pl.BlockSpec((1, p, LANES), lambda hp, bi: (bi, 0, nhp + hp)),
                  pl.BlockSpec((1, p, LANES), lambda hp, bi: (bi, 0, 2 * nhp + hp)),
                  pl.BlockSpec((1, 3, 2 * nq, nk), lambda hp, bi: (hp, 0, 0, 0))],
        out_specs=pl.BlockSpec((1, p, LANES), lambda hp, bi: (bi, 0, hp)),
        out_shape=jax.ShapeDtypeStruct((b, p, d), BF16),
        scratch_shapes=[pltpu.VMEM((p, 2 * LANES), BF16)],
        compiler_params=_params(("parallel", "parallel")),
        name="nbr_attn",
    )(qkv, qkv, qkv, bias)


def _ret_tables(n_ctx, s, dk, ch):
    quarter = dk // 4
    pos = jnp.arange(s)
    inv = ROPE_BASE ** (-jnp.arange(quarter, dtype=F32) / quarter)
    ang_r = (pos // GRID_W).astype(F32)[:, None] * inv
    ang_c = (pos % GRID_W).astype(F32)[:, None] * inv
    cos = jnp.concatenate([jnp.cos(ang_r)] * 2 + [jnp.cos(ang_c)] * 2, axis=-1)
    sin = jnp.concatenate([-jnp.sin(ang_r), jnp.sin(ang_r), -jnp.sin(ang_c), jnp.sin(ang_c)], axis=-1)
    cos = jnp.concatenate([jnp.ones((n_ctx, dk), F32), cos], axis=0)
    sin = jnp.concatenate([jnp.zeros((n_ctx, dk), F32), sin], axis=0)
    log_gamma = jnp.log1p(-(2.0 ** (-5.0 - jnp.arange(RET_HEADS, dtype=F32))))[:, None, None]
    pq = jnp.arange(ch, dtype=F32)
    col = jnp.broadcast_to(pq[:, None], (ch, dk))[None]
    tabs = jnp.stack([
        jnp.exp(jnp.abs(pq[:, None] - pq[None, :])[None] * log_gamma),
        jnp.exp((col + 1.0) * log_gamma),
        jnp.exp((ch - col) * log_gamma),
        jnp.exp((ch - 1.0 - col) * log_gamma),
        jnp.exp(col * log_gamma),
    ], axis=1)
    chunk_decay = jnp.exp(ch * log_gamma[:, 0, 0])
    return cos, sin, tabs, chunk_decay


def _ret_kernel(cd_ref, q_ref, k_ref, v_ref, g_ref, cos_ref, sin_ref, tab_ref, o_ref,
                sb_ref, st_ref, *, ch, nc):
    dk = q_ref.shape[-1]
    cd = cd_ref[pl.program_id(1)]
    k_scale = dk ** -0.5

    def rope(t, c0):
        parts = []
        for hf in range(dk // LANES):
            sl = slice(hf * LANES, (hf + 1) * LANES)
            th = t[:, sl]
            parts.append(th * cos_ref[pl.ds(c0, ch), sl] + pltpu.roll(th, LANES // 2, 1) * sin_ref[pl.ds(c0, ch), sl])
        return jnp.concatenate(parts, axis=-1)

    def load_k(c0):
        return rope(k_ref[0, pl.ds(c0, ch), :].astype(F32) * k_scale, c0)

    def kv_outer(kdec, c0):
        return _dot_tn(kdec.astype(BF16), v_ref[0, pl.ds(c0, ch), :])

    nz = 1
    sb_ref[0] = jnp.zeros(sb_ref.shape[1:], BF16)
    st_ref[...] = kv_outer(load_k(0) * tab_ref[0, 4], 0)

    def bwd(j, _):
        c = nc - 1 - j
        c0 = pl.multiple_of(c * ch, ch)
        sb_ref[c] = st_ref[...].astype(BF16)
        st_ref[...] = cd * st_ref[...] + kv_outer(load_k(c0) * tab_ref[0, 4], c0)
        return 0

    lax.fori_loop(0, nc - 1 - nz, bwd, 0, unroll=True)
    sb_ref[nz] = st_ref[...].astype(BF16)

    st_ref[...] = jnp.zeros(st_ref.shape, F32)

    def fwd(c, _):
        c0 = pl.multiple_of(c * ch, ch)
        q = rope(q_ref[0, pl.ds(c0, ch), :].astype(F32), c0)
        k = load_k(c0)
        v = v_ref[0, pl.ds(c0, ch), :]
        inner = _dot_nt(q.astype(BF16), k.astype(BF16)) * tab_ref[0, 0]
        o = (_dot(inner.astype(BF16), v)
             + _dot((q * tab_ref[0, 1]).astype(BF16), st_ref[...].astype(BF16))
             + _dot((q * tab_ref[0, 2]).astype(BF16), sb_ref[c]))
        st_ref[...] = cd * st_ref[...] + kv_outer(k * tab_ref[0, 3], c0)
        o = o * lax.rsqrt(jnp.mean(o * o, axis=-1, keepdims=True) + RMS_EPS)
        g = g_ref[0, pl.ds(c0, ch), :].astype(F32)
        o_ref[0, pl.ds(c0, ch), :] = (o * (g * jax.nn.sigmoid(g))).astype(o_ref.dtype)
        return 0

    lax.fori_loop(0, nc, fwd, 0, unroll=3 if nc % 3 == 0 else 1)


def _ret(qkvg, n_ctx):
    b, p, d6 = qkvg.shape
    d = d6 // 6
    dk = d // RET_HEADS
    dv = 2 * dk
    ch = dk
    assert dk == 2 * LANES and n_ctx == ch and p % ch == 0
    nc = p // ch
    cos, sin, tabs, chunk_decay = _ret_tables(n_ctx, p - n_ctx, dk, ch)
    kern = functools.partial(_ret_kernel, ch=ch, nc=nc)
    nh = RET_HEADS
    return pl.pallas_call(
        kern,
        grid_spec=pltpu.PrefetchScalarGridSpec(
            num_scalar_prefetch=1,
            grid=(b, nh),
            in_specs=[pl.BlockSpec((1, p, dk), lambda bi, h, cd: (bi, 0, h)),
                      pl.BlockSpec((1, p, dk), lambda bi, h, cd: (bi, 0, nh + h)),
                      pl.BlockSpec((1, p, dv), lambda bi, h, cd: (bi, 0, nh + h)),
                      pl.BlockSpec((1, p, dv), lambda bi, h, cd: (bi, 0, 2 * nh + h)),
                      pl.BlockSpec((p, dk), lambda bi, h, cd: (0, 0)),
                      pl.BlockSpec((p, dk), lambda bi, h, cd: (0, 0)),
                      pl.BlockSpec((1, 5, ch, dk), lambda bi, h, cd: (h, 0, 0, 0))],
            out_specs=pl.BlockSpec((1, p, dv), lambda bi, h, cd: (bi, 0, h)),
            scratch_shapes=[pltpu.VMEM((nc, dk, dv), BF16), pltpu.VMEM((dk, dv), F32)]),
        out_shape=jax.ShapeDtypeStruct((b, p, nh * dv), BF16),
        compiler_params=_params(("parallel", "parallel")),
        name="retention",
    )(chunk_decay, qkvg, qkvg, qkvg, qkvg, cos, sin, tabs)


def _router_kernel(lg_ref, rb_ref, tri_ref, e_ref, w_ref, rk_ref, cnt_ref, run_ref):
    @pl.when(pl.program_id(0) == 0)
    def _():
        run_ref[...] = jnp.zeros(run_ref.shape, F32)

    lg = lg_ref[...]
    ex = jnp.exp(lg - jnp.max(lg, axis=0, keepdims=True))
    probs = ex / jnp.sum(ex, axis=0, keepdims=True)
    sel = probs + rb_ref[...]
    epg = EXPERTS_PER_GROUP
    best = grp = cur = curp = None
    for g in range(N_GROUPS):
        s = [sel[g * epg + i:g * epg + i + 1, :] for i in range(epg)]
        pr = [probs[g * epg + i:g * epg + i + 1, :] for i in range(epg)]
        top2 = None
        for i in range(epg):
            for j in range(i + 1, epg):
                top2 = s[i] + s[j] if top2 is None else jnp.maximum(top2, s[i] + s[j])
        if g == 0:
            best, grp, cur, curp = top2, jnp.zeros(top2.shape, jnp.int32), s, pr
        else:
            better = top2 > best
            best = jnp.where(better, top2, best)
            grp = jnp.where(better, g, grp)
            cur = [jnp.where(better, s[i], cur[i]) for i in range(epg)]
            curp = [jnp.where(better, pr[i], curp[i]) for i in range(epg)]
    b1, i1, p1 = cur[0], jnp.zeros(best.shape, jnp.int32), curp[0]
    for i in range(1, epg):
        gt = cur[i] > b1
        b1, i1, p1 = jnp.where(gt, cur[i], b1), jnp.where(gt, i, i1), jnp.where(gt, curp[i], p1)
    b2 = i2 = p2 = None
    for i in range(epg):
        v = jnp.where(i1 == i, -jnp.inf, cur[i])
        if b2 is None:
            b2, i2, p2 = v, jnp.zeros(best.shape, jnp.int32), curp[0]
        else:
            gt = v > b2
            b2, i2, p2 = jnp.where(gt, v, b2), jnp.where(gt, i, i2), jnp.where(gt, curp[i], p2)
    e1 = grp * epg + i1
    e2 = grp * epg + i2
    inv = 1.0 / (p1 + p2)
    e_ref[0:1, :] = e1
    e_ref[1:2, :] = e2
    w_ref[0:1, :] = p1 * inv
    w_ref[1:2, :] = p2 * inv

    eidx = lax.broadcasted_iota(jnp.int32, lg.shape, 0)
    eq1 = eidx == e1
    eq2 = eidx == e2
    member = jnp.where(eq1, 1.0, jnp.where(eq2, 1.0, 0.0))
    before = _dot(member.astype(BF16), tri_ref[...]) + run_ref[...]
    rk_ref[0:1, :] = jnp.sum(jnp.where(eq1, before, 0.0), axis=0, keepdims=True).astype(jnp.int32)
    rk_ref[1:2, :] = jnp.sum(jnp.where(eq2, before, 0.0), axis=0, keepdims=True).astype(jnp.int32)
    run_ref[...] = run_ref[...] + jnp.sum(member, axis=1, keepdims=True)
    cnt_ref[...] = run_ref[...]


def _router(logits_t, router_b):
    e, n = logits_t.shape
    tt = 512
    assert n % tt == 0
    tri = (jnp.arange(tt)[:, None] < jnp.arange(tt)[None, :]).astype(BF16)
    kn = jax.ShapeDtypeStruct((TOP_K, n), jnp.int32)
    return pl.pallas_call(
        _router_kernel,
        grid=(n // tt,),
        in_specs=[pl.BlockSpec((e, tt), lambda i: (0, i)),
                  pl.BlockSpec((e, 1), lambda i: (0, 0)),
                  pl.BlockSpec((tt, tt), lambda i: (0, 0))],
        out_specs=[pl.BlockSpec((TOP_K, tt), lambda i: (0, i)),
                   pl.BlockSpec((TOP_K, tt), lambda i: (0, i)),
                   pl.BlockSpec((TOP_K, tt), lambda i: (0, i)),
                   pl.BlockSpec((e, 1), lambda i: (0, 0))],
        out_shape=[kn, jax.ShapeDtypeStruct((TOP_K, n), F32), kn, jax.ShapeDtypeStruct((e, 1), F32)],
        scratch_shapes=[pltpu.VMEM((e, 1), F32)],
        compiler_params=_params(("arbitrary",)),
        name="router",
    )(logits_t, router_b.astype(F32).reshape(e, 1), tri)


def _moe_kernel(be_ref, nu_ref, xs_ref, wg_ref, wu_ref, wd_ref, ys_ref, wgb, wub, wdb, *, fc):
    i = pl.program_id(0)
    used = i < nu_ref[0]
    new_expert = jnp.logical_or(i == 0, be_ref[i] != be_ref[jnp.maximum(i - 1, 0)])

    @pl.when(jnp.logical_and(used, new_expert))
    def _():
        wgb[...] = wg_ref[0, 0].astype(BF16)
        wub[...] = wu_ref[0, 0].astype(BF16)
        wdb[...] = wd_ref[0, 0].astype(BF16)

    @pl.when(used)
    def _():
        x = xs_ref[...]
        f = wgb.shape[-1]
        acc = jnp.zeros(ys_ref.shape, F32)
        for c in range(f // fc):
            sl = slice(c * fc, (c + 1) * fc)
            g = _dot(x, wgb[:, sl])
            u = _dot(x, wub[:, sl])
            a = (g * jax.nn.sigmoid(g) * u).astype(BF16)
            acc = acc + _dot(a, wdb[sl, :])
        ys_ref[...] = acc.astype(ys_ref.dtype)

    @pl.when(jnp.logical_not(used))
    def _():
        ys_ref[...] = jnp.zeros(ys_ref.shape, ys_ref.dtype)


def _moe_experts(xs, block_expert, n_used, w_gate, w_up, w_down, layer):
    n_rows, d = xs.shape
    f = w_gate.shape[-1]
    bm = MOE_BLOCK
    kern = functools.partial(_moe_kernel, fc=_pick(f, (512, 256, 128)))
    return pl.pallas_call(
        kern,
        grid_spec=pltpu.PrefetchScalarGridSpec(
            num_scalar_prefetch=2,
            grid=(n_rows // bm,),
            in_specs=[pl.BlockSpec((bm, d), lambda i, be, nu: (i, 0)),
                      pl.BlockSpec((1, 1, d, f), lambda i, be, nu: (layer, be[i], 0, 0)),
                      pl.BlockSpec((1, 1, d, f), lambda i, be, nu: (layer, be[i], 0, 0)),
                      pl.BlockSpec((1, 1, f, d), lambda i, be, nu: (layer, be[i], 0, 0))],
            out_specs=pl.BlockSpec((bm, d), lambda i, be, nu: (i, 0)),
            scratch_shapes=[pltpu.VMEM((d, f), BF16), pltpu.VMEM((d, f), BF16), pltpu.VMEM((f, d), BF16)]),
        out_shape=jax.ShapeDtypeStruct((n_rows, d), BF16),
        compiler_params=_params(("arbitrary",)),
        name="moe_experts",
    )(block_expert, n_used, xs, w_gate, w_up, w_down)


def _moe(h_rows, tok_ids, logits_t, router_b, w_gate, w_up, w_down, layer):
    n = tok_ids.shape[0]
    bm = MOE_BLOCK
    expert, weight, rank, counts = _router(logits_t, router_b)
    counts = counts[:, 0].astype(jnp.int32)
    padded = (counts + bm - 1) // bm * bm
    pad_end = jnp.cumsum(padded)
    pad_start = pad_end - padded
    dest = rank + jnp.sum(jnp.where(expert[..., None] == jnp.arange(N_EXPERTS), pad_start, 0), axis=-1)
    n_blocks = -(-(n * TOP_K) // bm) + N_EXPERTS
    n_rows = n_blocks * bm
    tok = tok_ids.astype(jnp.int32)
    src = (jnp.arange(n_rows, dtype=jnp.int32) % h_rows.shape[0]).at[dest.reshape(-1)].set(
        jnp.tile(tok, TOP_K), unique_indices=True, mode='promise_in_bounds')
    block_start = jnp.arange(n_blocks, dtype=jnp.int32) * bm
    block_expert = jnp.minimum(
        jnp.sum((pad_end[None, :] <= block_start[:, None]).astype(jnp.int32), axis=1), N_EXPERTS - 1)
    n_used = (pad_end[-1] // bm).astype(jnp.int32).reshape(1)
    ys = _moe_experts(h_rows[src], block_expert, n_used, w_gate, w_up, w_down, layer)
    return [ys[dest[k]] for k in range(TOP_K)], weight


def _final_kernel(x_ref, y0_ref, y1_ref, gw_ref, mod_ref, g_ref, o_ref):
    gate = mod_ref[0, 2 * N_MOD - 1:2 * N_MOD, :]
    xn = x_ref[0] + gate * _ffn_sum(y0_ref, y1_ref, gw_ref, slice(None))
    o_ref[0] = xn * lax.rsqrt(jnp.mean(xn * xn, axis=-1, keepdims=True) + RMS_EPS) * g_ref[...]


def _final(t, ys, gw, mod, g, n_ctx):
    b, p, d = t.shape
    rows = p - n_ctx
    tm = _pick(n_ctx, (512, 256))
    assert rows % tm == 0
    off = n_ctx // tm
    row_spec = pl.BlockSpec((1, tm, d), lambda bi, i: (bi, i, 0))
    return pl.pallas_call(
        _final_kernel,
        grid=(b, rows // tm),
        in_specs=[pl.BlockSpec((1, tm, d), lambda bi, i: (bi, i + off, 0)),
                  row_spec, row_spec,
                  pl.BlockSpec((1, tm, TOP_K), lambda bi, i: (bi, i, 0)),
                  pl.BlockSpec((1, 2 * N_MOD, d), lambda bi, i: (bi, 0, 0)),
                  pl.BlockSpec((1, d), lambda bi, i: (0, 0))],
        out_specs=row_spec,
        out_shape=jax.ShapeDtypeStruct((b, rows, d), F32),
        compiler_params=_params(("parallel", "parallel")),
        name="ffn_final",
    )(t, ys[0], ys[1], gw, mod, g.reshape(1, d))


def kernel(x, c, ctx, c_ctx, ada_w, ada_b, norm_mix_g, norm_ffn_g, final_norm_g, lru_w_in, lru_conv_w, lru_conv_b, lru_gate_w, lru_gate_b, lru_lambda, lru_w_out, na_w_qkv, na_rpb, na_w_o, ret_w_qkvg, ret_w_o, router_w, router_b, moe_w_gate, moe_w_up, moe_w_down):
    b, s, d = x.shape
    n_ctx = ctx.shape[1]
    p = n_ctx + s
    depth = ada_w.shape[0]
    t = jnp.concatenate([ctx, x], axis=1)

    pad_rows = -(b + 1) % 16
    cc = jnp.concatenate([c, c_ctx[None, :], jnp.zeros((pad_rows, d), F32)], axis=0)
    mods = _ada(cc, ada_w, ada_b)
    mod_x = mods[:, :b].reshape(depth, b, N_MOD, d)
    mod_z = jnp.broadcast_to(mods[:, b].reshape(depth, 1, N_MOD, d), (depth, b, N_MOD, d))
    mods = jnp.concatenate([mod_z, mod_x], axis=2)

    rw32 = router_w.astype(F32)
    rw_hi = rw32.astype(BF16)
    rw_lo = (rw32 - rw_hi.astype(F32)).astype(BF16)
    rw = jnp.concatenate([rw_hi, rw_lo, jnp.zeros((d, ROUTER_LANES - 2 * N_EXPERTS), BF16)], axis=1)

    all_rows = jnp.arange(b * p, dtype=jnp.int32)
    pending = None
    for i in range(depth):
        kind, j = i % 3, i // 3
        last = i == depth - 1
        mod = _at(mods, i)
        if kind == 0:
            proj, t = _norm_proj(t, _at(norm_mix_g, i), mod, _at(lru_w_in, j).astype(BF16), n_ctx,
                                 act_cols=d, pending=pending)
            a = _lru(proj, _at(lru_conv_w, j), _at(lru_conv_b, j), _at(lru_gate_w, j), _at(lru_gate_b, j),
                     _at(lru_lambda, j), n_ctx)
            w_o = _at(lru_w_out, j)
        elif kind == 1:
            qkv, t = _norm_proj(t, _at(norm_mix_g, i), mod, _at(na_w_qkv, j).astype(BF16), n_ctx, pending=pending)
            a = _na(qkv, _at(na_rpb, j), n_ctx)
            w_o = _at(na_w_o, j)
        else:
            qkvg, t = _norm_proj(t, _at(norm_mix_g, i), mod, _at(ret_w_qkvg, j).astype(BF16), n_ctx, pending=pending)
            a = _ret(qkvg, n_ctx)
            w_o = _at(ret_w_o, j)
        t, h2, pl_ = _out_proj(a, w_o.astype(BF16), t, mod, _at(norm_ffn_g, i), rw, n_ctx)
        logits = (pl_[..., :N_EXPERTS] + pl_[..., N_EXPERTS:2 * N_EXPERTS])
        if last:
            tok_ids = all_rows.reshape(b, p)[:, n_ctx:].reshape(-1)
            logits = logits[:, n_ctx:]
        else:
            tok_ids = all_rows
        ys, wts = _moe(h2.reshape(b * p, d), tok_ids, logits.reshape(-1, N_EXPERTS).T, router_b,
                       moe_w_gate, moe_w_up, moe_w_down, i)
        ys = [y.reshape(b, -1, d) for y in ys]
        wts = wts.T.reshape(b, -1, TOP_K)
        pending = (ys[0], ys[1], wts, mod)
    return _final(t, ys, wts, mod, final_norm_g, n_ctx)
```

```python
import functools

import jax
import jax.numpy as jnp
import numpy as np
from jax import lax
from jax.experimental import pallas as pl
from jax.experimental.pallas import tpu as pltpu

F32 = jnp.float32
BF16 = jnp.bfloat16

GRID_W = 64
N_MOD = 6
RMS_EPS = 1e-6
MIN_NORMAL = float(np.finfo(np.float32).tiny)
LRU_BLOCKS = 8
CONV_W = 4
CONV_LEFT = CONV_W // 2
LRU_C = 8.0
NA_HEADS = 16
NA_ROWS = 8
NA_COLS = 16
NA_QROWS = 4
NEG_INF = -1e30
RET_HEADS = 4
ROPE_BASE = 10000.0
N_EXPERTS = 16
N_GROUPS = 4
EXPERTS_PER_GROUP = N_EXPERTS // N_GROUPS
TOP_K = 2
MOE_BLOCK = 512
LANES = 128
ROUTER_LANES = 128
SCAN_TILE = 256
GATE_ROWS = 128
CONV_ROWS = 64
VMEM_LIMIT = 56 * 1024 * 1024


def _pick(n, candidates):
    for c in candidates:
        if n % c == 0:
            return c
    raise ValueError(f"no tile in {candidates} divides {n}")


def _params(sem):
    return pltpu.CompilerParams(dimension_semantics=sem, vmem_limit_bytes=VMEM_LIMIT)


def _at(w, i):
    return lax.index_in_dim(w, i, axis=0, keepdims=False)


def _dot(a, b):
    return jnp.dot(a, b, preferred_element_type=F32)


def _dot_nt(a, b):
    return lax.dot_general(a, b, (((1,), (1,)), ((), ())), preferred_element_type=F32)


def _dot_tn(a, b):
    return lax.dot_general(a, b, (((0,), (0,)), ((), ())), preferred_element_type=F32)


def _is_ctx(pos0, tm, n_ctx):
    return (pos0 + lax.broadcasted_iota(jnp.int32, (tm, 1), 0)) < n_ctx


def _mod_row(mod_ref, is_ctx, k):
    return jnp.where(is_ctx, mod_ref[0, k:k + 1, :], mod_ref[0, N_MOD + k:N_MOD + k + 1, :])


def _norm_mod(x, g, mod_ref, is_ctx, k):
    y = x * lax.rsqrt(jnp.mean(x * x, axis=-1, keepdims=True) + RMS_EPS) * g
    return y * (1.0 + _mod_row(mod_ref, is_ctx, k + 1)) + _mod_row(mod_ref, is_ctx, k)


def _ada_kernel(cc_ref, w_ref, b_ref, o_ref):
    cc = cc_ref[...]
    s = (cc * jax.nn.sigmoid(cc)).astype(BF16)
    o_ref[0] = _dot(s, w_ref[0].astype(BF16)) + b_ref[0]


def _ada(cc, ada_w, ada_b):
    depth, d, n = ada_w.shape
    rows = cc.shape[0]
    tn = _pick(n, (1536, 1024, 512, 256, 128))
    return pl.pallas_call(
        _ada_kernel,
        grid=(depth, n // tn),
        in_specs=[pl.BlockSpec((rows, d), lambda i, j: (0, 0)),
                  pl.BlockSpec((1, d, tn), lambda i, j: (i, 0, j)),
                  pl.BlockSpec((1, 1, tn), lambda i, j: (i, 0, j))],
        out_specs=pl.BlockSpec((1, rows, tn), lambda i, j: (i, 0, j)),
        out_shape=jax.ShapeDtypeStruct((depth, rows, n), F32),
        compiler_params=_params(("parallel", "parallel")),
        name="ada_mod",
    )(cc, ada_w, ada_b.reshape(depth, 1, n))


def _ffn_sum(y0_ref, y1_ref, gw_ref, rows):
    gw = gw_ref[0, rows, :]
    return gw[:, 0:1] * y0_ref[0, rows, :].astype(F32) + gw[:, 1:2] * y1_ref[0, rows, :].astype(F32)


def _proj_kernel(*refs, tm, rs, tn, n_ctx, act_cols, pending):
    if pending:
        x_ref, y0_ref, y1_ref, gw_ref, pmod_ref, g_ref, mod_ref, w_ref, o_ref, xo_ref = refs
    else:
        x_ref, g_ref, mod_ref, w_ref, o_ref = refs
    i = pl.program_id(1)
    n = w_ref.shape[1]
    for s in range(tm // rs):
        rows = slice(s * rs, (s + 1) * rs)
        is_ctx = _is_ctx(i * tm + s * rs, rs, n_ctx)
        x = x_ref[0, rows, :]
        if pending:
            x = x + _mod_row(pmod_ref, is_ctx, 5) * _ffn_sum(y0_ref, y1_ref, gw_ref, rows)
            xo_ref[0, rows, :] = x
        h = _norm_mod(x, g_ref[...], mod_ref, is_ctx, 0).astype(BF16)
        for c in range(n // tn):
            cols = slice(c * tn, (c + 1) * tn)
            y = _dot(h, w_ref[:, cols])
            if c * tn < act_cols:
                y = jax.nn.gelu(y)
            o_ref[0, rows, cols] = y.astype(o_ref.dtype)


def _norm_proj(t, g, mod, w, n_ctx, act_cols=0, pending=None):
    b, p, d = t.shape
    n = w.shape[1]
    tm = _pick(p, (768, 512, 256))
    rs = _pick(tm, (256, 128))
    tn = _pick(n, (1024, 512))
    assert act_cols % tn == 0
    kern = functools.partial(_proj_kernel, tm=tm, rs=rs, tn=tn, n_ctx=n_ctx, act_cols=act_cols,
                             pending=pending is not None)
    row_spec = pl.BlockSpec((1, tm, d), lambda bi, i: (bi, i, 0))
    mod_spec = pl.BlockSpec((1, 2 * N_MOD, d), lambda bi, i: (bi, 0, 0))
    in_specs = [row_spec]
    args = [t]
    if pending is not None:
        in_specs += [row_spec, row_spec, pl.BlockSpec((1, tm, TOP_K), lambda bi, i: (bi, i, 0)), mod_spec]
        args += [pending[0], pending[1], pending[2], pending[3]]
    in_specs += [pl.BlockSpec((1, d), lambda bi, i: (0, 0)), mod_spec,
                 pl.BlockSpec((d, n), lambda bi, i: (0, 0), pipeline_mode=pl.Buffered(1))]
    args += [g.reshape(1, d), mod, w]
    out_specs = [pl.BlockSpec((1, tm, n), lambda bi, i: (bi, i, 0))]
    out_shape = [jax.ShapeDtypeStruct((b, p, n), BF16)]
    if pending is not None:
        out_specs.append(row_spec)
        out_shape.append(jax.ShapeDtypeStruct((b, p, d), F32))
    res = pl.pallas_call(
        kern,
        grid=(b, p // tm),
        in_specs=in_specs,
        out_specs=out_specs,
        out_shape=out_shape,
        compiler_params=_params(("parallel", "parallel")),
        name="norm_proj",
    )(*args)
    return (res[0], res[1]) if pending is not None else (res[0], t)


def _out_kernel(a_ref, w_ref, x_ref, mod_ref, g_ref, rw_ref, xo_ref, h_ref, p_ref, *, tm, rs, n_ctx):
    for s in range(tm // rs):
        rows = slice(s * rs, (s + 1) * rs)
        is_ctx = _is_ctx(pl.program_id(1) * tm + s * rs, rs, n_ctx)
        y = _dot(a_ref[0, rows, :], w_ref[...])
        xn = x_ref[0, rows, :] + _mod_row(mod_ref, is_ctx, 2) * y
        xo_ref[0, rows, :] = xn
        h = _norm_mod(xn, g_ref[...], mod_ref, is_ctx, 3)
        hi = h.astype(BF16)
        h_ref[0, rows, :] = hi
        lo = (h - hi.astype(F32)).astype(BF16)
        p_ref[0, rows, :] = _dot(hi, rw_ref[...]) + _dot(lo, rw_ref[...])


def _out_proj(a, w, t, mod, g, rw, n_ctx):
    b, p, d = t.shape
    k = a.shape[-1]
    tm = _pick(p, (768, 512, 256))
    kern = functools.partial(_out_kernel, tm=tm, rs=_pick(tm, (256, 128)), n_ctx=n_ctx)
    return pl.pallas_call(
        kern,
        grid=(b, p // tm),
        in_specs=[pl.BlockSpec((1, tm, k), lambda bi, i: (bi, i, 0)),
                  pl.BlockSpec((k, d), lambda bi, i: (0, 0)),
                  pl.BlockSpec((1, tm, d), lambda bi, i: (bi, i, 0)),
                  pl.BlockSpec((1, 2 * N_MOD, d), lambda bi, i: (bi, 0, 0)),
                  pl.BlockSpec((1, d), lambda bi, i: (0, 0)),
                  pl.BlockSpec((d, ROUTER_LANES), lambda bi, i: (0, 0))],
        out_specs=[pl.BlockSpec((1, tm, d), lambda bi, i: (bi, i, 0)),
                   pl.BlockSpec((1, tm, d), lambda bi, i: (bi, i, 0)),
                   pl.BlockSpec((1, tm, ROUTER_LANES), lambda bi, i: (bi, i, 0))],
        out_shape=[jax.ShapeDtypeStruct((b, p, d), F32),
                   jax.ShapeDtypeStruct((b, p, d), BF16),
                   jax.ShapeDtypeStruct((b, p, ROUTER_LANES), F32)],
        compiler_params=_params(("parallel", "parallel")),
        name="out_proj",
    )(a, w, t, mod, g.reshape(1, d), rw)


def _lru_kernel(gt_ref, up_ref, cw_ref, cb_ref, wg_ref, gb_ref, lam_ref, o_ref,
                upad, u_s, hf_s, hb_s, af, bf, ab, bb, *, n_ctx, tt, nt):
    p = nt * tt
    tc = o_ref.shape[-1]
    pad = 8
    upad[0:pad, :] = jnp.zeros((pad, tc), F32)
    upad[pad + p:pad + p + pad, :] = jnp.zeros((pad, tc), F32)
    for t in range(nt):
        upad[pad + t * tt:pad + (t + 1) * tt, :] = up_ref[0, t * tt:(t + 1) * tt, :].astype(F32)

    neg_lam = -lam_ref[...]
    sp = jnp.maximum(neg_lam, 0.0) + jnp.log1p(jnp.exp(-jnp.abs(neg_lam)))
    rowi = lax.broadcasted_iota(jnp.int32, (tt, 1), 0)

    def conv_rows(r0, nr):
        first = r0 in (0, n_ctx)
        last = r0 + nr in (n_ctx, p)
        acc = jnp.broadcast_to(cb_ref[...], (nr, tc))
        for kk in range(CONV_W):
            d = kk - CONV_LEFT
            xs = upad[pad + r0 + d:pad + r0 + d + nr, :]
            if first and d < 0:
                xs = jnp.where(rowi[0:nr] >= -d, xs, 0.0)
            if last and d > 0:
                xs = jnp.where(rowi[0:nr] < nr - d, xs, 0.0)
            acc = acc + cw_ref[kk:kk + 1, :] * xs
        return acc

    half_c_sp = (0.5 * LRU_C) * sp

    def gates(t, d, a_ref, b_ref):
        gr = GATE_ROWS
        for n in range(tc // LANES):
            sl = slice(n * LANES, (n + 1) * LANES)
            c = half_c_sp[d:d + 1, sl]
            for rc in range(tt // gr):
                rows = slice(rc * gr, (rc + 1) * gr)
                u = u_s[t * tt + rc * gr:t * tt + (rc + 1) * gr, sl]
                ri = _dot(u.astype(BF16), wg_ref[d, n])
                t_r = jnp.tanh(ri[:, :LANES] + gb_ref[d, 0:1, sl])
                t_i = jnp.tanh(ri[:, LANES:] + gb_ref[d, 1:2, sl])
                neg_log_a = c * t_r + c
                a = jnp.exp(-neg_log_a)
                one_minus_a2 = jnp.tanh(neg_log_a) * (a * a + 1.0)
                root = one_minus_a2 * lax.rsqrt(jnp.maximum(one_minus_a2, MIN_NORMAL))
                a_ref[rows, sl] = a
                b_ref[rows, sl] = root * ((t_i + 1.0) * (0.5 * u))

    nz = n_ctx // tt
    fwd_order = list(range(nt))
    bwd_order = list(range(nz - 1, -1, -1)) + list(range(nt - 1, nz - 1, -1))
    for t in range(nt):
        for r0 in range(t * tt, (t + 1) * tt, CONV_ROWS):
            u_s[r0:r0 + CONV_ROWS, :] = conv_rows(r0, CONV_ROWS)

    sub = 8
    row8 = lax.broadcasted_iota(jnp.int32, (sub, 1), 0)

    def group_scan(a, b, h_in, reverse):
        for s in (1, 2, 4):
            keep = row8 < sub - s if reverse else row8 >= s
            shift = sub - s if reverse else s
            a_prev = jnp.where(keep, pltpu.roll(a, shift, 0), 1.0)
            b_prev = jnp.where(keep, pltpu.roll(b, shift, 0), 0.0)
            b = a * b_prev + b
            a = a * a_prev
        h = a * h_in + b
        last = h[0:1, :] if reverse else h[sub - 1:sub, :]
        return h, jnp.broadcast_to(last, h.shape)

    carry = (jnp.zeros((sub, tc), F32), jnp.zeros((sub, tc), F32))
    for tf, tb in zip(fwd_order, bwd_order):
        gates(tf, 0, af, bf)
        gates(tb, 1, ab, bb)

        def step(jj, c, tf=tf, tb=tb):
            f0 = pl.multiple_of(jj * sub, sub)
            b0 = pl.multiple_of(tt - sub - jj * sub, sub)
            h_f, hf = group_scan(af[pl.ds(f0, sub), :], bf[pl.ds(f0, sub), :], c[0], False)
            h_b, hb = group_scan(ab[pl.ds(b0, sub), :], bb[pl.ds(b0, sub), :], c[1], True)
            hf_s[pl.ds(tf * tt + f0, sub), :] = h_f
            hb_s[pl.ds(tb * tt + b0, sub), :] = h_b
            return hf, hb

        carry = lax.fori_loop(0, tt // sub, step, carry)

    for t in range(nt):
        rs = slice(t * tt, (t + 1) * tt)
        o_ref[0, rs, :] = ((hf_s[rs, :] + hb_s[rs, :]) * gt_ref[0, rs, :].astype(F32)).astype(o_ref.dtype)


def _lru(proj, conv_w, conv_b, gate_w, gate_b, lam, n_ctx):
    b, p, d2 = proj.shape
    d = d2 // 2
    tc = 512
    tt = SCAN_TILE
    assert p % tt == 0 and n_ctx % tt == 0 and d % tc == 0 and d // LRU_BLOCKS == LANES
    nt = p // tt
    nct = d // tc
    wg = (0.5 * jnp.concatenate([gate_w[:, 0], gate_w[:, 1]], axis=-1)).astype(BF16)
    gate_b = 0.5 * gate_b
    kern = functools.partial(_lru_kernel, n_ctx=n_ctx, tt=tt, nt=nt)
    return pl.pallas_call(
        kern,
        grid=(b, nct),
        in_specs=[pl.BlockSpec((1, p, tc), lambda bi, ci: (bi, 0, ci)),
                  pl.BlockSpec((1, p, tc), lambda bi, ci: (bi, 0, nct + ci)),
                  pl.BlockSpec((CONV_W, tc), lambda bi, ci: (0, ci)),
                  pl.BlockSpec((1, tc), lambda bi, ci: (0, ci)),
                  pl.BlockSpec((2, tc // LANES, LANES, 2 * LANES), lambda bi, ci: (0, ci, 0, 0)),
                  pl.BlockSpec((2, 2, tc), lambda bi, ci: (0, 0, ci)),
                  pl.BlockSpec((2, tc), lambda bi, ci: (0, ci))],
        out_specs=pl.BlockSpec((1, p, tc), lambda bi, ci: (bi, 0, ci)),
        out_shape=jax.ShapeDtypeStruct((b, p, d), BF16),
        scratch_shapes=[pltpu.VMEM((p + 16, tc), F32), pltpu.VMEM((p, tc), F32),
                        pltpu.VMEM((p, tc), F32), pltpu.VMEM((p, tc), F32),
                        pltpu.VMEM((tt, tc), F32), pltpu.VMEM((tt, tc), F32),
                        pltpu.VMEM((tt, tc), F32), pltpu.VMEM((tt, tc), F32)],
        compiler_params=_params(("parallel", "parallel")),
        name="rglru",
    )(proj, proj, conv_w, conv_b.reshape(1, d), wg, gate_b, lam)


def _na_window(rb, rows):
    win = NA_QROWS + NA_ROWS - 1
    return min(max(NA_QROWS * rb - NA_ROWS // 2, 0), rows - win)


def _na_bias(rpb, rows):
    nb = rows // NA_QROWS
    win = NA_QROWS + NA_ROWS - 1
    kr = min(NA_ROWS, rows)
    n_rel_r, n_rel_c = 2 * NA_ROWS - 1, 2 * NA_COLS - 1
    col = np.arange(GRID_W)
    c0 = np.clip(col - NA_COLS // 2, 0, GRID_W - NA_COLS)
    valid_c = (col[None, :] >= c0[:, None]) & (col[None, :] < c0[:, None] + NA_COLS)
    rel_c = np.clip(col[None, :] - col[:, None] + NA_COLS - 1, 0, n_rel_c - 1)
    pick_c = (rel_c[..., None] == np.arange(n_rel_c)).astype(np.float32)
    pick_r, valid = [], []
    for rb in (0, 1, nb - 1):
        r_abs = NA_QROWS * rb + np.arange(NA_QROWS)
        r0 = np.clip(r_abs - kr // 2, 0, rows - kr)
        k_abs = _na_window(rb, rows) + np.arange(win)
        valid_r = (k_abs[None, :] >= r0[:, None]) & (k_abs[None, :] < r0[:, None] + kr)
        rel_r = np.clip(k_abs[None, :] - r_abs[:, None] + NA_ROWS - 1, 0, n_rel_r - 1)
        pick_r.append((rel_r[..., None] == np.arange(n_rel_r)).astype(np.float32))
        valid.append(valid_r[:, None, :, None] & valid_c[None, :, None, :])
    rpb2 = rpb.reshape(rpb.shape[0] // 2, 2, n_rel_r, n_rel_c)
    rows_sel = jnp.einsum('peab,tqwa->pteqwb', rpb2, jnp.asarray(np.stack(pick_r)), precision=lax.Precision.HIGHEST)
    bias = jnp.einsum('pteqwb,ckb->pteqcwk', rows_sel, jnp.asarray(pick_c), precision=lax.Precision.HIGHEST)
    bias = jnp.where(jnp.asarray(np.stack(valid))[None, :, None], bias, NEG_INF)
    return bias.reshape(rpb.shape[0] // 2, 3, 2 * NA_QROWS * GRID_W, win * GRID_W)


def _na_kernel(q_ref, k_ref, v_ref, bias_ref, o_ref, vx_ref, *, n_ctx, rows):
    hd = LANES // 2
    scale = hd ** -0.5
    nq = NA_QROWS * GRID_W
    nb = rows // NA_QROWS
    nk = (NA_QROWS + NA_ROWS - 1) * GRID_W
    first = lax.broadcasted_iota(jnp.int32, (1, LANES), 1) < hd
    vx_ref[:, 0:LANES] = v_ref[0]
    vx_ref[:, LANES:2 * LANES] = jnp.ones((vx_ref.shape[0], LANES), BF16)

    def attend(qb, ks, bias_ty):
        n = qb.shape[0]
        qb = qb * scale
        zero = jnp.zeros_like(qb)
        q2 = jnp.concatenate([jnp.where(first, qb, zero), jnp.where(first, zero, qb)], axis=0)
        s_ctx = _dot_nt(q2, k_ref[0, 0:n_ctx, :])
        m = jnp.max(s_ctx, axis=-1, keepdims=True)
        if ks is not None:
            s_loc = _dot_nt(q2, k_ref[0, ks:ks + nk, :]) + bias_ref[0, bias_ty]
            m = jnp.maximum(m, jnp.max(s_loc, axis=-1, keepdims=True))
        o = _dot(jnp.exp(s_ctx - m).astype(BF16), vx_ref[0:n_ctx, :])
        if ks is not None:
            o = o + _dot(jnp.exp(s_loc - m).astype(BF16), vx_ref[ks:ks + nk, :])
        o = o[:, 0:LANES] * (1.0 / o[:, LANES:2 * LANES])
        return jnp.where(first, o[0:n], o[n:2 * n])

    o_ref[0, 0:n_ctx, :] = attend(q_ref[0, 0:n_ctx, :], None, None).astype(o_ref.dtype)
    for rb in range(nb):
        qs = n_ctx + rb * nq
        ks = n_ctx + _na_window(rb, rows) * GRID_W
        ty = 0 if rb == 0 else (2 if rb == nb - 1 else 1)
        o_ref[0, qs:qs + nq, :] = attend(q_ref[0, qs:qs + nq, :], ks, ty).astype(o_ref.dtype)


def _na(qkv, rpb, n_ctx):
    b, p, d3 = qkv.shape
    d = d3 // 3
    rows = (p - n_ctx) // GRID_W
    assert d // NA_HEADS == LANES // 2 and rows % NA_QROWS == 0 and rows // NA_QROWS >= 3
    nhp = d // LANES
    bias = _na_bias(rpb.astype(F32), rows)
    nq, nk = bias.shape[2] // 2, bias.shape[3]
    kern = functools.partial(_na_kernel, n_ctx=n_ctx, rows=rows)
    return pl.pallas_call(
        kern,
        grid=(nhp, b),
        in_specs=[pl.BlockSpec((1, p, LANES), lambda hp, bi: (bi, 0, hp)),
                  pl.BlockSpec((1, p, LANES), lambda hp, bi: (bi, 0, nhp + hp)),
                  pl.BlockSpec((1, p, LANES), lambda hp, bi: (bi, 0, 2 * nhp + hp)),
                  pl.BlockSpec((1, 3, 2 * nq, nk), lambda hp, bi: (hp, 0, 0, 0))],
        out_specs=pl.BlockSpec((1, p, LANES), lambda hp, bi: (bi, 0, hp)),
        out_shape=jax.ShapeDtypeStruct((b, p, d), BF16),
        scratch_shapes=[pltpu.VMEM((p, 2 * LANES), BF16)],
        compiler_params=_params(("parallel", "parallel")),
        name="nbr_attn",
    )(qkv, qkv, qkv, bias)


def _ret_tables(n_ctx, s, dk, ch):
    quarter = dk // 4
    pos = jnp.arange(s)
    inv = ROPE_BASE ** (-jnp.arange(quarter, dtype=F32) / quarter)
    ang_r = (pos // GRID_W).astype(F32)[:, None] * inv
    ang_c = (pos % GRID_W).astype(F32)[:, None] * inv
    cos = jnp.concatenate([jnp.cos(ang_r)] * 2 + [jnp.cos(ang_c)] * 2, axis=-1)
    sin = jnp.concatenate([-jnp.sin(ang_r), jnp.sin(ang_r), -jnp.sin(ang_c), jnp.sin(ang_c)], axis=-1)
    cos = jnp.concatenate([jnp.ones((n_ctx, dk), F32), cos], axis=0)
    sin = jnp.concatenate([jnp.zeros((n_ctx, dk), F32), sin], axis=0)
    log_gamma = jnp.log1p(-(2.0 ** (-5.0 - jnp.arange(RET_HEADS, dtype=F32))))[:, None, None]
    pq = jnp.arange(ch, dtype=F32)
    col = jnp.broadcast_to(pq[:, None], (ch, dk))[None]
    tabs = jnp.stack([
        jnp.exp(jnp.abs(pq[:, None] - pq[None, :])[None] * log_gamma),
        jnp.exp((col + 1.0) * log_gamma),
        jnp.exp((ch - col) * log_gamma),
        jnp.exp((ch - 1.0 - col) * log_gamma),
        jnp.exp(col * log_gamma),
    ], axis=1)
    chunk_decay = jnp.exp(ch * log_gamma[:, 0, 0])
    return cos, sin, tabs, chunk_decay


def _ret_kernel(cd_ref, q_ref, k_ref, v_ref, g_ref, cos_ref, sin_ref, tab_ref, o_ref,
                sb_ref, st_ref, *, ch, nc):
    dk = q_ref.shape[-1]
    cd = cd_ref[pl.program_id(1)]
    k_scale = dk ** -0.5

    def rope(t, c0):
        parts = []
        for hf in range(dk // LANES):
            sl = slice(hf * LANES, (hf + 1) * LANES)
            th = t[:, sl]
            parts.append(th * cos_ref[pl.ds(c0, ch), sl] + pltpu.roll(th, LANES // 2, 1) * sin_ref[pl.ds(c0, ch), sl])
        return jnp.concatenate(parts, axis=-1)

    def load_k(c0):
        return rope(k_ref[0, pl.ds(c0, ch), :].astype(F32) * k_scale, c0)

    def kv_outer(kdec, c0):
        return _dot_tn(kdec.astype(BF16), v_ref[0, pl.ds(c0, ch), :])

    nz = 1
    sb_ref[0] = jnp.zeros(sb_ref.shape[1:], BF16)
    st_ref[...] = kv_outer(load_k(0) * tab_ref[0, 4], 0)

    def bwd(j, _):
        c = nc - 1 - j
        c0 = pl.multiple_of(c * ch, ch)
        sb_ref[c] = st_ref[...].astype(BF16)
        st_ref[...] = cd * st_ref[...] + kv_outer(load_k(c0) * tab_ref[0, 4], c0)
        return 0

    lax.fori_loop(0, nc - 1 - nz, bwd, 0, unroll=True)
    sb_ref[nz] = st_ref[...].astype(BF16)

    st_ref[...] = jnp.zeros(st_ref.shape, F32)

    def fwd(c, _):
        c0 = pl.multiple_of(c * ch, ch)
        q = rope(q_ref[0, pl.ds(c0, ch), :].astype(F32), c0)
        k = load_k(c0)
        v = v_ref[0, pl.ds(c0, ch), :]
        inner = _dot_nt(q.astype(BF16), k.astype(BF16)) * tab_ref[0, 0]
        o = (_dot(inner.astype(BF16), v)
             + _dot((q * tab_ref[0, 1]).astype(BF16), st_ref[...].astype(BF16))
             + _dot((q * tab_ref[0, 2]).astype(BF16), sb_ref[c]))
        st_ref[...] = cd * st_ref[...] + kv_outer(k * tab_ref[0, 3], c0)
        o = o * lax.rsqrt(jnp.mean(o * o, axis=-1, keepdims=True) + RMS_EPS)
        g = g_ref[0, pl.ds(c0, ch), :].astype(F32)
        o_ref[0, pl.ds(c0, ch), :] = (o * (g * jax.nn.sigmoid(g))).astype(o_ref.dtype)
        return 0

    lax.fori_loop(0, nc, fwd, 0, unroll=3 if nc % 3 == 0 else 1)


def _ret(qkvg, n_ctx):
    b, p, d6 = qkvg.shape
    d = d6 // 6
    dk = d // RET_HEADS
    dv = 2 * dk
    ch = dk
    assert dk == 2 * LANES and n_ctx == ch and p % ch == 0
    nc = p // ch
    cos, sin, tabs, chunk_decay = _ret_tables(n_ctx, p - n_ctx, dk, ch)
    kern = functools.partial(_ret_kernel, ch=ch, nc=nc)
    nh = RET_HEADS
    return pl.pallas_call(
        kern,
        grid_spec=pltpu.PrefetchScalarGridSpec(
            num_scalar_prefetch=1,
            grid=(b, nh),
            in_specs=[pl.BlockSpec((1, p, dk), lambda bi, h, cd: (bi, 0, h)),
                      pl.BlockSpec((1, p, dk), lambda bi, h, cd: (bi, 0, nh + h)),
                      pl.BlockSpec((1, p, dv), lambda bi, h, cd: (bi, 0, nh + h)),
                      pl.BlockSpec((1, p, dv), lambda bi, h, cd: (bi, 0, 2 * nh + h)),
                      pl.BlockSpec((p, dk), lambda bi, h, cd: (0, 0)),
                      pl.BlockSpec((p, dk), lambda bi, h, cd: (0, 0)),
                      pl.BlockSpec((1, 5, ch, dk), lambda bi, h, cd: (h, 0, 0, 0))],
            out_specs=pl.BlockSpec((1, p, dv), lambda bi, h, cd: (bi, 0, h)),
            scratch_shapes=[pltpu.VMEM((nc, dk, dv), BF16), pltpu.VMEM((dk, dv), F32)]),
        out_shape=jax.ShapeDtypeStruct((b, p, nh * dv), BF16),
        compiler_params=_params(("parallel", "parallel")),
        name="retention",
    )(chunk_decay, qkvg, qkvg, qkvg, qkvg, cos, sin, tabs)


def _router_kernel(lg_ref, rb_ref, tri_ref, e_ref, w_ref, rk_ref, cnt_ref, run_ref):
    @pl.when(pl.program_id(0) == 0)
    def _():
        run_ref[...] = jnp.zeros(run_ref.shape, F32)

    lg = lg_ref[...]
    ex = jnp.exp(lg - jnp.max(lg, axis=0, keepdims=True))
    probs = ex / jnp.sum(ex, axis=0, keepdims=True)
    sel = probs + rb_ref[...]
    epg = EXPERTS_PER_GROUP
    best = grp = cur = curp = None
    for g in range(N_GROUPS):
        s = [sel[g * epg + i:g * epg + i + 1, :] for i in range(epg)]
        pr = [probs[g * epg + i:g * epg + i + 1, :] for i in range(epg)]
        top2 = None
        for i in range(epg):
            for j in range(i + 1, epg):
                top2 = s[i] + s[j] if top2 is None else jnp.maximum(top2, s[i] + s[j])
        if g == 0:
            best, grp, cur, curp = top2, jnp.zeros(top2.shape, jnp.int32), s, pr
        else:
            better = top2 > best
            best = jnp.where(better, top2, best)
            grp = jnp.where(better, g, grp)
            cur = [jnp.where(better, s[i], cur[i]) for i in range(epg)]
            curp = [jnp.where(better, pr[i], curp[i]) for i in range(epg)]
    b1, i1, p1 = cur[0], jnp.zeros(best.shape, jnp.int32), curp[0]
    for i in range(1, epg):
        gt = cur[i] > b1
        b1, i1, p1 = jnp.where(gt, cur[i], b1), jnp.where(gt, i, i1), jnp.where(gt, curp[i], p1)
    b2 = i2 = p2 = None
    for i in range(epg):
        v = jnp.where(i1 == i, -jnp.inf, cur[i])
        if b2 is None:
            b2, i2, p2 = v, jnp.zeros(best.shape, jnp.int32), curp[0]
        else:
            gt = v > b2
            b2, i2, p2 = jnp.where(gt, v, b2), jnp.where(gt, i, i2), jnp.where(gt, curp[i], p2)
    e1 = grp * epg + i1
    e2 = grp * epg + i2
    inv = 1.0 / (p1 + p2)
    e_ref[0:1, :] = e1
    e_ref[1:2, :] = e2
    w_ref[0:1, :] = p1 * inv
    w_ref[1:2, :] = p2 * inv

    eidx = lax.broadcasted_iota(jnp.int32, lg.shape, 0)
    eq1 = eidx == e1
    eq2 = eidx == e2
    member = jnp.where(eq1, 1.0, jnp.where(eq2, 1.0, 0.0))
    before = _dot(member.astype(BF16), tri_ref[...]) + run_ref[...]
    rk_ref[0:1, :] = jnp.sum(jnp.where(eq1, before, 0.0), axis=0, keepdims=True).astype(jnp.int32)
    rk_ref[1:2, :] = jnp.sum(jnp.where(eq2, before, 0.0), axis=0, keepdims=True).astype(jnp.int32)
    run_ref[...] = run_ref[...] + jnp.sum(member, axis=1, keepdims=True)
    cnt_ref[...] = run_ref[...]


def _router(logits_t, router_b):
    e, n = logits_t.shape
    tt = _pick(n, (1024, 512))
    tri = (jnp.arange(tt)[:, None] < jnp.arange(tt)[None, :]).astype(BF16)
    kn = jax.ShapeDtypeStruct((TOP_K, n), jnp.int32)
    return pl.pallas_call(
        _router_kernel,
        grid=(n // tt,),
        in_specs=[pl.BlockSpec((e, tt), lambda i: (0, i)),
                  pl.BlockSpec((e, 1), lambda i: (0, 0)),
                  pl.BlockSpec((tt, tt), lambda i: (0, 0))],
        out_specs=[pl.BlockSpec((TOP_K, tt), lambda i: (0, i)),
                   pl.BlockSpec((TOP_K, tt), lambda i: (0, i)),
                   pl.BlockSpec((TOP_K, tt), lambda i: (0, i)),
                   pl.BlockSpec((e, 1), lambda i: (0, 0))],
        out_shape=[kn, jax.ShapeDtypeStruct((TOP_K, n), F32), kn, jax.ShapeDtypeStruct((e, 1), F32)],
        scratch_shapes=[pltpu.VMEM((e, 1), F32)],
        compiler_params=_params(("arbitrary",)),
        name="router",
    )(logits_t, router_b.astype(F32).reshape(e, 1), tri)


def _moe_kernel(be_ref, nu_ref, xs_ref, wg_ref, wu_ref, wd_ref, ys_ref, wgb, wub, wdb, *, fc):
    i = pl.program_id(0)
    used = i < nu_ref[0]
    new_expert = jnp.logical_or(i == 0, be_ref[i] != be_ref[jnp.maximum(i - 1, 0)])

    @pl.when(jnp.logical_and(used, new_expert))
    def _():
        wgb[...] = wg_ref[0, 0].astype(BF16)
        wub[...] = wu_ref[0, 0].astype(BF16)
        wdb[...] = wd_ref[0, 0].astype(BF16)

    @pl.when(used)
    def _():
        x = xs_ref[...]
        f = wgb.shape[-1]
        acc = jnp.zeros(ys_ref.shape, F32)
        for c in range(f // fc):
            sl = slice(c * fc, (c + 1) * fc)
            g = _dot(x, wgb[:, sl])
            u = _dot(x, wub[:, sl])
            a = (g * jax.nn.sigmoid(g) * u).astype(BF16)
            acc = acc + _dot(a, wdb[sl, :])
        ys_ref[...] = acc.astype(ys_ref.dtype)

    @pl.when(jnp.logical_not(used))
    def _():
        ys_ref[...] = jnp.zeros(ys_ref.shape, ys_ref.dtype)


def _moe_experts(xs, block_expert, n_used, w_gate, w_up, w_down, layer):
    n_rows, d = xs.shape
    f = w_gate.shape[-1]
    bm = MOE_BLOCK
    kern = functools.partial(_moe_kernel, fc=_pick(f, (512, 256, 128)))
    return pl.pallas_call(
        kern,
        grid_spec=pltpu.PrefetchScalarGridSpec(
            num_scalar_prefetch=2,
            grid=(n_rows // bm,),
            in_specs=[pl.BlockSpec((bm, d), lambda i, be, nu: (i, 0)),
                      pl.BlockSpec((1, 1, d, f), lambda i, be, nu: (layer, be[i], 0, 0)),
                      pl.BlockSpec((1, 1, d, f), lambda i, be, nu: (layer, be[i], 0, 0)),
                      pl.BlockSpec((1, 1, f, d), lambda i, be, nu: (layer, be[i], 0, 0))],
            out_specs=pl.BlockSpec((bm, d), lambda i, be, nu: (i, 0)),
            scratch_shapes=[pltpu.VMEM((d, f), BF16), pltpu.VMEM((d, f), BF16), pltpu.VMEM((f, d), BF16)]),
        out_shape=jax.ShapeDtypeStruct((n_rows, d), BF16),
        compiler_params=_params(("arbitrary",)),
        name="moe_experts",
    )(block_expert, n_used, xs, w_gate, w_up, w_down)


def _moe(h_rows, tok_ids, logits_t, router_b, w_gate, w_up, w_down, layer):
    n = tok_ids.shape[0]
    bm = MOE_BLOCK
    expert, weight, rank, counts = _router(logits_t, router_b)
    counts = counts[:, 0].astype(jnp.int32)
    padded = (counts + bm - 1) // bm * bm
    pad_end = jnp.cumsum(padded)
    pad_start = pad_end - padded
    dest = rank + jnp.sum(jnp.where(expert[..., None] == jnp.arange(N_EXPERTS), pad_start, 0), axis=-1)
    n_blocks = -(-(n * TOP_K) // bm) + N_EXPERTS
    n_rows = n_blocks * bm
    tok = tok_ids.astype(jnp.int32)
    src = (jnp.arange(n_rows, dtype=jnp.int32) % h_rows.shape[0]).at[dest.reshape(-1)].set(
        jnp.tile(tok, TOP_K), unique_indices=True, mode='promise_in_bounds')
    block_start = jnp.arange(n_blocks, dtype=jnp.int32) * bm
    block_expert = jnp.minimum(
        jnp.sum((pad_end[None, :] <= block_start[:, None]).astype(jnp.int32), axis=1), N_EXPERTS - 1)
    n_used = (pad_end[-1] // bm).astype(jnp.int32).reshape(1)
    ys = _moe_experts(h_rows[src], block_expert, n_used, w_gate, w_up, w_down, layer)
    return [ys[dest[k]] for k in range(TOP_K)], weight


def _final_kernel(x_ref, y0_ref, y1_ref, gw_ref, mod_ref, g_ref, o_ref):
    gate = mod_ref[0, 2 * N_MOD - 1:2 * N_MOD, :]
    xn = x_ref[0] + gate * _ffn_sum(y0_ref, y1_ref, gw_ref, slice(None))
    o_ref[0] = xn * lax.rsqrt(jnp.mean(xn * xn, axis=-1, keepdims=True) + RMS_EPS) * g_ref[...]


def _final(t, ys, gw, mod, g, n_ctx):
    b, p, d = t.shape
    rows = p - n_ctx
    tm = _pick(n_ctx, (512, 256))
    assert rows % tm == 0
    off = n_ctx // tm
    row_spec = pl.BlockSpec((1, tm, d), lambda bi, i: (bi, i, 0))
    return pl.pallas_call(
        _final_kernel,
        grid=(b, rows // tm),
        in_specs=[pl.BlockSpec((1, tm, d), lambda bi, i: (bi, i + off, 0)),
                  row_spec, row_spec,
                  pl.BlockSpec((1, tm, TOP_K), lambda bi, i: (bi, i, 0)),
                  pl.BlockSpec((1, 2 * N_MOD, d), lambda bi, i: (bi, 0, 0)),
                  pl.BlockSpec((1, d), lambda bi, i: (0, 0))],
        out_specs=row_spec,
        out_shape=jax.ShapeDtypeStruct((b, rows, d), F32),
        compiler_params=_params(("parallel", "parallel")),
        name="ffn_final",
    )(t, ys[0], ys[1], gw, mod, g.reshape(1, d))


def kernel(x, c, ctx, c_ctx, ada_w, ada_b, norm_mix_g, norm_ffn_g, final_norm_g, lru_w_in, lru_conv_w, lru_conv_b, lru_gate_w, lru_gate_b, lru_lambda, lru_w_out, na_w_qkv, na_rpb, na_w_o, ret_w_qkvg, ret_w_o, router_w, router_b, moe_w_gate, moe_w_up, moe_w_down):
    b, s, d = x.shape
    n_ctx = ctx.shape[1]
    p = n_ctx + s
    depth = ada_w.shape[0]
    t = jnp.concatenate([ctx, x], axis=1)

    pad_rows = -(b + 1) % 16
    cc = jnp.concatenate([c, c_ctx[None, :], jnp.zeros((pad_rows, d), F32)], axis=0)
    mods = _ada(cc, ada_w, ada_b)
    mod_x = mods[:, :b].reshape(depth, b, N_MOD, d)
    mod_z = jnp.broadcast_to(mods[:, b].reshape(depth, 1, N_MOD, d), (depth, b, N_MOD, d))
    mods = jnp.concatenate([mod_z, mod_x], axis=2)

    rw32 = router_w.astype(F32)
    rw_hi = rw32.astype(BF16)
    rw_lo = (rw32 - rw_hi.astype(F32)).astype(BF16)
    rw = jnp.concatenate([rw_hi, rw_lo, jnp.zeros((d, ROUTER_LANES - 2 * N_EXPERTS), BF16)], axis=1)

    all_rows = jnp.arange(b * p, dtype=jnp.int32)
    pending = None
    for i in range(depth):
        kind, j = i % 3, i // 3
        last = i == depth - 1
        mod = _at(mods, i)
        if kind == 0:
            proj, t = _norm_proj(t, _at(norm_mix_g, i), mod, _at(lru_w_in, j).astype(BF16), n_ctx,
                                 act_cols=d, pending=pending)
            a = _lru(proj, _at(lru_conv_w, j), _at(lru_conv_b, j), _at(lru_gate_w, j), _at(lru_gate_b, j),
                     _at(lru_lambda, j), n_ctx)
            w_o = _at(lru_w_out, j)
        elif kind == 1:
            qkv, t = _norm_proj(t, _at(norm_mix_g, i), mod, _at(na_w_qkv, j).astype(BF16), n_ctx, pending=pending)
            a = _na(qkv, _at(na_rpb, j), n_ctx)
            w_o = _at(na_w_o, j)
        else:
            qkvg, t = _norm_proj(t, _at(norm_mix_g, i), mod, _at(ret_w_qkvg, j).astype(BF16), n_ctx, pending=pending)
            a = _ret(qkvg, n_ctx)
            w_o = _at(ret_w_o, j)
        t, h2, pl_ = _out_proj(a, w_o.astype(BF16), t, mod, _at(norm_ffn_g, i), rw, n_ctx)
        logits = (pl_[..., :N_EXPERTS] + pl_[..., N_EXPERTS:2 * N_EXPERTS])
        if last:
            tok_ids = all_rows.reshape(b, p)[:, n_ctx:].reshape(-1)
            logits = logits[:, n_ctx:]
        else:
            tok_ids = all_rows
        ys, wts = _moe(h2.reshape(b * p, d), tok_ids, logits.reshape(-1, N_EXPERTS).T, router_b,
                       moe_w_gate, moe_w_up, moe_w_down, i)
        ys = [y.reshape(b, -1, d) for y in ys]
        wts = wts.T.reshape(b, -1, TOP_K)
        pending = (ys[0], ys[1], wts, mod)
    return _final(t, ys, wts, mod, final_norm_g, n_ctx)
```

```python
import functools

import jax
import jax.numpy as jnp
import numpy as np
from jax import lax
from jax.experimental import pallas as pl
from jax.experimental.pallas import tpu as pltpu

F32 = jnp.float32
BF16 = jnp.bfloat16

GRID_W = 64
N_MOD = 6
RMS_EPS = 1e-6
MIN_NORMAL = float(np.finfo(np.float32).tiny)
LRU_BLOCKS = 8
CONV_W = 4
CONV_LEFT = CONV_W // 2
LRU_C = 8.0
NA_HEADS = 16
NA_ROWS = 8
NA_COLS = 16
NA_QROWS = 4
NEG_INF = -1e30
RET_HEADS = 4
ROPE_BASE = 10000.0
N_EXPERTS = 16
N_GROUPS = 4
EXPERTS_PER_GROUP = N_EXPERTS // N_GROUPS
TOP_K = 2
MOE_BLOCK = 512
LANES = 128
ROUTER_LANES = 128
SCAN_TILE = 256
GATE_ROWS = 128
CONV_ROWS = 64
VMEM_LIMIT = 56 * 1024 * 1024


def _pick(n, candidates):
    for c in candidates:
        if n % c == 0:
            return c
    raise ValueError(f"no tile in {candidates} divides {n}")


def _params(sem):
    return pltpu.CompilerParams(dimension_semantics=sem, vmem_limit_bytes=VMEM_LIMIT)


def _at(w, i):
    return lax.index_in_dim(w, i, axis=0, keepdims=False)


def _dot(a, b):
    return jnp.dot(a, b, preferred_element_type=F32)


def _dot_nt(a, b):
    return lax.dot_general(a, b, (((1,), (1,)), ((), ())), preferred_element_type=F32)


def _dot_tn(a, b):
    return lax.dot_general(a, b, (((0,), (0,)), ((), ())), preferred_element_type=F32)


def _is_ctx(pos0, tm, n_ctx):
    return (pos0 + lax.broadcasted_iota(jnp.int32, (tm, 1), 0)) < n_ctx


def _mod_row(mod_ref, is_ctx, k):
    return jnp.where(is_ctx, mod_ref[0, k:k + 1, :], mod_ref[0, N_MOD + k:N_MOD + k + 1, :])


def _norm_mod(x, g, mod_ref, is_ctx, k):
    y = x * lax.rsqrt(jnp.mean(x * x, axis=-1, keepdims=True) + RMS_EPS) * g
    return y * (1.0 + _mod_row(mod_ref, is_ctx, k + 1)) + _mod_row(mod_ref, is_ctx, k)


def _ada_kernel(cc_ref, w_ref, b_ref, o_ref):
    cc = cc_ref[...]
    s = (cc * jax.nn.sigmoid(cc)).astype(BF16)
    o_ref[0] = _dot(s, w_ref[0].astype(BF16)) + b_ref[0]


def _ada(cc, ada_w, ada_b):
    depth, d, n = ada_w.shape
    rows = cc.shape[0]
    tn = _pick(n, (1536, 1024, 512, 256, 128))
    return pl.pallas_call(
        _ada_kernel,
        grid=(depth, n // tn),
        in_specs=[pl.BlockSpec((rows, d), lambda i, j: (0, 0)),
                  pl.BlockSpec((1, d, tn), lambda i, j: (i, 0, j)),
                  pl.BlockSpec((1, 1, tn), lambda i, j: (i, 0, j))],
        out_specs=pl.BlockSpec((1, rows, tn), lambda i, j: (i, 0, j)),
        out_shape=jax.ShapeDtypeStruct((depth, rows, n), F32),
        compiler_params=_params(("parallel", "parallel")),
        name="ada_mod",
    )(cc, ada_w, ada_b.reshape(depth, 1, n))


def _ffn_sum(y0_ref, y1_ref, gw_ref, rows):
    gw = gw_ref[0, rows, :]
    return gw[:, 0:1] * y0_ref[0, rows, :].astype(F32) + gw[:, 1:2] * y1_ref[0, rows, :].astype(F32)


def _proj_kernel(*refs, tm, rs, tn, n_ctx, act_cols, pending):
    if pending:
        x_ref, y0_ref, y1_ref, gw_ref, pmod_ref, g_ref, mod_ref, w_ref, o_ref, xo_ref = refs
    else:
        x_ref, g_ref, mod_ref, w_ref, o_ref = refs
    i = pl.program_id(1)
    n = w_ref.shape[1]
    for s in range(tm // rs):
        rows = slice(s * rs, (s + 1) * rs)
        is_ctx = _is_ctx(i * tm + s * rs, rs, n_ctx)
        x = x_ref[0, rows, :]
        if pending:
            x = x + _mod_row(pmod_ref, is_ctx, 5) * _ffn_sum(y0_ref, y1_ref, gw_ref, rows)
            xo_ref[0, rows, :] = x
        h = _norm_mod(x, g_ref[...], mod_ref, is_ctx, 0).astype(BF16)
        for c in range(n // tn):
            cols = slice(c * tn, (c + 1) * tn)
            y = _dot(h, w_ref[:, cols])
            if c * tn < act_cols:
                y = jax.nn.gelu(y)
            o_ref[0, rows, cols] = y.astype(o_ref.dtype)


def _norm_proj(t, g, mod, w, n_ctx, act_cols=0, pending=None):
    b, p, d = t.shape
    n = w.shape[1]
    tm = _pick(p, (768, 512, 256))
    rs = _pick(tm, (256, 128))
    tn = _pick(n, (1024, 512))
    assert act_cols % tn == 0
    kern = functools.partial(_proj_kernel, tm=tm, rs=rs, tn=tn, n_ctx=n_ctx, act_cols=act_cols,
                             pending=pending is not None)
    row_spec = pl.BlockSpec((1, tm, d), lambda bi, i: (bi, i, 0))
    mod_spec = pl.BlockSpec((1, 2 * N_MOD, d), lambda bi, i: (bi, 0, 0))
    in_specs = [row_spec]
    args = [t]
    if pending is not None:
        in_specs += [pl.BlockSpec((None, 1, tm, d), lambda bi, i: (0, bi, i, 0)),
                     pl.BlockSpec((None, 1, tm, d), lambda bi, i: (1, bi, i, 0)),
                     pl.BlockSpec((1, tm, TOP_K), lambda bi, i: (bi, i, 0)), mod_spec]
        args += [pending[0], pending[0], pending[1], pending[2]]
    in_specs += [pl.BlockSpec((1, d), lambda bi, i: (0, 0)), mod_spec,
                 pl.BlockSpec((d, n), lambda bi, i: (0, 0), pipeline_mode=pl.Buffered(1))]
    args += [g.reshape(1, d), mod, w]
    out_specs = [pl.BlockSpec((1, tm, n), lambda bi, i: (bi, i, 0))]
    out_shape = [jax.ShapeDtypeStruct((b, p, n), BF16)]
    if pending is not None:
        out_specs.append(row_spec)
        out_shape.append(jax.ShapeDtypeStruct((b, p, d), F32))
    res = pl.pallas_call(
        kern,
        grid=(b, p // tm),
        in_specs=in_specs,
        out_specs=out_specs,
        out_shape=out_shape,
        compiler_params=_params(("parallel", "parallel")),
        name="norm_proj",
    )(*args)
    return (res[0], res[1]) if pending is not None else (res[0], t)


def _out_kernel(a_ref, w_ref, x_ref, mod_ref, g_ref, rw_ref, xo_ref, h_ref, p_ref, *, tm, rs, n_ctx):
    for s in range(tm // rs):
        rows = slice(s * rs, (s + 1) * rs)
        is_ctx = _is_ctx(pl.program_id(1) * tm + s * rs, rs, n_ctx)
        y = _dot(a_ref[0, rows, :], w_ref[...])
        xn = x_ref[0, rows, :] + _mod_row(mod_ref, is_ctx, 2) * y
        xo_ref[0, rows, :] = xn
        h = _norm_mod(xn, g_ref[...], mod_ref, is_ctx, 3)
        hi = h.astype(BF16)
        h_ref[0, rows, :] = hi
        lo = (h - hi.astype(F32)).astype(BF16)
        p_ref[0, rows, :] = _dot(hi, rw_ref[...]) + _dot(lo, rw_ref[...])


def _out_proj(a, w, t, mod, g, rw, n_ctx):
    b, p, d = t.shape
    k = a.shape[-1]
    tm = _pick(p, (768, 512, 256))
    kern = functools.partial(_out_kernel, tm=tm, rs=_pick(tm, (256, 128)), n_ctx=n_ctx)
    return pl.pallas_call(
        kern,
        grid=(b, p // tm),
        in_specs=[pl.BlockSpec((1, tm, k), lambda bi, i: (bi, i, 0)),
                  pl.BlockSpec((k, d), lambda bi, i: (0, 0)),
                  pl.BlockSpec((1, tm, d), lambda bi, i: (bi, i, 0)),
                  pl.BlockSpec((1, 2 * N_MOD, d), lambda bi, i: (bi, 0, 0)),
                  pl.BlockSpec((1, d), lambda bi, i: (0, 0)),
                  pl.BlockSpec((d, ROUTER_LANES), lambda bi, i: (0, 0))],
        out_specs=[pl.BlockSpec((1, tm, d), lambda bi, i: (bi, i, 0)),
                   pl.BlockSpec((1, tm, d), lambda bi, i: (bi, i, 0)),
                   pl.BlockSpec((1, tm, ROUTER_LANES), lambda bi, i: (bi, i, 0))],
        out_shape=[jax.ShapeDtypeStruct((b, p, d), F32),
                   jax.ShapeDtypeStruct((b, p, d), BF16),
                   jax.ShapeDtypeStruct((b, p, ROUTER_LANES), F32)],
        compiler_params=_params(("parallel", "parallel")),
        name="out_proj",
    )(a, w, t, mod, g.reshape(1, d), rw)


def _lru_kernel(gt_ref, up_ref, cw_ref, cb_ref, wg_ref, gb_ref, lam_ref, o_ref,
                upad, u_s, hf_s, hb_s, af, bf, ab, bb, *, n_ctx, tt, nt):
    p = nt * tt
    tc = o_ref.shape[-1]
    pad = 8
    upad[0:pad, :] = jnp.zeros((pad, tc), F32)
    upad[pad + p:pad + p + pad, :] = jnp.zeros((pad, tc), F32)
    for t in range(nt):
        upad[pad + t * tt:pad + (t + 1) * tt, :] = up_ref[0, t * tt:(t + 1) * tt, :].astype(F32)

    neg_lam = -lam_ref[...]
    sp = jnp.maximum(neg_lam, 0.0) + jnp.log1p(jnp.exp(-jnp.abs(neg_lam)))
    rowi = lax.broadcasted_iota(jnp.int32, (tt, 1), 0)

    def conv_rows(r0, nr):
        first = r0 in (0, n_ctx)
        last = r0 + nr in (n_ctx, p)
        acc = jnp.broadcast_to(cb_ref[...], (nr, tc))
        for kk in range(CONV_W):
            d = kk - CONV_LEFT
            xs = upad[pad + r0 + d:pad + r0 + d + nr, :]
            if first and d < 0:
                xs = jnp.where(rowi[0:nr] >= -d, xs, 0.0)
            if last and d > 0:
                xs = jnp.where(rowi[0:nr] < nr - d, xs, 0.0)
            acc = acc + cw_ref[kk:kk + 1, :] * xs
        return acc

    half_c_sp = (0.5 * LRU_C) * sp

    def gates(t, d, a_ref, b_ref):
        gr = GATE_ROWS
        for n in range(tc // LANES):
            sl = slice(n * LANES, (n + 1) * LANES)
            c = half_c_sp[d:d + 1, sl]
            for rc in range(tt // gr):
                rows = slice(rc * gr, (rc + 1) * gr)
                u = u_s[t * tt + rc * gr:t * tt + (rc + 1) * gr, sl]
                ri = _dot(u.astype(BF16), wg_ref[d, n])
                t_r = jnp.tanh(ri[:, :LANES] + gb_ref[d, 0:1, sl])
                t_i = jnp.tanh(ri[:, LANES:] + gb_ref[d, 1:2, sl])
                neg_log_a = c * t_r + c
                a = jnp.exp(-neg_log_a)
                one_minus_a2 = jnp.tanh(neg_log_a) * (a * a + 1.0)
                root = one_minus_a2 * lax.rsqrt(jnp.maximum(one_minus_a2, MIN_NORMAL))
                a_ref[rows, sl] = a
                b_ref[rows, sl] = root * ((t_i + 1.0) * (0.5 * u))

    nz = n_ctx // tt
    fwd_order = list(range(nt))
    bwd_order = list(range(nz - 1, -1, -1)) + list(range(nt - 1, nz - 1, -1))
    for t in range(nt):
        for r0 in range(t * tt, (t + 1) * tt, CONV_ROWS):
            u_s[r0:r0 + CONV_ROWS, :] = conv_rows(r0, CONV_ROWS)

    sub = 8
    row8 = lax.broadcasted_iota(jnp.int32, (sub, 1), 0)

    def group_scan(a, b, h_in, reverse):
        for s in (1, 2, 4):
            keep = row8 < sub - s if reverse else row8 >= s
            shift = sub - s if reverse else s
            a_prev = jnp.where(keep, pltpu.roll(a, shift, 0), 1.0)
            b_prev = jnp.where(keep, pltpu.roll(b, shift, 0), 0.0)
            b = a * b_prev + b
            a = a * a_prev
        h = a * h_in + b
        last = h[0:1, :] if reverse else h[sub - 1:sub, :]
        return h, jnp.broadcast_to(last, h.shape)

    carry = (jnp.zeros((sub, tc), F32), jnp.zeros((sub, tc), F32))
    for tf, tb in zip(fwd_order, bwd_order):
        gates(tf, 0, af, bf)
        gates(tb, 1, ab, bb)

        def step(jj, c, tf=tf, tb=tb):
            f0 = pl.multiple_of(jj * sub, sub)
            b0 = pl.multiple_of(tt - sub - jj * sub, sub)
            h_f, hf = group_scan(af[pl.ds(f0, sub), :], bf[pl.ds(f0, sub), :], c[0], False)
            h_b, hb = group_scan(ab[pl.ds(b0, sub), :], bb[pl.ds(b0, sub), :], c[1], True)
            hf_s[pl.ds(tf * tt + f0, sub), :] = h_f
            hb_s[pl.ds(tb * tt + b0, sub), :] = h_b
            return hf, hb

        carry = lax.fori_loop(0, tt // sub, step, carry)

    for t in range(nt):
        rs = slice(t * tt, (t + 1) * tt)
        o_ref[0, rs, :] = ((hf_s[rs, :] + hb_s[rs, :]) * gt_ref[0, rs, :].astype(F32)).astype(o_ref.dtype)


def _lru(proj, conv_w, conv_b, gate_w, gate_b, lam, n_ctx):
    b, p, d2 = proj.shape
    d = d2 // 2
    tc = 512
    tt = SCAN_TILE
    assert p % tt == 0 and n_ctx % tt == 0 and d % tc == 0 and d // LRU_BLOCKS == LANES
    nt = p // tt
    nct = d // tc
    wg = (0.5 * jnp.concatenate([gate_w[:, 0], gate_w[:, 1]], axis=-1)).astype(BF16)
    gate_b = 0.5 * gate_b
    kern = functools.partial(_lru_kernel, n_ctx=n_ctx, tt=tt, nt=nt)
    return pl.pallas_call(
        kern,
        grid=(b, nct),
        in_specs=[pl.BlockSpec((1, p, tc), lambda bi, ci: (bi, 0, ci)),
                  pl.BlockSpec((1, p, tc), lambda bi, ci: (bi, 0, nct + ci)),
                  pl.BlockSpec((CONV_W, tc), lambda bi, ci: (0, ci)),
                  pl.BlockSpec((1, tc), lambda bi, ci: (0, ci)),
                  pl.BlockSpec((2, tc // LANES, LANES, 2 * LANES), lambda bi, ci: (0, ci, 0, 0)),
                  pl.BlockSpec((2, 2, tc), lambda bi, ci: (0, 0, ci)),
                  pl.BlockSpec((2, tc), lambda bi, ci: (0, ci))],
        out_specs=pl.BlockSpec((1, p, tc), lambda bi, ci: (bi, 0, ci)),
        out_shape=jax.ShapeDtypeStruct((b, p, d), BF16),
        scratch_shapes=[pltpu.VMEM((p + 16, tc), F32), pltpu.VMEM((p, tc), F32),
                        pltpu.VMEM((p, tc), F32), pltpu.VMEM((p, tc), F32),
                        pltpu.VMEM((tt, tc), F32), pltpu.VMEM((tt, tc), F32),
                        pltpu.VMEM((tt, tc), F32), pltpu.VMEM((tt, tc), F32)],
        compiler_params=_params(("parallel", "parallel")),
        name="rglru",
    )(proj, proj, conv_w, conv_b.reshape(1, d), wg, gate_b, lam)


def _na_window(rb, rows):
    win = NA_QROWS + NA_ROWS - 1
    return min(max(NA_QROWS * rb - NA_ROWS // 2, 0), rows - win)


def _na_bias(rpb, rows):
    nb = rows // NA_QROWS
    win = NA_QROWS + NA_ROWS - 1
    kr = min(NA_ROWS, rows)
    n_rel_r, n_rel_c = 2 * NA_ROWS - 1, 2 * NA_COLS - 1
    col = np.arange(GRID_W)
    c0 = np.clip(col - NA_COLS // 2, 0, GRID_W - NA_COLS)
    valid_c = (col[None, :] >= c0[:, None]) & (col[None, :] < c0[:, None] + NA_COLS)
    rel_c = np.clip(col[None, :] - col[:, None] + NA_COLS - 1, 0, n_rel_c - 1)
    pick_c = (rel_c[..., None] == np.arange(n_rel_c)).astype(np.float32)
    pick_r, valid = [], []
    for rb in (0, 1, nb - 1):
        r_abs = NA_QROWS * rb + np.arange(NA_QROWS)
        r0 = np.clip(r_abs - kr // 2, 0, rows - kr)
        k_abs = _na_window(rb, rows) + np.arange(win)
        valid_r = (k_abs[None, :] >= r0[:, None]) & (k_abs[None, :] < r0[:, None] + kr)
        rel_r = np.clip(k_abs[None, :] - r_abs[:, None] + NA_ROWS - 1, 0, n_rel_r - 1)
        pick_r.append((rel_r[..., None] == np.arange(n_rel_r)).astype(np.float32))
        valid.append(valid_r[:, None, :, None] & valid_c[None, :, None, :])
    rpb2 = rpb.reshape(rpb.shape[0] // 2, 2, n_rel_r, n_rel_c)
    rows_sel = jnp.einsum('peab,tqwa->pteqwb', rpb2, jnp.asarray(np.stack(pick_r)), precision=lax.Precision.HIGHEST)
    bias = jnp.einsum('pteqwb,ckb->pteqcwk', rows_sel, jnp.asarray(pick_c), precision=lax.Precision.HIGHEST)
    bias = jnp.where(jnp.asarray(np.stack(valid))[None, :, None], bias, NEG_INF)
    return bias.reshape(rpb.shape[0] // 2, 3, 2 * NA_QROWS * GRID_W, win * GRID_W)


def _na_kernel(q_ref, k_ref, v_ref, bias_ref, o_ref, vx_ref, *, n_ctx, rows):
    hd = LANES // 2
    scale = hd ** -0.5
    nq = NA_QROWS * GRID_W
    nb = rows // NA_QROWS
    nk = (NA_QROWS + NA_ROWS - 1) * GRID_W
    first = lax.broadcasted_iota(jnp.int32, (1, LANES), 1) < hd
    vx_ref[:, 0:LANES] = v_ref[0]
    vx_ref[:, LANES:2 * LANES] = jnp.ones((vx_ref.shape[0], LANES), BF16)

    def attend(qb, ks, bias_ty):
        n = qb.shape[0]
        qb = qb * scale
        zero = jnp.zeros_like(qb)
        q2 = jnp.concatenate([jnp.where(first, qb, zero), jnp.where(first, zero, qb)], axis=0)
        s_ctx = _dot_nt(q2, k_ref[0, 0:n_ctx, :])
        m = jnp.max(s_ctx, axis=-1, keepdims=True)
        if ks is not None:
            s_loc = _dot_nt(q2, k_ref[0, ks:ks + nk, :]) + bias_ref[0, bias_ty]
            m = jnp.maximum(m, jnp.max(s_loc, axis=-1, keepdims=True))
        o = _dot(jnp.exp(s_ctx - m).astype(BF16), vx_ref[0:n_ctx, :])
        if ks is not None:
            o = o + _dot(jnp.exp(s_loc - m).astype(BF16), vx_ref[ks:ks + nk, :])
        o = o[:, 0:LANES] * (1.0 / o[:, LANES:2 * LANES])
        return jnp.where(first, o[0:n], o[n:2 * n])

    o_ref[0, 0:n_ctx, :] = attend(q_ref[0, 0:n_ctx, :], None, None).astype(o_ref.dtype)
    for rb in range(nb):
        qs = n_ctx + rb * nq
        ks = n_ctx + _na_window(rb, rows) * GRID_W
        ty = 0 if rb == 0 else (2 if rb == nb - 1 else 1)
        o_ref[0, qs:qs + nq, :] = attend(q_ref[0, qs:qs + nq, :], ks, ty).astype(o_ref.dtype)


def _na(qkv, rpb, n_ctx):
    b, p, d3 = qkv.shape
    d = d3 // 3
    rows = (p - n_ctx) // GRID_W
    assert d // NA_HEADS == LANES // 2 and rows % NA_QROWS == 0 and rows // NA_QROWS >= 3
    nhp = d // LANES
    bias = _na_bias(rpb.astype(F32), rows)
    nq, nk = bias.shape[2] // 2, bias.shape[3]
    kern = functools.partial(_na_kernel, n_ctx=n_ctx, rows=rows)
    return pl.pallas_call(
        kern,
        grid=(nhp, b),
        in_specs=[pl.BlockSpec((1, p, LANES), lambda hp, bi: (bi, 0, hp)),
                  pl.BlockSpec((1, p, LANES), lambda hp, bi: (bi, 0, nhp + hp)),
                  pl.BlockSpec((1, p, LANES), lambda hp, bi: (bi, 0, 2 * nhp + hp)),
                  pl.BlockSpec((1, 3, 2 * nq, nk), lambda hp, bi: (hp, 0, 0, 0))],
        out_specs=pl.BlockSpec((1, p, LANES), lambda hp, bi: (bi, 0, hp)),
        out_shape=jax.ShapeDtypeStruct((b, p, d), BF16),
        scratch_shapes=[pltpu.VMEM((p, 2 * LANES), BF16)],
        compiler_params=_params(("parallel", "parallel")),
        name="nbr_attn",
    )(qkv, qkv, qkv, bias)


def _ret_tables(n_ctx, s, dk, ch):
    quarter = dk // 4
    pos = jnp.arange(s)
    inv = ROPE_BASE ** (-jnp.arange(quarter, dtype=F32) / quarter)
    ang_r = (pos // GRID_W).astype(F32)[:, None] * inv
    ang_c = (pos % GRID_W).astype(F32)[:, None] * inv
    cos = jnp.concatenate([jnp.cos(ang_r)] * 2 + [jnp.cos(ang_c)] * 2, axis=-1)
    sin = jnp.concatenate([-jnp.sin(ang_r), jnp.sin(ang_r), -jnp.sin(ang_c), jnp.sin(ang_c)], axis=-1)
    cos = jnp.concatenate([jnp.ones((n_ctx, dk), F32), cos], axis=0)
    sin = jnp.concatenate([jnp.zeros((n_ctx, dk), F32), sin], axis=0)
    log_gamma = jnp.log1p(-(2.0 ** (-5.0 - jnp.arange(RET_HEADS, dtype=F32))))[:, None, None]
    pq = jnp.arange(ch, dtype=F32)
    col = jnp.broadcast_to(pq[:, None], (ch, dk))[None]
    tabs = jnp.stack([
        jnp.exp(jnp.abs(pq[:, None] - pq[None, :])[None] * log_gamma),
        jnp.exp((col + 1.0) * log_gamma),
        jnp.exp((ch - col) * log_gamma),
        jnp.exp((ch - 1.0 - col) * log_gamma),
        jnp.exp(col * log_gamma),
    ], axis=1)
    chunk_decay = jnp.exp(ch * log_gamma[:, 0, 0])
    return cos, sin, tabs, chunk_decay


def _ret_kernel(cd_ref, q_ref, k_ref, v_ref, g_ref, cos_ref, sin_ref, tab_ref, o_ref,
                sb_ref, st_ref, *, ch, nc):
    dk = q_ref.shape[-1]
    cd = cd_ref[pl.program_id(1)]
    k_scale = dk ** -0.5

    def rope(t, c0):
        parts = []
        for hf in range(dk // LANES):
            sl = slice(hf * LANES, (hf + 1) * LANES)
            th = t[:, sl]
            parts.append(th * cos_ref[pl.ds(c0, ch), sl] + pltpu.roll(th, LANES // 2, 1) * sin_ref[pl.ds(c0, ch), sl])
        return jnp.concatenate(parts, axis=-1)

    def load_k(c0):
        return rope(k_ref[0, pl.ds(c0, ch), :].astype(F32) * k_scale, c0)

    def kv_outer(kdec, c0):
        return _dot_tn(kdec.astype(BF16), v_ref[0, pl.ds(c0, ch), :])

    nz = 1
    sb_ref[0] = jnp.zeros(sb_ref.shape[1:], BF16)
    st_ref[...] = kv_outer(load_k(0) * tab_ref[0, 4], 0)

    def bwd(j, _):
        c = nc - 1 - j
        c0 = pl.multiple_of(c * ch, ch)
        sb_ref[c] = st_ref[...].astype(BF16)
        st_ref[...] = cd * st_ref[...] + kv_outer(load_k(c0) * tab_ref[0, 4], c0)
        return 0

    lax.fori_loop(0, nc - 1 - nz, bwd, 0, unroll=True)
    sb_ref[nz] = st_ref[...].astype(BF16)

    st_ref[...] = jnp.zeros(st_ref.shape, F32)

    def fwd(c, _):
        c0 = pl.multiple_of(c * ch, ch)
        q = rope(q_ref[0, pl.ds(c0, ch), :].astype(F32), c0)
        k = load_k(c0)
        v = v_ref[0, pl.ds(c0, ch), :]
        inner = _dot_nt(q.astype(BF16), k.astype(BF16)) * tab_ref[0, 0]
        o = (_dot(inner.astype(BF16), v)
             + _dot((q * tab_ref[0, 1]).astype(BF16), st_ref[...].astype(BF16))
             + _dot((q * tab_ref[0, 2]).astype(BF16), sb_ref[c]))
        st_ref[...] = cd * st_ref[...] + kv_outer(k * tab_ref[0, 3], c0)
        o = o * lax.rsqrt(jnp.mean(o * o, axis=-1, keepdims=True) + RMS_EPS)
        g = g_ref[0, pl.ds(c0, ch), :].astype(F32)
        o_ref[0, pl.ds(c0, ch), :] = (o * (g * jax.nn.sigmoid(g))).astype(o_ref.dtype)
        return 0

    lax.fori_loop(0, nc, fwd, 0, unroll=3 if nc % 3 == 0 else 1)


def _ret(qkvg, n_ctx):
    b, p, d6 = qkvg.shape
    d = d6 // 6
    dk = d // RET_HEADS
    dv = 2 * dk
    ch = dk
    assert dk == 2 * LANES and n_ctx == ch and p % ch == 0
    nc = p // ch
    cos, sin, tabs, chunk_decay = _ret_tables(n_ctx, p - n_ctx, dk, ch)
    kern = functools.partial(_ret_kernel, ch=ch, nc=nc)
    nh = RET_HEADS
    return pl.pallas_call(
        kern,
        grid_spec=pltpu.PrefetchScalarGridSpec(
            num_scalar_prefetch=1,
            grid=(b, nh),
            in_specs=[pl.BlockSpec((1, p, dk), lambda bi, h, cd: (bi, 0, h)),
                      pl.BlockSpec((1, p, dk), lambda bi, h, cd: (bi, 0, nh + h)),
                      pl.BlockSpec((1, p, dv), lambda bi, h, cd: (bi, 0, nh + h)),
                      pl.BlockSpec((1, p, dv), lambda bi, h, cd: (bi, 0, 2 * nh + h)),
                      pl.BlockSpec((p, dk), lambda bi, h, cd: (0, 0)),
                      pl.BlockSpec((p, dk), lambda bi, h, cd: (0, 0)),
                      pl.BlockSpec((1, 5, ch, dk), lambda bi, h, cd: (h, 0, 0, 0))],
            out_specs=pl.BlockSpec((1, p, dv), lambda bi, h, cd: (bi, 0, h)),
            scratch_shapes=[pltpu.VMEM((nc, dk, dv), BF16), pltpu.VMEM((dk, dv), F32)]),
        out_shape=jax.ShapeDtypeStruct((b, p, nh * dv), BF16),
        compiler_params=_params(("parallel", "parallel")),
        name="retention",
    )(chunk_decay, qkvg, qkvg, qkvg, qkvg, cos, sin, tabs)


def _router_kernel(lg_ref, rb_ref, tri_ref, e_ref, w_ref, rk_ref, cnt_ref, run_ref):
    @pl.when(pl.program_id(0) == 0)
    def _():
        run_ref[...] = jnp.zeros(run_ref.shape, F32)

    lg = lg_ref[...]
    ex = jnp.exp(lg - jnp.max(lg, axis=0, keepdims=True))
    probs = ex / jnp.sum(ex, axis=0, keepdims=True)
    sel = probs + rb_ref[...]
    epg = EXPERTS_PER_GROUP
    best = grp = cur = curp = None
    for g in range(N_GROUPS):
        s = [sel[g * epg + i:g * epg + i + 1, :] for i in range(epg)]
        pr = [probs[g * epg + i:g * epg + i + 1, :] for i in range(epg)]
        top2 = None
        for i in range(epg):
            for j in range(i + 1, epg):
                top2 = s[i] + s[j] if top2 is None else jnp.maximum(top2, s[i] + s[j])
        if g == 0:
            best, grp, cur, curp = top2, jnp.zeros(top2.shape, jnp.int32), s, pr
        else:
            better = top2 > best
            best = jnp.where(better, top2, best)
            grp = jnp.where(better, g, grp)
            cur = [jnp.where(better, s[i], cur[i]) for i in range(epg)]
            curp = [jnp.where(better, pr[i], curp[i]) for i in range(epg)]
    b1, i1, p1 = cur[0], jnp.zeros(best.shape, jnp.int32), curp[0]
    for i in range(1, epg):
        gt = cur[i] > b1
        b1, i1, p1 = jnp.where(gt, cur[i], b1), jnp.where(gt, i, i1), jnp.where(gt, curp[i], p1)
    b2 = i2 = p2 = None
    for i in range(epg):
        v = jnp.where(i1 == i, -jnp.inf, cur[i])
        if b2 is None:
            b2, i2, p2 = v, jnp.zeros(best.shape, jnp.int32), curp[0]
        else:
            gt = v > b2
            b2, i2, p2 = jnp.where(gt, v, b2), jnp.where(gt, i, i2), jnp.where(gt, curp[i], p2)
    e1 = grp * epg + i1
    e2 = grp * epg + i2
    inv = 1.0 / (p1 + p2)
    e_ref[0:1, :] = e1
    e_ref[1:2, :] = e2
    w_ref[0:1, :] = p1 * inv
    w_ref[1:2, :] = p2 * inv

    eidx = lax.broadcasted_iota(jnp.int32, lg.shape, 0)
    eq1 = eidx == e1
    eq2 = eidx == e2
    member = jnp.where(eq1, 1.0, jnp.where(eq2, 1.0, 0.0))
    before = _dot(member.astype(BF16), tri_ref[...]) + run_ref[...]
    rk_ref[0:1, :] = jnp.sum(jnp.where(eq1, before, 0.0), axis=0, keepdims=True).astype(jnp.int32)
    rk_ref[1:2, :] = jnp.sum(jnp.where(eq2, before, 0.0), axis=0, keepdims=True).astype(jnp.int32)
    run_ref[...] = run_ref[...] + jnp.sum(member, axis=1, keepdims=True)
    cnt_ref[...] = run_ref[...]


def _router(logits_t, router_b):
    e, n = logits_t.shape
    tt = _pick(n, (1024, 512))
    tri = (jnp.arange(tt)[:, None] < jnp.arange(tt)[None, :]).astype(BF16)
    kn = jax.ShapeDtypeStruct((TOP_K, n), jnp.int32)
    return pl.pallas_call(
        _router_kernel,
        grid=(n // tt,),
        in_specs=[pl.BlockSpec((e, tt), lambda i: (0, i)),
                  pl.BlockSpec((e, 1), lambda i: (0, 0)),
                  pl.BlockSpec((tt, tt), lambda i: (0, 0))],
        out_specs=[pl.BlockSpec((TOP_K, tt), lambda i: (0, i)),
                   pl.BlockSpec((TOP_K, tt), lambda i: (0, i)),
                   pl.BlockSpec((TOP_K, tt), lambda i: (0, i)),
                   pl.BlockSpec((e, 1), lambda i: (0, 0))],
        out_shape=[kn, jax.ShapeDtypeStruct((TOP_K, n), F32), kn, jax.ShapeDtypeStruct((e, 1), F32)],
        scratch_shapes=[pltpu.VMEM((e, 1), F32)],
        compiler_params=_params(("arbitrary",)),
        name="router",
    )(logits_t, router_b.astype(F32).reshape(e, 1), tri)


def _moe_kernel(be_ref, nu_ref, xs_ref, wg_ref, wu_ref, wd_ref, ys_ref, wgb, wub, wdb, *, fc):
    i = pl.program_id(0)
    used = i < nu_ref[0]
    new_expert = jnp.logical_or(i == 0, be_ref[i] != be_ref[jnp.maximum(i - 1, 0)])

    @pl.when(jnp.logical_and(used, new_expert))
    def _():
        wgb[...] = wg_ref[0, 0].astype(BF16)
        wub[...] = wu_ref[0, 0].astype(BF16)
        wdb[...] = wd_ref[0, 0].astype(BF16)

    @pl.when(used)
    def _():
        x = xs_ref[...]
        f = wgb.shape[-1]
        acc = jnp.zeros(ys_ref.shape, F32)
        for c in range(f // fc):
            sl = slice(c * fc, (c + 1) * fc)
            g = _dot(x, wgb[:, sl])
            u = _dot(x, wub[:, sl])
            a = (g * jax.nn.sigmoid(g) * u).astype(BF16)
            acc = acc + _dot(a, wdb[sl, :])
        ys_ref[...] = acc.astype(ys_ref.dtype)

    @pl.when(jnp.logical_not(used))
    def _():
        ys_ref[...] = jnp.zeros(ys_ref.shape, ys_ref.dtype)


def _moe_experts(xs, block_expert, n_used, w_gate, w_up, w_down, layer):
    n_rows, d = xs.shape
    f = w_gate.shape[-1]
    bm = MOE_BLOCK
    kern = functools.partial(_moe_kernel, fc=_pick(f, (512, 256, 128)))
    return pl.pallas_call(
        kern,
        grid_spec=pltpu.PrefetchScalarGridSpec(
            num_scalar_prefetch=2,
            grid=(n_rows // bm,),
            in_specs=[pl.BlockSpec((bm, d), lambda i, be, nu: (i, 0)),
                      pl.BlockSpec((1, 1, d, f), lambda i, be, nu: (layer, be[i], 0, 0)),
                      pl.BlockSpec((1, 1, d, f), lambda i, be, nu: (layer, be[i], 0, 0)),
                      pl.BlockSpec((1, 1, f, d), lambda i, be, nu: (layer, be[i], 0, 0))],
            out_specs=pl.BlockSpec((bm, d), lambda i, be, nu: (i, 0)),
            scratch_shapes=[pltpu.VMEM((d, f), BF16), pltpu.VMEM((d, f), BF16), pltpu.VMEM((f, d), BF16)]),
        out_shape=jax.ShapeDtypeStruct((n_rows, d), BF16),
        compiler_params=_params(("arbitrary",)),
        name="moe_experts",
    )(block_expert, n_used, xs, w_gate, w_up, w_down)


def _moe(h_rows, tok_ids, logits_t, router_b, w_gate, w_up, w_down, layer):
    n = tok_ids.shape[0]
    bm = MOE_BLOCK
    expert, weight, rank, counts = _router(logits_t, router_b)
    counts = counts[:, 0].astype(jnp.int32)
    padded = (counts + bm - 1) // bm * bm
    pad_end = jnp.cumsum(padded)
    pad_start = pad_end - padded
    dest = rank + jnp.sum(jnp.where(expert[..., None] == jnp.arange(N_EXPERTS), pad_start, 0), axis=-1)
    n_blocks = -(-(n * TOP_K) // bm) + N_EXPERTS
    n_rows = n_blocks * bm
    tok = tok_ids.astype(jnp.int32)
    src = (jnp.arange(n_rows, dtype=jnp.int32) % h_rows.shape[0]).at[dest.reshape(-1)].set(
        jnp.tile(tok, TOP_K), unique_indices=True, mode='promise_in_bounds')
    block_start = jnp.arange(n_blocks, dtype=jnp.int32) * bm
    block_expert = jnp.minimum(
        jnp.sum((pad_end[None, :] <= block_start[:, None]).astype(jnp.int32), axis=1), N_EXPERTS - 1)
    n_used = (pad_end[-1] // bm).astype(jnp.int32).reshape(1)
    ys = _moe_experts(h_rows[src], block_expert, n_used, w_gate, w_up, w_down, layer)
    return ys[dest.reshape(-1)], weight


def _final_kernel(x_ref, y0_ref, y1_ref, gw_ref, mod_ref, g_ref, o_ref):
    gate = mod_ref[0, 2 * N_MOD - 1:2 * N_MOD, :]
    xn = x_ref[0] + gate * _ffn_sum(y0_ref, y1_ref, gw_ref, slice(None))
    o_ref[0] = xn * lax.rsqrt(jnp.mean(xn * xn, axis=-1, keepdims=True) + RMS_EPS) * g_ref[...]


def _final(t, ys, gw, mod, g, n_ctx):
    b, p, d = t.shape
    rows = p - n_ctx
    tm = _pick(n_ctx, (512, 256))
    assert rows % tm == 0
    off = n_ctx // tm
    row_spec = pl.BlockSpec((1, tm, d), lambda bi, i: (bi, i, 0))
    return pl.pallas_call(
        _final_kernel,
        grid=(b, rows // tm),
        in_specs=[pl.BlockSpec((1, tm, d), lambda bi, i: (bi, i + off, 0)),
                  pl.BlockSpec((None, 1, tm, d), lambda bi, i: (0, bi, i, 0)),
                  pl.BlockSpec((None, 1, tm, d), lambda bi, i: (1, bi, i, 0)),
                  pl.BlockSpec((1, tm, TOP_K), lambda bi, i: (bi, i, 0)),
                  pl.BlockSpec((1, 2 * N_MOD, d), lambda bi, i: (bi, 0, 0)),
                  pl.BlockSpec((1, d), lambda bi, i: (0, 0))],
        out_specs=row_spec,
        out_shape=jax.ShapeDtypeStruct((b, rows, d), F32),
        compiler_params=_params(("parallel", "parallel")),
        name="ffn_final",
    )(t, ys, ys, gw, mod, g.reshape(1, d))


def kernel(x, c, ctx, c_ctx, ada_w, ada_b, norm_mix_g, norm_ffn_g, final_norm_g, lru_w_in, lru_conv_w, lru_conv_b, lru_gate_w, lru_gate_b, lru_lambda, lru_w_out, na_w_qkv, na_rpb, na_w_o, ret_w_qkvg, ret_w_o, router_w, router_b, moe_w_gate, moe_w_up, moe_w_down):
    b, s, d = x.shape
    n_ctx = ctx.shape[1]
    p = n_ctx + s
    depth = ada_w.shape[0]
    t = jnp.concatenate([ctx, x], axis=1)

    pad_rows = -(b + 1) % 16
    cc = jnp.concatenate([c, c_ctx[None, :], jnp.zeros((pad_rows, d), F32)], axis=0)
    mods = _ada(cc, ada_w, ada_b)
    mod_x = mods[:, :b].reshape(depth, b, N_MOD, d)
    mod_z = jnp.broadcast_to(mods[:, b].reshape(depth, 1, N_MOD, d), (depth, b, N_MOD, d))
    mods = jnp.concatenate([mod_z, mod_x], axis=2)

    rw32 = router_w.astype(F32)
    rw_hi = rw32.astype(BF16)
    rw_lo = (rw32 - rw_hi.astype(F32)).astype(BF16)
    rw = jnp.concatenate([rw_hi, rw_lo, jnp.zeros((d, ROUTER_LANES - 2 * N_EXPERTS), BF16)], axis=1)

    all_rows = jnp.arange(b * p, dtype=jnp.int32)
    pending = None
    for i in range(depth):
        kind, j = i % 3, i // 3
        last = i == depth - 1
        mod = _at(mods, i)
        if kind == 0:
            proj, t = _norm_proj(t, _at(norm_mix_g, i), mod, _at(lru_w_in, j).astype(BF16), n_ctx,
                                 act_cols=d, pending=pending)
            a = _lru(proj, _at(lru_conv_w, j), _at(lru_conv_b, j), _at(lru_gate_w, j), _at(lru_gate_b, j),
                     _at(lru_lambda, j), n_ctx)
            w_o = _at(lru_w_out, j)
        elif kind == 1:
            qkv, t = _norm_proj(t, _at(norm_mix_g, i), mod, _at(na_w_qkv, j).astype(BF16), n_ctx, pending=pending)
            a = _na(qkv, _at(na_rpb, j), n_ctx)
            w_o = _at(na_w_o, j)
        else:
            qkvg, t = _norm_proj(t, _at(norm_mix_g, i), mod, _at(ret_w_qkvg, j).astype(BF16), n_ctx, pending=pending)
            a = _ret(qkvg, n_ctx)
            w_o = _at(ret_w_o, j)
        t, h2, pl_ = _out_proj(a, w_o.astype(BF16), t, mod, _at(norm_ffn_g, i), rw, n_ctx)
        logits = (pl_[..., :N_EXPERTS] + pl_[..., N_EXPERTS:2 * N_EXPERTS])
        if last:
            tok_ids = all_rows.reshape(b, p)[:, n_ctx:].reshape(-1)
            logits = logits[:, n_ctx:]
        else:
            tok_ids = all_rows
        ys, wts = _moe(h2.reshape(b * p, d), tok_ids, logits.reshape(-1, N_EXPERTS).T, router_b,
                       moe_w_gate, moe_w_up, moe_w_down, i)
        ys = ys.reshape(TOP_K, b, -1, d)
        wts = wts.T.reshape(b, -1, TOP_K)
        pending = (ys, wts, mod)
    return _final(t, ys, wts, mod, final_norm_g, n_ctx)
```
